```python
import math
import jax, jax.numpy as jnp
from jax import lax
import numpy as np

D_MODEL = 1024
BATCH = 8
SEQ = 2048
DEPTH = 4
DEC_BATCH = 128
DEC_SEQ = 4
PAST_LEN = 8192
PAGE_SIZE = 128

MIX_WIDTH = D_MODEL
A_WIDTH = MIX_WIDTH // 2
B_WIDTH = MIX_WIDTH - A_WIDTH
H_A = 4
DK_A = A_WIDTH // H_A
DV_A = A_WIDTH // H_A
CONV_WIDTH = 4
CONV_DIM = 2 * H_A * DK_A + H_A * DV_A
CHUNK = 64
HD_B = 64
H_QB = B_WIDTH // HD_B
H_KVB = 2
GQA_GROUP = H_QB // H_KVB
WINDOW = 128
D_FF = -(-8 * D_MODEL // (3 * 256)) * 256
ALPHA = (2 * DEPTH) ** 0.25
BETA_INIT = (8 * DEPTH) ** -0.25
EPS = 1e-6
PROJ_SIZES = (CONV_DIM, H_A * DV_A, H_A, H_A, H_QB * HD_B, H_KVB * HD_B, H_KVB * HD_B)
PROJ_SPLITS = tuple(int(s) for s in np.cumsum(PROJ_SIZES)[:-1])
PROJ_WIDTH = int(sum(PROJ_SIZES))

kernel_name = "hymba_gdn_swa_sink_deepnorm_step"


def layernorm(x, g, b):
    xf = x.astype(jnp.float32)
    mu = xf.mean(-1, keepdims=True)
    var = jnp.square(xf - mu).mean(-1, keepdims=True)
    return ((xf - mu) * lax.rsqrt(var + EPS) * g.astype(jnp.float32) + b.astype(jnp.float32)).astype(x.dtype)


def l2norm(t):
    return t * lax.rsqrt(jnp.sum(t * t, axis=-1, keepdims=True) + EPS)


def rmsnorm_gated(o, w, z):
    of = o.astype(jnp.float32)
    of = of * lax.rsqrt(jnp.mean(of * of, axis=-1, keepdims=True) + EPS) * w.astype(jnp.float32)
    return (of * jax.nn.silu(z.astype(jnp.float32))).astype(z.dtype)


def causal_conv(x, buf, w):
    L = x.shape[1]
    xc = jnp.concatenate([buf, x], axis=1)
    y = xc[:, 0:L] * w[0]
    for j in range(1, CONV_WIDTH):
        y = y + xc[:, j:j + L] * w[j]
    return jax.nn.silu(y), xc[:, -(CONV_WIDTH - 1):]


def gated_delta_rule(q, k, v, g, beta, S0):
    Bn, L, H, DK = q.shape
    DV = v.shape[-1]
    C = math.gcd(L, CHUNK)
    N = L // C

    def chunks(t):
        t = t.reshape((Bn, N, C, H) + t.shape[3:])
        return jnp.swapaxes(t, 2, 3)

    qc, kc, vc, gc_, bc = chunks(q), chunks(k), chunks(v), chunks(g), chunks(beta)
    gc = jnp.cumsum(gc_, axis=-1)
    ii = jnp.arange(C)[:, None]
    jj = jnp.arange(C)[None, :]
    causal = ii >= jj
    strict = ii > jj
    decay = jnp.exp(jnp.where(causal, gc[..., :, None] - gc[..., None, :], -jnp.inf))
    kb = kc * bc[..., None]
    A = jnp.where(strict, jnp.einsum('bnhid,bnhjd->bnhij', kb, kc) * decay, 0.0)
    M = A + jnp.eye(C, dtype=A.dtype)
    rhs = jnp.concatenate([vc * bc[..., None], kb * jnp.exp(gc)[..., None]], axis=-1)
    sol = lax.linalg.triangular_solve(M, rhs, left_side=True, lower=True, unit_diagonal=True)
    u, w = sol[..., :DV], sol[..., DV:]
    qk = jnp.einsum('bnhid,bnhjd->bnhij', qc, kc) * decay
    q_dec = qc * jnp.exp(gc)[..., None]
    k_dec = kc * jnp.exp(gc[..., -1:] - gc)[..., None]
    g_last = jnp.exp(gc[..., -1])

    def step(S, inp):
        u_n, w_n, qk_n, qd_n, kd_n, gl_n = inp
        v_new = u_n - jnp.einsum('bhcd,bhde->bhce', w_n, S)
        o_n = jnp.einsum('bhcd,bhde->bhce', qd_n, S) + jnp.einsum('bhij,bhje->bhie', qk_n, v_new)
        S = S * gl_n[..., None, None] + jnp.einsum('bhcd,bhce->bhde', kd_n, v_new)
        return S, o_n

    xs = tuple(jnp.moveaxis(t, 1, 0) for t in (u, w, qk, q_dec, k_dec, g_last))
    S_final, o = lax.scan(step, S0, xs)
    o = jnp.transpose(o, (1, 0, 3, 2, 4)).reshape(Bn, L, H, DV)
    return o, S_final


def swa_sinks(q, k, v, k_pre, v_pre, prefix_len, sinks):
    Bn, L = q.shape[:2]
    QB = math.gcd(L, WINDOW)
    N = L // QB
    KB = WINDOW + QB
    k_ext = jnp.concatenate([k_pre, k], axis=1)
    v_ext = jnp.concatenate([v_pre, v], axis=1)
    idx = jnp.arange(N)[:, None] * QB + jnp.arange(KB)[None, :]
    kbk = k_ext[:, idx]
    vbk = v_ext[:, idx]
    qb = q.reshape(Bn, N, QB, H_KVB, GQA_GROUP, HD_B)
    s = jnp.einsum('bnqhgd,bnkhd->bnhgqk', qb, kbk).astype(jnp.float32) * (HD_B ** -0.5)
    r = jnp.arange(QB)[:, None]
    c = jnp.arange(KB)[None, :]
    band = (c > r) & (c <= r + WINDOW)
    mask = band[None] & (idx >= WINDOW - prefix_len)[:, None, :]
    s = jnp.where(mask[None, :, None, None], s, -jnp.inf)
    sink = jnp.broadcast_to(sinks.astype(jnp.float32).reshape(1, 1, H_KVB, GQA_GROUP, 1, 1),
                            s.shape[:-1] + (1,))
    p = jax.nn.softmax(jnp.concatenate([s, sink], axis=-1), axis=-1)[..., :KB]
    o = jnp.einsum('bnhgqk,bnkhd->bnqhgd', p.astype(v.dtype), vbk).reshape(Bn, L, H_QB * HD_B)
    return o, k_ext[:, -WINDOW:], v_ext[:, -WINDOW:]


def token_mix(h, S0, conv_buf, k_pre, v_pre, prefix_len,
              w_in, conv_w, a_log, dt_bias, norm_a_w, sinks, w_out):
    Bn, L, _ = h.shape
    proj = jnp.einsum('bld,dp->blp', h, w_in)
    qkv_a, z_a, b_a, a_a, q_b, k_b, v_b = jnp.split(proj, PROJ_SPLITS, axis=-1)
    qkv_c, new_conv = causal_conv(qkv_a, conv_buf, conv_w)
    qc, kc, vc = jnp.split(qkv_c.astype(jnp.float32), (H_A * DK_A, 2 * H_A * DK_A), axis=-1)
    q = l2norm(qc.reshape(Bn, L, H_A, DK_A)) * (DK_A ** -0.5)
    k = l2norm(kc.reshape(Bn, L, H_A, DK_A))
    v = vc.reshape(Bn, L, H_A, DV_A)
    beta = jax.nn.sigmoid(b_a.astype(jnp.float32))
    g = -jnp.exp(a_log.astype(jnp.float32)) * jax.nn.softplus(a_a.astype(jnp.float32) + dt_bias.astype(jnp.float32))
    o_a, S_new = gated_delta_rule(q, k, v, g, beta, S0.astype(jnp.float32))
    o_a = rmsnorm_gated(o_a, norm_a_w, z_a.reshape(Bn, L, H_A, DV_A))
    o_b, k_new, v_new = swa_sinks(q_b.reshape(Bn, L, H_QB, HD_B), k_b.reshape(Bn, L, H_KVB, HD_B),
                                  v_b.reshape(Bn, L, H_KVB, HD_B), k_pre, v_pre, prefix_len, sinks)
    mixed = jnp.concatenate([o_a.reshape(Bn, L, A_WIDTH).astype(h.dtype), o_b.astype(h.dtype)], axis=-1)
    out = jnp.einsum('blm,md->bld', mixed, w_out)
    return out, S_new.astype(S0.dtype), new_conv, k_new, v_new


def swiglu(h, w_ffn_in, w_ffn_out):
    gate, up = jnp.split(jnp.einsum('bld,df->blf', h, w_ffn_in), 2, axis=-1)
    return jnp.einsum('blf,fd->bld', jax.nn.silu(gate) * up, w_ffn_out)


def setup_inputs(seed: int = 0) -> dict:
    key = jax.random.key(seed)
    ks = jax.random.split(key, 20)
    f32 = jnp.float32

    def nrm(k, shape, s):
        return jax.random.normal(k, shape, f32) * s

    swa_len = min(WINDOW, PAST_LEN)
    dt = jnp.exp(jax.random.uniform(ks[8], (DEPTH, H_A), f32, math.log(1e-3), math.log(1e-1)))
    return {
        "x_prompt": nrm(ks[0], (BATCH, SEQ, D_MODEL), 1.0),
        "x_sample": nrm(ks[1], (DEC_BATCH, DEC_SEQ, D_MODEL), 1.0),
        "state_delta": nrm(ks[2], (DEPTH, DEC_BATCH, H_A, DK_A, DV_A), 0.1),
        "state_conv": nrm(ks[3], (DEPTH, DEC_BATCH, CONV_WIDTH - 1, CONV_DIM), 1.0),
        "cache_swa_k": nrm(ks[4], (DEPTH, DEC_BATCH, swa_len, H_KVB, HD_B), 1.0),
        "cache_swa_v": nrm(ks[5], (DEPTH, DEC_BATCH, swa_len, H_KVB, HD_B), 1.0),
        "w_in": nrm(ks[6], (DEPTH, D_MODEL, PROJ_WIDTH), D_MODEL ** -0.5),
        "conv_w": nrm(ks[7], (DEPTH, CONV_WIDTH, CONV_DIM), CONV_WIDTH ** -0.5),
        "a_log": jnp.log(jax.random.uniform(ks[9], (DEPTH, H_A), f32, 1.0, 16.0)),
        "dt_bias": dt + jnp.log(-jnp.expm1(-dt)),
        "norm_a_w": 1.0 + nrm(ks[10], (DEPTH, DV_A), 0.02),
        "sinks": nrm(ks[11], (DEPTH, H_QB), 1.0),
        "w_out": nrm(ks[12], (DEPTH, MIX_WIDTH, D_MODEL), BETA_INIT * MIX_WIDTH ** -0.5),
        "ln1_g": 1.0 + nrm(ks[13], (DEPTH, D_MODEL), 0.02),
        "ln1_b": nrm(ks[14], (DEPTH, D_MODEL), 0.02),
        "w_ffn_in": nrm(ks[15], (DEPTH, D_MODEL, 2 * D_FF), D_MODEL ** -0.5),
        "w_ffn_out": nrm(ks[16], (DEPTH, D_FF, D_MODEL), BETA_INIT * D_FF ** -0.5),
        "ln2_g": 1.0 + nrm(ks[17], (DEPTH, D_MODEL), 0.02),
        "ln2_b": nrm(ks[18], (DEPTH, D_MODEL), 0.02),
    }


def reference(x_prompt, x_sample, state_delta, state_conv, cache_swa_k, cache_swa_v,
              w_in, conv_w, a_log, dt_bias, norm_a_w, sinks, w_out,
              ln1_g, ln1_b, w_ffn_in, w_ffn_out, ln2_g, ln2_b):
    xp, xs = x_prompt, x_sample
    Bp = xp.shape[0]
    dt_ = xp.dtype
    S_zero = jnp.zeros((Bp, H_A, DK_A, DV_A), state_delta.dtype)
    conv_zero = jnp.zeros((Bp, CONV_WIDTH - 1, CONV_DIM), dt_)
    kv_zero = jnp.zeros((Bp, WINDOW, H_KVB, HD_B), dt_)
    sample_prefix = cache_swa_k.shape[2]
    Sp, Cp, Kp, Vp, Ss, Cs, Ks, Vs = [], [], [], [], [], [], [], []
    for l in range(DEPTH):
        lw = (w_in[l], conv_w[l], a_log[l], dt_bias[l], norm_a_w[l], sinks[l], w_out[l])
        mp, s1, c1, k1, v1 = token_mix(xp, S_zero, conv_zero, kv_zero, kv_zero, 0, *lw)
        xp = layernorm(ALPHA * xp + mp, ln1_g[l], ln1_b[l])
        xp = layernorm(ALPHA * xp + swiglu(xp, w_ffn_in[l], w_ffn_out[l]), ln2_g[l], ln2_b[l])
        ms, s2, c2, k2, v2 = token_mix(xs, state_delta[l], state_conv[l], cache_swa_k[l], cache_swa_v[l],
                                       sample_prefix, *lw)
        xs = layernorm(ALPHA * xs + ms, ln1_g[l], ln1_b[l])
        xs = layernorm(ALPHA * xs + swiglu(xs, w_ffn_in[l], w_ffn_out[l]), ln2_g[l], ln2_b[l])
        Sp.append(s1); Cp.append(c1); Kp.append(k1); Vp.append(v1)
        Ss.append(s2); Cs.append(c2); Ks.append(k2); Vs.append(v2)
    new_delta_p = jnp.stack(Sp)
    new_conv_p = jnp.stack(Cp)
    new_swa_k_p = jnp.stack(Kp)
    new_swa_v_p = jnp.stack(Vp)
    new_delta_s = jnp.stack(Ss)
    new_conv_s = jnp.stack(Cs)
    new_swa_k_s = jnp.stack(Ks)
    new_swa_v_s = jnp.stack(Vs)
    return (xp, xs, new_delta_p, new_conv_p, new_swa_k_p, new_swa_v_p,
            new_delta_s, new_conv_s, new_swa_k_s, new_swa_v_s)
```

```python
import functools
import math

import jax
import jax.numpy as jnp
from jax import lax
from jax.experimental import pallas as pl
from jax.experimental.pallas import tpu as pltpu

F32 = jnp.float32
BF16 = jnp.bfloat16

D_MODEL = 1024
H_A = 4
DK_A = 128
DV_A = 128
CONV_WIDTH = 4
CONV_DIM = 2 * H_A * DK_A + H_A * DV_A
A_WIDTH = H_A * DV_A
CHUNK = 64
HD_B = 64
H_QB = 8
H_KVB = 2
GQA_GROUP = H_QB // H_KVB
B_WIDTH = H_QB * HD_B
KV_WIDTH = H_KVB * HD_B
WINDOW = 128
D_FF = 2816
EPS = 1e-6

LANES = 128
SUBLANES = 8

COL_Z = CONV_DIM
COL_QB = COL_Z + A_WIDTH
COL_KB = COL_QB + B_WIDTH
COL_VB = COL_KB + KV_WIDTH
COL_BA = COL_VB + KV_WIDTH
PROJ_PAD = COL_BA + LANES

S_TILE = SUBLANES
ROW0 = CONV_WIDTH - 1
DEC_SEQ = 4
SB = 4

TM = 512
SEG = 512
FF_CHUNK = 256
NEG = -1e30
VMEM_LIMIT = 56 * 1024 * 1024


def _params(n_axes, vmem=None):
    return pltpu.CompilerParams(
        dimension_semantics=("arbitrary",) * n_axes,
        vmem_limit_bytes=vmem if vmem is not None else VMEM_LIMIT)


def _dot(a, b):
    return jnp.dot(a.astype(BF16), b.astype(BF16), preferred_element_type=F32)


def _dot_nt(a, b):
    return lax.dot_general(a.astype(BF16), b.astype(BF16), (((1,), (1,)), ((), ())),
                           preferred_element_type=F32)


def _sigmoid(x):
    return 1.0 / (1.0 + jnp.exp(-x))


def _softplus(x):
    return jnp.maximum(x, 0.0) + jnp.log(1.0 + jnp.exp(-jnp.abs(x)))


def _layernorm(h, g, b):
    mu = jnp.mean(h, axis=-1, keepdims=True)
    d = h - mu
    var = jnp.mean(d * d, axis=-1, keepdims=True)
    return d * lax.rsqrt(var + EPS) * g + b


def _in_proj_kernel(x_ref, w_ref, o_ref):
    xb = x_ref[...].astype(BF16)
    n = o_ref.shape[1]
    for n0 in range(0, n, 512):
        n1 = min(n0 + 512, n)
        o_ref[:, n0:n1] = jnp.dot(xb, w_ref[:, n0:n1], preferred_element_type=F32)


def _in_proj(x, w_all, layer):
    t = x.shape[0]
    return pl.pallas_call(
        _in_proj_kernel,
        name="in_proj",
        grid=(t // TM,),
        in_specs=[pl.BlockSpec((TM, D_MODEL), lambda i: (i, 0)),
                  pl.BlockSpec((None, D_MODEL, PROJ_PAD), lambda i: (layer, 0, 0))],
        out_specs=pl.BlockSpec((TM, PROJ_PAD), lambda i: (i, 0)),
        out_shape=jax.ShapeDtypeStruct((t, PROJ_PAD), F32),
        compiler_params=_params(1),
    )(x, w_all)


def _out_proj_ln_kernel(alpha, x_ref, oa_ref, ob_ref, w_ref, g_ref, b_ref, o_ref):
    m = (jnp.dot(oa_ref[...].astype(BF16), w_ref[0:A_WIDTH, :], preferred_element_type=F32)
         + jnp.dot(ob_ref[...].astype(BF16), w_ref[A_WIDTH:, :], preferred_element_type=F32))
    o_ref[...] = _layernorm(alpha * x_ref[...] + m, g_ref[...], b_ref[...])


def _out_proj_ln(x, oa, ob, w_all, g_all, b_all, layer, alpha):
    t = x.shape[0]
    return pl.pallas_call(
        functools.partial(_out_proj_ln_kernel, alpha),
        name="out_proj_ln",
        grid=(t // TM,),
        in_specs=[pl.BlockSpec((TM, D_MODEL), lambda i: (i, 0)),
                  pl.BlockSpec((TM, A_WIDTH), lambda i: (i, 0)),
                  pl.BlockSpec((TM, B_WIDTH), lambda i: (i, 0)),
                  pl.BlockSpec((None, D_MODEL, D_MODEL), lambda i: (layer, 0, 0)),
                  pl.BlockSpec((None, 1, D_MODEL), lambda i: (layer, 0, 0)),
                  pl.BlockSpec((None, 1, D_MODEL), lambda i: (layer, 0, 0))],
        out_specs=pl.BlockSpec((TM, D_MODEL), lambda i: (i, 0)),
        out_shape=jax.ShapeDtypeStruct((t, D_MODEL), F32),
        compiler_params=_params(1),
    )(x, oa, ob, w_all, g_all, b_all)


def _ffn_ln_kernel(alpha, x_ref, wi_ref, wo_ref, g_ref, b_ref, o_ref):
    x = x_ref[...]
    xb = x.astype(BF16)
    acc = alpha * x
    for c0 in range(0, D_FF, FF_CHUNK):
        gate = jnp.dot(xb, wi_ref[:, c0:c0 + FF_CHUNK], preferred_element_type=F32)
        up = jnp.dot(xb, wi_ref[:, D_FF + c0:D_FF + c0 + FF_CHUNK], preferred_element_type=F32)
        h = gate * _sigmoid(gate) * up
        acc = acc + jnp.dot(h.astype(BF16), wo_ref[c0:c0 + FF_CHUNK, :], preferred_element_type=F32)
    o_ref[...] = _layernorm(acc, g_ref[...], b_ref[...])


def _ffn_ln(x, wi_all, wo_all, g_all, b_all, layer, alpha):
    t = x.shape[0]
    return pl.pallas_call(
        functools.partial(_ffn_ln_kernel, alpha),
        name="ffn_ln",
        grid=(t // TM,),
        in_specs=[pl.BlockSpec((TM, D_MODEL), lambda i: (i, 0)),
                  pl.BlockSpec((None, D_MODEL, 2 * D_FF), lambda i: (layer, 0, 0)),
                  pl.BlockSpec((None, D_FF, D_MODEL), lambda i: (layer, 0, 0)),
                  pl.BlockSpec((None, 1, D_MODEL), lambda i: (layer, 0, 0)),
                  pl.BlockSpec((None, 1, D_MODEL), lambda i: (layer, 0, 0))],
        out_specs=pl.BlockSpec((TM, D_MODEL), lambda i: (i, 0)),
        out_shape=jax.ShapeDtypeStruct((t, D_MODEL), F32),
        compiler_params=_params(1),
    )(x, wi_all, wo_all, g_all, b_all)


def _conv_silu(x, cw):
    y = cw[3:4, :] * x
    for j in range(1, CONV_WIDTH):
        y = y + cw[CONV_WIDTH - 1 - j:CONV_WIDTH - j, :] * pltpu.roll(x, j, axis=0)
    return y * _sigmoid(y)


def _gates(ba, alog, dtb):
    beta = _sigmoid(ba)
    g = -jnp.exp(alog) * _softplus(ba + dtb)
    return beta, g


def _gdn_group(q, k, v, beta, g, z, s_list, c, nw):
    r = LANES
    n_units = r // c
    ri = lax.broadcasted_iota(jnp.int32, (r, r), 0)
    ci = lax.broadcasted_iota(jnp.int32, (r, r), 1)
    shift = int(math.log2(c))
    same = (ri >> shift) == (ci >> shift)
    causal = same & (ri >= ci)
    strict = same & (ri > ci)

    qn = q * lax.rsqrt(jnp.sum(q * q, axis=-1, keepdims=True) + EPS) * (DK_A ** -0.5)
    kn = k * lax.rsqrt(jnp.sum(k * k, axis=-1, keepdims=True) + EPS)

    gc = jnp.broadcast_to(g, (r, r))
    rin = ri & (c - 1)
    s = 1
    while s < c:
        gc = gc + jnp.where(rin >= s, pltpu.roll(gc, s, axis=0), 0.0)
        s *= 2
    grow = gc.T
    dec = jnp.where(causal, jnp.exp(jnp.minimum(gc - grow, 0.0)), 0.0)

    kb = kn * beta
    kq = _dot_nt(jnp.concatenate([kb, qn], axis=0), kn)
    a = jnp.where(strict, kq[:r] * dec, 0.0)
    aqk = kq[r:] * dec

    bk = -a
    p = jnp.where(ri == ci, 1.0, 0.0) + bk
    n_factors = int(math.log2(c))
    bk = _dot(bk, bk)
    for _ in range(n_factors - 2):
        st = _dot(jnp.concatenate([p, bk], axis=0), bk)
        p = p + st[:r]
        bk = st[r:]
    p = p + _dot(p, bk)

    eg = jnp.exp(gc)
    uw = _dot(p, jnp.concatenate([v * beta, kb * eg], axis=1))
    u = uw[:, :DV_A]
    w = uw[:, DV_A:]
    qd = qn * eg

    g_last_rows = [gc[(i + 1) * c - 1:(i + 1) * c, :] for i in range(n_units)]
    gl = jnp.concatenate([jnp.broadcast_to(t, (c, r)) for t in g_last_rows], axis=0)
    kdt = (kn * jnp.exp(gl - gc)).T

    v_new, q_s = [], []
    for i in range(n_units):
        sl = slice(i * c, (i + 1) * c)
        res = _dot(jnp.concatenate([w[sl], qd[sl]], axis=0), s_list[i])
        v_new.append(u[sl] - res[:c])
        q_s.append(res[c:])
    vn = jnp.concatenate(v_new, axis=0)
    qs = jnp.concatenate(q_s, axis=0)

    lane_unit = ci >> shift
    lhs = jnp.concatenate([aqk] + [jnp.where(lane_unit == i, kdt, 0.0) for i in range(n_units)], axis=0)
    res = _dot(lhs, vn)
    o = qs + res[:r]
    s_new = [s_list[i] * jnp.exp(g_last_rows[i]) + res[r * (i + 1):r * (i + 2)] for i in range(n_units)]

    on = o * lax.rsqrt(jnp.mean(o * o, axis=-1, keepdims=True) + EPS) * nw * (z * _sigmoid(z))
    return on, s_new


def _gdn_prompt_kernel(n_seg, prev_ref, qkv_ref, z_ref, ba_ref, cw_ref, alog_ref, dtb_ref, nw_ref,
                       o_ref, s_ref, conv_ref, xbuf, s_scr):
    seg = pl.program_id(1)

    @pl.when(seg == 0)
    def _():
        s_scr[...] = jnp.zeros_like(s_scr)

    xbuf[0:SUBLANES, :] = jnp.where(seg > 0, prev_ref[...], 0.0)
    xbuf[SUBLANES:, :] = qkv_ref[...]
    cw = cw_ref[...]
    alog = alog_ref[...]
    dtb = dtb_ref[...]
    nw = nw_ref[...]

    def body(ci, carry):
        r0 = pl.multiple_of(ci * CHUNK, CHUNK)
        y = _conv_silu(xbuf[pl.ds(r0, CHUNK + SUBLANES), :], cw)[SUBLANES:]
        beta_all, g_all = _gates(ba_ref[pl.ds(r0, CHUNK), :], alog, dtb)
        z = z_ref[pl.ds(r0, CHUNK), :]
        for pair in range(H_A // 2):
            heads = (2 * pair, 2 * pair + 1)

            def cols(base, width=LANES):
                return jnp.concatenate([y[:, base + h * width:base + (h + 1) * width] for h in heads], axis=0)

            q = cols(0)
            k = cols(H_A * DK_A)
            v = cols(2 * H_A * DK_A)
            beta = jnp.concatenate([beta_all[:, h:h + 1] for h in heads], axis=0)
            g = jnp.concatenate([g_all[:, H_A + h:H_A + h + 1] for h in heads], axis=0)
            zz = jnp.concatenate([z[:, h * DV_A:(h + 1) * DV_A] for h in heads], axis=0)
            on, s_new = _gdn_group(q, k, v, beta, g, zz, [s_scr[h] for h in heads], CHUNK, nw)
            for i, h in enumerate(heads):
                o_ref[pl.ds(r0, CHUNK), h * DV_A:(h + 1) * DV_A] = on[i * CHUNK:(i + 1) * CHUNK]
                s_scr[h] = s_new[i]
        return carry

    lax.fori_loop(0, SEG // CHUNK, body, 0)

    @pl.when(seg == n_seg - 1)
    def _():
        s_ref[...] = s_scr[...]
        conv_ref[...] = xbuf[SEG:SEG + SUBLANES, :]


def _gdn_prompt(proj, n_batch, seq, cw_all, alog_all, dtb_all, nw_all, layer):
    n_seg = seq // SEG
    seg8 = SEG // SUBLANES
    seq8 = seq // SUBLANES

    def prev_map(b, s):
        return (jnp.maximum(b * seq8 + s * seg8 - 1, 0), 0)

    row_map = lambda b, s: (b * n_seg + s, 0)
    vec = lambda: pl.BlockSpec((None, 1, LANES), lambda b, s: (layer, 0, 0))
    return pl.pallas_call(
        functools.partial(_gdn_prompt_kernel, n_seg),
        name="gdn_prompt",
        grid=(n_batch, n_seg),
        in_specs=[pl.BlockSpec((SUBLANES, CONV_DIM), prev_map),
                  pl.BlockSpec((SEG, CONV_DIM), row_map),
                  pl.BlockSpec((SEG, A_WIDTH), lambda b, s: (b * n_seg + s, COL_Z // A_WIDTH)),
                  pl.BlockSpec((SEG, LANES), lambda b, s: (b * n_seg + s, COL_BA // LANES)),
                  pl.BlockSpec((None, CONV_WIDTH, CONV_DIM), lambda b, s: (layer, 0, 0)),
                  vec(), vec(), vec()],
        out_specs=[pl.BlockSpec((SEG, A_WIDTH), row_map),
                   pl.BlockSpec((None, H_A, DK_A, DV_A), lambda b, s: (b, 0, 0, 0)),
                   pl.BlockSpec((SUBLANES, CONV_DIM), lambda b, s: (b, 0))],
        out_shape=[jax.ShapeDtypeStruct((n_batch * seq, A_WIDTH), F32),
                   jax.ShapeDtypeStruct((n_batch, H_A, DK_A, DV_A), F32),
                   jax.ShapeDtypeStruct((n_batch * SUBLANES, CONV_DIM), F32)],
        scratch_shapes=[pltpu.VMEM((SEG + SUBLANES, CONV_DIM), F32),
                        pltpu.VMEM((H_A, DK_A, DV_A), F32)],
        compiler_params=_params(2),
    )(proj, proj, proj, proj, cw_all, alog_all, dtb_all, nw_all)


def _token_rows(n_rows):
    row = lax.broadcasted_iota(jnp.int32, (n_rows, 1), 0) & (S_TILE - 1)
    return (row >= ROW0) & (row < ROW0 + DEC_SEQ)


def _gdn_sample_kernel(qkv_ref, hist_ref, z_ref, ba_ref, cw_ref, alog_ref, dtb_ref, nw_ref, s_in_ref,
                       o_ref, s_out_ref, conv_ref):
    rows = SB * S_TILE
    valid = _token_rows(rows)
    x = jnp.where(valid, qkv_ref[...], 0.0) + hist_ref[...]
    conv_ref[...] = pltpu.roll(x, rows - (DEC_SEQ), axis=0)
    y = _conv_silu(x, cw_ref[...])
    beta_all, g_all = _gates(ba_ref[...], alog_ref[...], dtb_ref[...])
    beta_all = jnp.where(valid, beta_all, 0.0)
    g_all = jnp.where(valid, g_all, 0.0)
    z = z_ref[...]
    ym = jnp.where(valid, y, 0.0)

    units = [(b, h) for b in range(SB) for h in range(H_A)]

    def stack(src, base, width=LANES, lane_off=0):
        return jnp.concatenate(
            [src[b * S_TILE:(b + 1) * S_TILE, base + h * width + lane_off:base + (h + 1) * width + lane_off]
             for b, h in units], axis=0)

    q = stack(y, 0)
    k = stack(ym, H_A * DK_A)
    v = stack(ym, 2 * H_A * DK_A)
    beta = stack(beta_all, 0, 1)
    g = stack(g_all, H_A, 1)
    zz = stack(z, 0)
    on, s_new = _gdn_group(q, k, v, beta, g, zz, [s_in_ref[b, h] for b, h in units], S_TILE, nw_ref[...])
    for i, (b, h) in enumerate(units):
        o_ref[b * S_TILE:(b + 1) * S_TILE, h * DV_A:(h + 1) * DV_A] = on[i * S_TILE:(i + 1) * S_TILE]
        s_out_ref[b, h] = s_new[i]


def _gdn_sample(proj, hist, state_all, cw_all, alog_all, dtb_all, nw_all, layer):
    rows = SB * S_TILE
    n_seq = proj.shape[0] // S_TILE
    vec = lambda: pl.BlockSpec((None, 1, LANES), lambda i: (layer, 0, 0))
    return pl.pallas_call(
        _gdn_sample_kernel,
        name="gdn_sample",
        grid=(n_seq // SB,),
        in_specs=[pl.BlockSpec((rows, CONV_DIM), lambda i: (i, 0)),
                  pl.BlockSpec((None, rows, CONV_DIM), lambda i: (layer, i, 0)),
                  pl.BlockSpec((rows, A_WIDTH), lambda i: (i, COL_Z // A_WIDTH)),
                  pl.BlockSpec((rows, LANES), lambda i: (i, COL_BA // LANES)),
                  pl.BlockSpec((None, CONV_WIDTH, CONV_DIM), lambda i: (layer, 0, 0)),
                  vec(), vec(), vec(),
                  pl.BlockSpec((None, SB, H_A, DK_A, DV_A), lambda i: (layer, i, 0, 0, 0))],
        out_specs=[pl.BlockSpec((rows, A_WIDTH), lambda i: (i, 0)),
                   pl.BlockSpec((SB, H_A, DK_A, DV_A), lambda i: (i, 0, 0, 0)),
                   pl.BlockSpec((rows, CONV_DIM), lambda i: (i, 0))],
        out_shape=[jax.ShapeDtypeStruct((n_seq * S_TILE, A_WIDTH), F32),
                   jax.ShapeDtypeStruct((n_seq, H_A, DK_A, DV_A), F32),
                   jax.ShapeDtypeStruct((n_seq * S_TILE, CONV_DIM), F32)],
        compiler_params=_params(1),
    )(proj, hist, proj, proj, cw_all, alog_all, dtb_all, nw_all, state_all)


def _swa_heads(q, k_all, v_all, mask, sink_ref, q_rows):
    lane_head = lax.broadcasted_iota(jnp.int32, (1, LANES), 1) >> int(math.log2(HD_B))
    outs = [None] * (H_QB // 2)
    for hk in range(H_KVB):
        kh = jnp.where(lane_head == hk, k_all, 0.0)
        vh = jnp.where(lane_head == hk, v_all, 0.0)
        q_tiles, sinks = [], []
        for gi in range(GQA_GROUP):
            hq = hk * GQA_GROUP + gi
            qt = q[:, (hq // 2) * LANES:(hq // 2 + 1) * LANES]
            if hq % 2 != hk:
                qt = pltpu.roll(qt, HD_B, axis=1)
            q_tiles.append(qt)
            sinks.append(jnp.broadcast_to(sink_ref[hq:hq + 1, 0:1], (q_rows, 1)))
        s = _dot_nt(jnp.concatenate(q_tiles, axis=0), kh) * (HD_B ** -0.5)
        s = jnp.where(mask, s, NEG)
        sk = jnp.concatenate(sinks, axis=0)
        m = jnp.maximum(jnp.max(s, axis=-1, keepdims=True), sk)
        p = jnp.exp(s - m)
        den = jnp.sum(p, axis=-1, keepdims=True) + jnp.exp(sk - m)
        o = _dot(p, vh) * (1.0 / den)
        for gi in range(GQA_GROUP):
            hq = hk * GQA_GROUP + gi
            og = o[gi * q_rows:(gi + 1) * q_rows]
            if hq % 2 != hk:
                og = pltpu.roll(og, HD_B, axis=1)
            outs[hq // 2] = og if outs[hq // 2] is None else outs[hq // 2] + og
    return jnp.concatenate(outs, axis=1)


def _swa_prompt_kernel(q_ref, kc_ref, kp_ref, vc_ref, vp_ref, sink_ref, o_ref):
    n = pl.program_id(1)
    k_all = jnp.concatenate([kp_ref[...], kc_ref[...]], axis=0)
    v_all = jnp.concatenate([vp_ref[...], vc_ref[...]], axis=0)
    shape = (GQA_GROUP * WINDOW, 2 * WINDOW)
    r = lax.broadcasted_iota(jnp.int32, shape, 0) & (WINDOW - 1)
    c = lax.broadcasted_iota(jnp.int32, shape, 1)
    mask = (c > r) & (c <= r + WINDOW) & (c + n * WINDOW >= WINDOW)
    o_ref[...] = _swa_heads(q_ref[...], k_all, v_all, mask, sink_ref, WINDOW)


def _swa_prompt(proj, n_batch, seq, sink_all, layer):
    nb = seq // WINDOW
    cur = lambda col: (lambda b, n: (b * nb + n, col))
    prev = lambda col: (lambda b, n: (b * nb + jnp.maximum(n - 1, 0), col))
    return pl.pallas_call(
        _swa_prompt_kernel,
        name="swa_prompt",
        grid=(n_batch, nb),
        in_specs=[pl.BlockSpec((WINDOW, B_WIDTH), cur(COL_QB // B_WIDTH)),
                  pl.BlockSpec((WINDOW, KV_WIDTH), cur(COL_KB // KV_WIDTH)),
                  pl.BlockSpec((WINDOW, KV_WIDTH), prev(COL_KB // KV_WIDTH)),
                  pl.BlockSpec((WINDOW, KV_WIDTH), cur(COL_VB // KV_WIDTH)),
                  pl.BlockSpec((WINDOW, KV_WIDTH), prev(COL_VB // KV_WIDTH)),
                  pl.BlockSpec((None, H_QB, LANES), lambda b, n: (layer, 0, 0))],
        out_specs=pl.BlockSpec((WINDOW, B_WIDTH), lambda b, n: (b * nb + n, 0)),
        out_shape=jax.ShapeDtypeStruct((n_batch * seq, B_WIDTH), F32),
        compiler_params=_params(2),
    )(proj, proj, proj, proj, proj, sink_all)


def _shift_cache(cache, new_tile):
    rolled = pltpu.roll(cache, WINDOW - DEC_SEQ, axis=0)
    moved = pltpu.roll(new_tile, S_TILE - DEC_SEQ - ROW0, axis=0)
    row = lax.broadcasted_iota(jnp.int32, (S_TILE, 1), 0)
    tail = jnp.where(row >= S_TILE - DEC_SEQ, moved, rolled[WINDOW - S_TILE:])
    return jnp.concatenate([rolled[:WINDOW - S_TILE], tail], axis=0)


def _swa_sample_kernel(q_ref, k_ref, v_ref, kc_ref, vc_ref, sink_ref, o_ref, ko_ref, vo_ref):
    shape = (GQA_GROUP * S_TILE, 2 * WINDOW)
    r = (lax.broadcasted_iota(jnp.int32, shape, 0) & (S_TILE - 1)) - ROW0
    c = lax.broadcasted_iota(jnp.int32, shape, 1)
    j = c - WINDOW - ROW0
    mask = ((c < WINDOW) & (c > r)) | ((j >= 0) & (j < DEC_SEQ) & (j <= r))
    pad = jnp.zeros((WINDOW - S_TILE, KV_WIDTH), F32)
    for b in range(SB):
        rows = slice(b * S_TILE, (b + 1) * S_TILE)
        k_new = k_ref[rows, :]
        v_new = v_ref[rows, :]
        k_all = jnp.concatenate([kc_ref[b], k_new, pad], axis=0)
        v_all = jnp.concatenate([vc_ref[b], v_new, pad], axis=0)
        o_ref[rows, :] = _swa_heads(q_ref[rows, :], k_all, v_all, mask, sink_ref, S_TILE)
        ko_ref[b] = _shift_cache(kc_ref[b], k_new)
        vo_ref[b] = _shift_cache(vc_ref[b], v_new)


def _swa_sample(proj, kc_all, vc_all, sink_all, layer):
    rows = SB * S_TILE
    n_seq = proj.shape[0] // S_TILE
    cache_in = lambda: pl.BlockSpec((None, SB, WINDOW, KV_WIDTH), lambda i: (layer, i, 0, 0))
    cache_out = lambda: pl.BlockSpec((SB, WINDOW, KV_WIDTH), lambda i: (i, 0, 0))
    return pl.pallas_call(
        _swa_sample_kernel,
        name="swa_sample",
        grid=(n_seq // SB,),
        in_specs=[pl.BlockSpec((rows, B_WIDTH), lambda i: (i, COL_QB // B_WIDTH)),
                  pl.BlockSpec((rows, KV_WIDTH), lambda i: (i, COL_KB // KV_WIDTH)),
                  pl.BlockSpec((rows, KV_WIDTH), lambda i: (i, COL_VB // KV_WIDTH)),
                  cache_in(), cache_in(),
                  pl.BlockSpec((None, H_QB, LANES), lambda i: (layer, 0, 0))],
        out_specs=[pl.BlockSpec((rows, B_WIDTH), lambda i: (i, 0)), cache_out(), cache_out()],
        out_shape=[jax.ShapeDtypeStruct((n_seq * S_TILE, B_WIDTH), F32),
                   jax.ShapeDtypeStruct((n_seq, WINDOW, KV_WIDTH), F32),
                   jax.ShapeDtypeStruct((n_seq, WINDOW, KV_WIDTH), F32)],
        compiler_params=_params(1),
    )(proj, proj, proj, kc_all, vc_all, sink_all)


def kernel(x_prompt, x_sample, state_delta, state_conv, cache_swa_k, cache_swa_v, w_in, conv_w, a_log,
           dt_bias, norm_a_w, sinks, w_out, ln1_g, ln1_b, w_ffn_in, w_ffn_out, ln2_g, ln2_b):
    depth = w_in.shape[0]
    n_batch, seq, _ = x_prompt.shape
    n_dec, dec_seq, _ = x_sample.shape
    assert dec_seq == DEC_SEQ and seq % SEG == 0 and n_dec % SB == 0
    assert cache_swa_k.shape[2] == WINDOW
    alpha = (2 * depth) ** 0.25

    c0 = CONV_DIM + A_WIDTH
    w_in_r = jnp.concatenate(
        [w_in[:, :, :c0], w_in[:, :, c0 + 2 * H_A:], w_in[:, :, c0:c0 + 2 * H_A],
         jnp.zeros((depth, D_MODEL, LANES - 2 * H_A), w_in.dtype)], axis=-1).astype(BF16)
    w_out_b = w_out.astype(BF16)
    w_ffn_in_b = w_ffn_in.astype(BF16)
    w_ffn_out_b = w_ffn_out.astype(BF16)
    lane_pad = lambda t: jnp.pad(t, ((0, 0), (H_A, LANES - 2 * H_A)))[:, None, :]
    alog_v = lane_pad(a_log)
    dtb_v = lane_pad(dt_bias)
    nw_v = norm_a_w[:, None, :]
    sink_v = jnp.broadcast_to(sinks[:, :, None], (depth, H_QB, LANES))
    ln1_g3, ln1_b3, ln2_g3, ln2_b3 = (t[:, None, :] for t in (ln1_g, ln1_b, ln2_g, ln2_b))

    hist = jnp.pad(state_conv, ((0, 0), (0, 0), (0, S_TILE - (CONV_WIDTH - 1)), (0, 0)))
    hist = hist.reshape(depth, n_dec * S_TILE, CONV_DIM)
    kc = cache_swa_k.reshape(depth, n_dec, WINDOW, KV_WIDTH)
    vc = cache_swa_v.reshape(depth, n_dec, WINDOW, KV_WIDTH)

    xp = x_prompt.reshape(n_batch * seq, D_MODEL)
    xs = jnp.pad(x_sample, ((0, 0), (ROW0, S_TILE - ROW0 - DEC_SEQ), (0, 0))).reshape(n_dec * S_TILE, D_MODEL)

    outs = [[] for _ in range(8)]
    for l in range(depth):
        proj = _in_proj(xp, w_in_r, l)
        oa, s_p, conv_p = _gdn_prompt(proj, n_batch, seq, conv_w, alog_v, dtb_v, nw_v, l)
        ob = _swa_prompt(proj, n_batch, seq, sink_v, l)
        xp = _out_proj_ln(xp, oa, ob, w_out_b, ln1_g3, ln1_b3, l, alpha)
        xp = _ffn_ln(xp, w_ffn_in_b, w_ffn_out_b, ln2_g3, ln2_b3, l, alpha)
        proj3 = proj.reshape(n_batch, seq, PROJ_PAD)
        outs[0].append(s_p)
        outs[1].append(conv_p.reshape(n_batch, SUBLANES, CONV_DIM)[:, SUBLANES - (CONV_WIDTH - 1):])
        outs[2].append(proj3[:, seq - WINDOW:, COL_KB:COL_KB + KV_WIDTH].reshape(n_batch, WINDOW, H_KVB, HD_B))
        outs[3].append(proj3[:, seq - WINDOW:, COL_VB:COL_VB + KV_WIDTH].reshape(n_batch, WINDOW, H_KVB, HD_B))
        proj = _in_proj(xs, w_in_r, l)
        oa, s_s, conv_s = _gdn_sample(proj, hist, state_delta, conv_w, alog_v, dtb_v, nw_v, l)
        ob, k_s, v_s = _swa_sample(proj, kc, vc, sink_v, l)
        xs = _out_proj_ln(xs, oa, ob, w_out_b, ln1_g3, ln1_b3, l, alpha)
        xs = _ffn_ln(xs, w_ffn_in_b, w_ffn_out_b, ln2_g3, ln2_b3, l, alpha)
        outs[4].append(s_s)
        outs[5].append(conv_s.reshape(n_dec, S_TILE, CONV_DIM)[:, :CONV_WIDTH - 1])
        outs[6].append(k_s.reshape(n_dec, WINDOW, H_KVB, HD_B))
        outs[7].append(v_s.reshape(n_dec, WINDOW, H_KVB, HD_B))

    y_prompt = xp.reshape(n_batch, seq, D_MODEL)
    y_sample = xs.reshape(n_dec, S_TILE, D_MODEL)[:, ROW0:ROW0 + DEC_SEQ]
    return (y_prompt, y_sample) + tuple(jnp.stack(o) for o in outs)
```

```python
import functools
import math
from typing import NamedTuple

import jax
import jax.numpy as jnp
from jax import lax
from jax.experimental import pallas as pl
from jax.experimental.pallas import tpu as pltpu

F32 = jnp.float32
BF16 = jnp.bfloat16

D_MODEL = 1024
H_A = 4
DK_A = 128
DV_A = 128
CONV_WIDTH = 4
CONV_DIM = 2 * H_A * DK_A + H_A * DV_A
A_WIDTH = H_A * DV_A
CHUNK = 64
HD_B = 64
H_QB = 8
H_KVB = 2
GQA_GROUP = H_QB // H_KVB
B_WIDTH = H_QB * HD_B
KV_WIDTH = H_KVB * HD_B
WINDOW = 128
D_FF = 2816
EPS = 1e-6

LANES = 128
SUBLANES = 8

COL_Z = CONV_DIM
COL_QB = COL_Z + A_WIDTH
COL_KB = COL_QB + B_WIDTH
COL_VB = COL_KB + KV_WIDTH
COL_BA = COL_VB + KV_WIDTH
PROJ_PAD = COL_BA + LANES
QB_HEAD_ORDER = tuple(hk * GQA_GROUP + g for g in range(GQA_GROUP) for hk in range(H_KVB))

S_TILE = SUBLANES
ROW0 = CONV_WIDTH - 1
DEC_SEQ = 4
SB = 4

TM = 512
SEG = 512
N_CHUNK = SEG // CHUNK
PREP_CHUNKS = 4
FF_CHUNK = 256
NEG = -1e30
VMEM_LIMIT = 56 * 1024 * 1024


def _params(n_axes, vmem=None):
    return pltpu.CompilerParams(
        dimension_semantics=("arbitrary",) * n_axes,
        vmem_limit_bytes=vmem if vmem is not None else VMEM_LIMIT)


def _dot(a, b):
    return jnp.dot(a.astype(BF16), b.astype(BF16), preferred_element_type=F32)


def _dot_nt(a, b):
    return lax.dot_general(a.astype(BF16), b.astype(BF16), (((1,), (1,)), ((), ())),
                           preferred_element_type=F32)


def _sigmoid(x):
    return 1.0 / (1.0 + jnp.exp(-x))


def _softplus(x):
    return jnp.maximum(x, 0.0) + jnp.log(1.0 + jnp.exp(-jnp.abs(x)))


def _layernorm(h, g, b):
    mu = jnp.mean(h, axis=-1, keepdims=True)
    d = h - mu
    var = jnp.mean(d * d, axis=-1, keepdims=True)
    return d * lax.rsqrt(var + EPS) * g + b


def _in_proj_kernel(x_ref, w_ref, o_ref):
    xb = x_ref[...].astype(BF16)
    n = o_ref.shape[1]
    for n0 in range(0, n, 512):
        n1 = min(n0 + 512, n)
        o_ref[:, n0:n1] = jnp.dot(xb, w_ref[:, n0:n1], preferred_element_type=F32)


def _in_proj(x, w_all, layer):
    t = x.shape[0]
    return pl.pallas_call(
        _in_proj_kernel,
        name="in_proj",
        grid=(t // TM,),
        in_specs=[pl.BlockSpec((TM, D_MODEL), lambda i: (i, 0)),
                  pl.BlockSpec((None, D_MODEL, PROJ_PAD), lambda i: (layer, 0, 0))],
        out_specs=pl.BlockSpec((TM, PROJ_PAD), lambda i: (i, 0)),
        out_shape=jax.ShapeDtypeStruct((t, PROJ_PAD), F32),
        compiler_params=_params(1),
    )(x, w_all)


def _out_proj_ln_kernel(alpha, x_ref, oa_ref, ob_ref, w_ref, g_ref, b_ref, o_ref):
    m = (jnp.dot(oa_ref[...].astype(BF16), w_ref[0:A_WIDTH, :], preferred_element_type=F32)
         + jnp.dot(ob_ref[...].astype(BF16), w_ref[A_WIDTH:, :], preferred_element_type=F32))
    o_ref[...] = _layernorm(alpha * x_ref[...] + m, g_ref[...], b_ref[...])


def _out_proj_ln(x, oa, ob, w_all, g_all, b_all, layer, alpha):
    t = x.shape[0]
    return pl.pallas_call(
        functools.partial(_out_proj_ln_kernel, alpha),
        name="out_proj_ln",
        grid=(t // TM,),
        in_specs=[pl.BlockSpec((TM, D_MODEL), lambda i: (i, 0)),
                  pl.BlockSpec((TM, A_WIDTH), lambda i: (i, 0)),
                  pl.BlockSpec((TM, B_WIDTH), lambda i: (i, 0)),
                  pl.BlockSpec((None, D_MODEL, D_MODEL), lambda i: (layer, 0, 0)),
                  pl.BlockSpec((None, 1, D_MODEL), lambda i: (layer, 0, 0)),
                  pl.BlockSpec((None, 1, D_MODEL), lambda i: (layer, 0, 0))],
        out_specs=pl.BlockSpec((TM, D_MODEL), lambda i: (i, 0)),
        out_shape=jax.ShapeDtypeStruct((t, D_MODEL), F32),
        compiler_params=_params(1),
    )(x, oa, ob, w_all, g_all, b_all)


def _ffn_ln_kernel(alpha, x_ref, wi_ref, wo_ref, g_ref, b_ref, o_ref):
    x = x_ref[...]
    xb = x.astype(BF16)
    acc = alpha * x
    for c0 in range(0, D_FF, FF_CHUNK):
        gate = jnp.dot(xb, wi_ref[:, c0:c0 + FF_CHUNK], preferred_element_type=F32)
        up = jnp.dot(xb, wi_ref[:, D_FF + c0:D_FF + c0 + FF_CHUNK], preferred_element_type=F32)
        h = gate * _sigmoid(gate) * up
        acc = acc + jnp.dot(h.astype(BF16), wo_ref[c0:c0 + FF_CHUNK, :], preferred_element_type=F32)
    o_ref[...] = _layernorm(acc, g_ref[...], b_ref[...])


def _ffn_ln(x, wi_all, wo_all, g_all, b_all, layer, alpha):
    t = x.shape[0]
    return pl.pallas_call(
        functools.partial(_ffn_ln_kernel, alpha),
        name="ffn_ln",
        grid=(t // TM,),
        in_specs=[pl.BlockSpec((TM, D_MODEL), lambda i: (i, 0)),
                  pl.BlockSpec((None, D_MODEL, 2 * D_FF), lambda i: (layer, 0, 0)),
                  pl.BlockSpec((None, D_FF, D_MODEL), lambda i: (layer, 0, 0)),
                  pl.BlockSpec((None, 1, D_MODEL), lambda i: (layer, 0, 0)),
                  pl.BlockSpec((None, 1, D_MODEL), lambda i: (layer, 0, 0))],
        out_specs=pl.BlockSpec((TM, D_MODEL), lambda i: (i, 0)),
        out_shape=jax.ShapeDtypeStruct((t, D_MODEL), F32),
        compiler_params=_params(1),
    )(x, wi_all, wo_all, g_all, b_all)


def _conv_silu(x, cw):
    y = cw[3:4, :] * x
    for j in range(1, CONV_WIDTH):
        y = y + cw[CONV_WIDTH - 1 - j:CONV_WIDTH - j, :] * pltpu.roll(x, j, axis=0)
    return y * _sigmoid(y)


def _gates(ba, alog, dtb):
    beta = _sigmoid(ba)
    g = -jnp.exp(alog) * _softplus(ba + dtb)
    return beta, g


def _cumsum_rows(g, c):
    rin = lax.broadcasted_iota(jnp.int32, g.shape, 0) & (c - 1)
    s = 1
    while s < c:
        g = g + jnp.where(rin >= s, pltpu.roll(g, s, axis=0), 0.0)
        s *= 2
    return g


class _Masks(NamedTuple):
    causal: jax.Array
    neg_strict: jax.Array
    eye: jax.Array
    unit: tuple


def _group_masks(c):
    r = LANES
    shift = int(math.log2(c))
    ri = lax.broadcasted_iota(jnp.int32, (r, r), 0)
    ci = lax.broadcasted_iota(jnp.int32, (r, r), 1)
    same = (ri >> shift) == (ci >> shift)
    return _Masks(
        causal=jnp.where(same & (ri >= ci), 1.0, 0.0),
        neg_strict=jnp.where(same & (ri > ci), -1.0, 0.0),
        eye=jnp.where(ri == ci, 1.0, 0.0),
        unit=tuple(jnp.where((ci >> shift) == i, 1.0, 0.0) for i in range(r // c)))


def _gdn_prepare(groups, c, mk):
    r = LANES
    n_units = r // c
    n_factors = int(math.log2(c))
    each = lambda f, *lists: [f(*a) for a in zip(*lists)]

    q, k, v, beta, gcol = (list(t) for t in zip(*groups))
    qn = each(lambda t: t * lax.rsqrt(jnp.sum(t * t, axis=-1, keepdims=True) + EPS) * (DK_A ** -0.5), q)
    kn = each(lambda t: t * lax.rsqrt(jnp.sum(t * t, axis=-1, keepdims=True) + EPS), k)
    gc = each(lambda t: jnp.broadcast_to(t, (r, r)), gcol)
    e = each(lambda t: jnp.exp(jnp.minimum(t - t.T, 0.0)), gc)
    kb = each(lambda a, b: a * b, kn, beta)
    kq = each(lambda a, b, d: _dot_nt(jnp.concatenate([a, b], axis=0), d), kb, qn, kn)
    bk = each(lambda a, b: a[:r] * (b * mk.neg_strict), kq, e)
    aqk = each(lambda a, b: a[r:] * (b * mk.causal), kq, e)

    p = each(lambda t: mk.eye + t, bk)
    bk = each(lambda t: _dot(t, t), bk)
    for _ in range(n_factors - 2):
        st = each(lambda a, b: _dot(jnp.concatenate([a, b], axis=0), b), p, bk)
        p = each(lambda a, b: a + b[:r], p, st)
        bk = each(lambda t: t[r:], st)
    p = each(lambda a, b: a + _dot(a, b), p, bk)

    eg = each(jnp.exp, gc)
    uw = each(lambda a, b, d, f, h: _dot(a, jnp.concatenate([b * d, f * h], axis=1)), p, v, beta, kb, eg)
    qd = each(lambda a, b: a * b, qn, eg)

    out = []
    for gi in range(len(groups)):
        g_last = [gc[gi][(i + 1) * c - 1:(i + 1) * c, :] for i in range(n_units)]
        gl = jnp.concatenate([jnp.broadcast_to(t, (c, r)) for t in g_last], axis=0)
        kdt = (kn[gi] * jnp.exp(gl - gc[gi])).T
        u = uw[gi][:, :DV_A]
        w = uw[gi][:, DV_A:]
        wq = jnp.concatenate([jnp.concatenate([w[i * c:(i + 1) * c], qd[gi][i * c:(i + 1) * c]], axis=0)
                              for i in range(n_units)], axis=0).astype(BF16)
        lhs = jnp.concatenate([aqk[gi]] + [kdt * mk.unit[i] for i in range(n_units)], axis=0).astype(BF16)
        out.append((u, wq, lhs, [jnp.exp(t) for t in g_last]))
    return out


def _gdn_apply(u, wq, lhs, egl, s_list, z, nw, c):
    r = LANES
    n_units = r // c
    v_new, q_s = [], []
    for i in range(n_units):
        res = jnp.dot(wq[2 * c * i:2 * c * (i + 1)], s_list[i].astype(BF16), preferred_element_type=F32)
        v_new.append(u[i * c:(i + 1) * c] - res[:c])
        q_s.append(res[c:])
    vn = jnp.concatenate(v_new, axis=0)
    res = jnp.dot(lhs, vn.astype(BF16), preferred_element_type=F32)
    o = jnp.concatenate(q_s, axis=0) + res[:r]
    s_new = [s_list[i] * egl[i] + res[r * (i + 1):r * (i + 2)] for i in range(n_units)]
    on = o * lax.rsqrt(jnp.mean(o * o, axis=-1, keepdims=True) + EPS) * nw * (z * _sigmoid(z))
    return on, s_new


def _gdn_prompt_kernel(n_seg, prev_ref, qkv_ref, z_ref, ba_ref, cw_ref, alog_ref, dtb_ref, nw_ref,
                       o_ref, s_ref, conv_ref, xbuf, s_scr, u_scr, wq_scr, lhs_scr, egl_scr):
    seg = pl.program_id(1)
    n_pairs = H_A // 2

    @pl.when(seg == 0)
    def _():
        s_scr[...] = jnp.zeros_like(s_scr)

    xbuf[0:SUBLANES, :] = jnp.where(seg > 0, prev_ref[...], 0.0)
    xbuf[SUBLANES:, :] = qkv_ref[...]
    cw = cw_ref[...]
    alog = alog_ref[...]
    dtb = dtb_ref[...]
    nw = nw_ref[...]
    mk = _group_masks(CHUNK)

    def prepare(it, carry):
        groups, where = [], []
        for j in range(PREP_CHUNKS):
            ci = it * PREP_CHUNKS + j
            r0 = pl.multiple_of(ci * CHUNK, CHUNK)
            y = _conv_silu(xbuf[pl.ds(r0, CHUNK + SUBLANES), :], cw)[SUBLANES:]
            beta_all, g_all = _gates(ba_ref[pl.ds(r0, CHUNK), :], alog, dtb)
            gc_all = _cumsum_rows(g_all, CHUNK)
            for pair in range(n_pairs):
                heads = (2 * pair, 2 * pair + 1)
                cols = lambda base: jnp.concatenate(
                    [y[:, base + h * LANES:base + (h + 1) * LANES] for h in heads], axis=0)
                beta = jnp.concatenate([beta_all[:, h:h + 1] for h in heads], axis=0)
                gcol = jnp.concatenate([gc_all[:, H_A + h:H_A + h + 1] for h in heads], axis=0)
                groups.append((cols(0), cols(H_A * DK_A), cols(2 * H_A * DK_A), beta, gcol))
                where.append((ci, pair))
        for (ci, pair), (u, wq, lhs, egl) in zip(where, _gdn_prepare(groups, CHUNK, mk)):
            u_scr[ci, pair] = u
            wq_scr[ci, pair] = wq
            lhs_scr[ci, pair] = lhs
            for i in range(2):
                egl_scr[ci, pair, i:i + 1, :] = egl[i]
        return carry

    lax.fori_loop(0, N_CHUNK // PREP_CHUNKS, prepare, 0)

    def scan(ci, carry):
        r0 = pl.multiple_of(ci * CHUNK, CHUNK)
        z = z_ref[pl.ds(r0, CHUNK), :]
        for pair in range(n_pairs):
            heads = (2 * pair, 2 * pair + 1)
            zz = jnp.concatenate([z[:, h * DV_A:(h + 1) * DV_A] for h in heads], axis=0)
            egl = [egl_scr[ci, pair, i:i + 1, :] for i in range(2)]
            on, s_new = _gdn_apply(u_scr[ci, pair], wq_scr[ci, pair], lhs_scr[ci, pair], egl,
                                   [s_scr[h] for h in heads], zz, nw, CHUNK)
            for i, h in enumerate(heads):
                o_ref[pl.ds(r0, CHUNK), h * DV_A:(h + 1) * DV_A] = on[i * CHUNK:(i + 1) * CHUNK]
                s_scr[h] = s_new[i]
        return carry

    lax.fori_loop(0, N_CHUNK, scan, 0)

    @pl.when(seg == n_seg - 1)
    def _():
        s_ref[...] = s_scr[...]
        conv_ref[...] = xbuf[SEG:SEG + SUBLANES, :]


def _gdn_prompt(proj, n_batch, seq, cw_all, alog_all, dtb_all, nw_all, layer):
    n_seg = seq // SEG
    seg8 = SEG // SUBLANES
    seq8 = seq // SUBLANES
    n_pairs = H_A // 2

    def prev_map(b, s):
        return (jnp.maximum(b * seq8 + s * seg8 - 1, 0), 0)

    row_map = lambda b, s: (b * n_seg + s, 0)
    vec = lambda: pl.BlockSpec((None, 1, LANES), lambda b, s: (layer, 0, 0))
    return pl.pallas_call(
        functools.partial(_gdn_prompt_kernel, n_seg),
        name="gdn_prompt",
        grid=(n_batch, n_seg),
        in_specs=[pl.BlockSpec((SUBLANES, CONV_DIM), prev_map),
                  pl.BlockSpec((SEG, CONV_DIM), row_map),
                  pl.BlockSpec((SEG, A_WIDTH), lambda b, s: (b * n_seg + s, COL_Z // A_WIDTH)),
                  pl.BlockSpec((SEG, LANES), lambda b, s: (b * n_seg + s, COL_BA // LANES)),
                  pl.BlockSpec((None, CONV_WIDTH, CONV_DIM), lambda b, s: (layer, 0, 0)),
                  vec(), vec(), vec()],
        out_specs=[pl.BlockSpec((SEG, A_WIDTH), row_map),
                   pl.BlockSpec((None, H_A, DK_A, DV_A), lambda b, s: (b, 0, 0, 0)),
                   pl.BlockSpec((SUBLANES, CONV_DIM), lambda b, s: (b, 0))],
        out_shape=[jax.ShapeDtypeStruct((n_batch * seq, A_WIDTH), F32),
                   jax.ShapeDtypeStruct((n_batch, H_A, DK_A, DV_A), F32),
                   jax.ShapeDtypeStruct((n_batch * SUBLANES, CONV_DIM), F32)],
        scratch_shapes=[pltpu.VMEM((SEG + SUBLANES, CONV_DIM), F32),
                        pltpu.VMEM((H_A, DK_A, DV_A), F32),
                        pltpu.VMEM((N_CHUNK, n_pairs, LANES, DV_A), F32),
                        pltpu.VMEM((N_CHUNK, n_pairs, 2 * LANES, DK_A), BF16),
                        pltpu.VMEM((N_CHUNK, n_pairs, 3 * LANES, LANES), BF16),
                        pltpu.VMEM((N_CHUNK, n_pairs, SUBLANES, LANES), F32)],
        compiler_params=_params(2),
    )(proj, proj, proj, proj, cw_all, alog_all, dtb_all, nw_all)


def _token_rows(n_rows):
    row = lax.broadcasted_iota(jnp.int32, (n_rows, 1), 0) & (S_TILE - 1)
    return (row >= ROW0) & (row < ROW0 + DEC_SEQ)


def _gdn_sample_kernel(qkv_ref, hist_ref, z_ref, ba_ref, cw_ref, alog_ref, dtb_ref, nw_ref, s_in_ref,
                       o_ref, s_out_ref, conv_ref):
    rows = SB * S_TILE
    valid = _token_rows(rows)
    x = jnp.where(valid, qkv_ref[...], 0.0) + hist_ref[...]
    conv_ref[...] = pltpu.roll(x, rows - DEC_SEQ, axis=0)
    y = _conv_silu(x, cw_ref[...])
    beta_all, g_all = _gates(ba_ref[...], alog_ref[...], dtb_ref[...])
    beta_all = jnp.where(valid, beta_all, 0.0)
    gc_all = _cumsum_rows(jnp.where(valid, g_all, 0.0), S_TILE)
    z = z_ref[...]
    ym = jnp.where(valid, y, 0.0)

    units = [(b, h) for b in range(SB) for h in range(H_A)]

    def stack(src, base, width=LANES):
        return jnp.concatenate(
            [src[b * S_TILE:(b + 1) * S_TILE, base + h * width:base + (h + 1) * width] for b, h in units], axis=0)

    mk = _group_masks(S_TILE)
    (u, wq, lhs, egl), = _gdn_prepare([(stack(y, 0), stack(ym, H_A * DK_A), stack(ym, 2 * H_A * DK_A),
                                        stack(beta_all, 0, 1), stack(gc_all, H_A, 1))], S_TILE, mk)
    on, s_new = _gdn_apply(u, wq, lhs, egl, [s_in_ref[b, h] for b, h in units], stack(z, 0), nw_ref[...], S_TILE)
    for i, (b, h) in enumerate(units):
        o_ref[b * S_TILE:(b + 1) * S_TILE, h * DV_A:(h + 1) * DV_A] = on[i * S_TILE:(i + 1) * S_TILE]
        s_out_ref[b, h] = s_new[i]


def _gdn_sample(proj, hist, state_all, cw_all, alog_all, dtb_all, nw_all, layer):
    rows = SB * S_TILE
    n_seq = proj.shape[0] // S_TILE
    vec = lambda: pl.BlockSpec((None, 1, LANES), lambda i: (layer, 0, 0))
    return pl.pallas_call(
        _gdn_sample_kernel,
        name="gdn_sample",
        grid=(n_seq // SB,),
        in_specs=[pl.BlockSpec((rows, CONV_DIM), lambda i: (i, 0)),
                  pl.BlockSpec((None, rows, CONV_DIM), lambda i: (layer, i, 0)),
                  pl.BlockSpec((rows, A_WIDTH), lambda i: (i, COL_Z // A_WIDTH)),
                  pl.BlockSpec((rows, LANES), lambda i: (i, COL_BA // LANES)),
                  pl.BlockSpec((None, CONV_WIDTH, CONV_DIM), lambda i: (layer, 0, 0)),
                  vec(), vec(), vec(),
                  pl.BlockSpec((None, SB, H_A, DK_A, DV_A), lambda i: (layer, i, 0, 0, 0))],
        out_specs=[pl.BlockSpec((rows, A_WIDTH), lambda i: (i, 0)),
                   pl.BlockSpec((SB, H_A, DK_A, DV_A), lambda i: (i, 0, 0, 0)),
                   pl.BlockSpec((rows, CONV_DIM), lambda i: (i, 0))],
        out_shape=[jax.ShapeDtypeStruct((n_seq * S_TILE, A_WIDTH), F32),
                   jax.ShapeDtypeStruct((n_seq, H_A, DK_A, DV_A), F32),
                   jax.ShapeDtypeStruct((n_seq * S_TILE, CONV_DIM), F32)],
        compiler_params=_params(1),
    )(proj, hist, proj, proj, cw_all, alog_all, dtb_all, nw_all, state_all)


def _swa_heads(q, k_all, v_all, mask, sink_ref, q_rows):
    lane_head = lax.broadcasted_iota(jnp.int32, (1, LANES), 1) >> int(math.log2(HD_B))
    outs = [None] * GQA_GROUP
    qs = q * (HD_B ** -0.5)
    for hk in range(H_KVB):
        kh = jnp.where(lane_head == hk, k_all, 0.0)
        vh = jnp.where(lane_head == hk, v_all, 0.0)
        q_tiles = [qs[:, gi * LANES:(gi + 1) * LANES] for gi in range(GQA_GROUP)]
        sinks = [jnp.broadcast_to(sink_ref[hk * GQA_GROUP + gi:hk * GQA_GROUP + gi + 1, 0:1], (q_rows, 1))
                 for gi in range(GQA_GROUP)]
        s = _dot_nt(jnp.concatenate(q_tiles, axis=0), kh)
        s = jnp.where(mask, s, NEG)
        sk = jnp.concatenate(sinks, axis=0)
        m = jnp.maximum(jnp.max(s, axis=-1, keepdims=True), sk)
        p = jnp.exp(s - m)
        den = jnp.sum(p, axis=-1, keepdims=True) + jnp.exp(sk - m)
        o = _dot(p, vh) * (1.0 / den)
        for gi in range(GQA_GROUP):
            og = o[gi * q_rows:(gi + 1) * q_rows]
            outs[gi] = og if outs[gi] is None else outs[gi] + og
    return jnp.concatenate(outs, axis=1)


def _swa_prompt_kernel(q_ref, kc_ref, kp_ref, vc_ref, vp_ref, sink_ref, o_ref):
    n = pl.program_id(1)
    keys = 2 * WINDOW
    k_all = jnp.concatenate([kp_ref[...], kc_ref[...]], axis=0)
    vt = jnp.concatenate([vp_ref[...], vc_ref[...]], axis=0).T
    c = lax.broadcasted_iota(jnp.int32, (keys, WINDOW), 0)
    r = lax.broadcasted_iota(jnp.int32, (keys, WINDOW), 1)
    mask = (c > r) & (c <= r + WINDOW) & (c + n * WINDOW >= WINDOW)
    shift = int(math.log2(HD_B))
    lane_head = lax.broadcasted_iota(jnp.int32, (1, LANES), 1) >> shift
    row = lax.broadcasted_iota(jnp.int32, (LANES, 1), 0)
    row_head = row >> shift
    qs = q_ref[...] * (HD_B ** -0.5)
    tiles = [None] * GQA_GROUP
    for hk in range(H_KVB):
        kh = jnp.where(lane_head == hk, k_all, 0.0).astype(BF16)
        ones_row = (1 - hk) * HD_B
        vth = jnp.where(row_head == hk, vt, jnp.where(row == ones_row, 1.0, 0.0)).astype(BF16)
        for gi in range(GQA_GROUP):
            hq = hk * GQA_GROUP + gi
            st = _dot_nt(kh, qs[:, gi * LANES:(gi + 1) * LANES])
            st = jnp.where(mask, st, NEG)
            sk = sink_ref[hq:hq + 1, :]
            m = jnp.maximum(jnp.max(st, axis=0, keepdims=True), sk)
            pt = jnp.exp(st - m)
            ot = _dot(vth, pt)
            den = ot[ones_row:ones_row + 1, :] + jnp.exp(sk - m)
            ot = jnp.where(row_head == hk, ot * (1.0 / den), 0.0)
            tiles[gi] = ot if tiles[gi] is None else tiles[gi] + ot
    o_ref[...] = jnp.concatenate([t.T for t in tiles], axis=1)


def _swa_prompt(proj, n_batch, seq, sink_all, layer):
    nb = seq // WINDOW
    cur = lambda col: (lambda b, n: (b * nb + n, col))
    prev = lambda col: (lambda b, n: (b * nb + jnp.maximum(n - 1, 0), col))
    return pl.pallas_call(
        _swa_prompt_kernel,
        name="swa_prompt",
        grid=(n_batch, nb),
        in_specs=[pl.BlockSpec((WINDOW, B_WIDTH), cur(COL_QB // B_WIDTH)),
                  pl.BlockSpec((WINDOW, KV_WIDTH), cur(COL_KB // KV_WIDTH)),
                  pl.BlockSpec((WINDOW, KV_WIDTH), prev(COL_KB // KV_WIDTH)),
                  pl.BlockSpec((WINDOW, KV_WIDTH), cur(COL_VB // KV_WIDTH)),
                  pl.BlockSpec((WINDOW, KV_WIDTH), prev(COL_VB // KV_WIDTH)),
                  pl.BlockSpec((None, H_QB, LANES), lambda b, n: (layer, 0, 0))],
        out_specs=pl.BlockSpec((WINDOW, B_WIDTH), lambda b, n: (b * nb + n, 0)),
        out_shape=jax.ShapeDtypeStruct((n_batch * seq, B_WIDTH), F32),
        compiler_params=_params(2),
    )(proj, proj, proj, proj, proj, sink_all)


def _shift_cache(cache, new_tile):
    rolled = pltpu.roll(cache, WINDOW - DEC_SEQ, axis=0)
    moved = pltpu.roll(new_tile, S_TILE - DEC_SEQ - ROW0, axis=0)
    row = lax.broadcasted_iota(jnp.int32, (S_TILE, 1), 0)
    tail = jnp.where(row >= S_TILE - DEC_SEQ, moved, rolled[WINDOW - S_TILE:])
    return jnp.concatenate([rolled[:WINDOW - S_TILE], tail], axis=0)


def _swa_sample_kernel(q_ref, k_ref, v_ref, kc_ref, vc_ref, sink_ref, o_ref, ko_ref, vo_ref):
    shape = (GQA_GROUP * S_TILE, 2 * WINDOW)
    r = (lax.broadcasted_iota(jnp.int32, shape, 0) & (S_TILE - 1)) - ROW0
    c = lax.broadcasted_iota(jnp.int32, shape, 1)
    j = c - WINDOW - ROW0
    mask = ((c < WINDOW) & (c > r)) | ((j >= 0) & (j < DEC_SEQ) & (j <= r))
    pad = jnp.zeros((WINDOW - S_TILE, KV_WIDTH), F32)
    for b in range(SB):
        rows = slice(b * S_TILE, (b + 1) * S_TILE)
        k_new = k_ref[rows, :]
        v_new = v_ref[rows, :]
        k_all = jnp.concatenate([kc_ref[b], k_new, pad], axis=0)
        v_all = jnp.concatenate([vc_ref[b], v_new, pad], axis=0)
        o_ref[rows, :] = _swa_heads(q_ref[rows, :], k_all, v_all, mask, sink_ref, S_TILE)
        ko_ref[b] = _shift_cache(kc_ref[b], k_new)
        vo_ref[b] = _shift_cache(vc_ref[b], v_new)


def _swa_sample(proj, kc_all, vc_all, sink_all, layer):
    rows = SB * S_TILE
    n_seq = proj.shape[0] // S_TILE
    cache_in = lambda: pl.BlockSpec((None, SB, WINDOW, KV_WIDTH), lambda i: (layer, i, 0, 0))
    cache_out = lambda: pl.BlockSpec((SB, WINDOW, KV_WIDTH), lambda i: (i, 0, 0))
    return pl.pallas_call(
        _swa_sample_kernel,
        name="swa_sample",
        grid=(n_seq // SB,),
        in_specs=[pl.BlockSpec((rows, B_WIDTH), lambda i: (i, COL_QB // B_WIDTH)),
                  pl.BlockSpec((rows, KV_WIDTH), lambda i: (i, COL_KB // KV_WIDTH)),
                  pl.BlockSpec((rows, KV_WIDTH), lambda i: (i, COL_VB // KV_WIDTH)),
                  cache_in(), cache_in(),
                  pl.BlockSpec((None, H_QB, LANES), lambda i: (layer, 0, 0))],
        out_specs=[pl.BlockSpec((rows, B_WIDTH), lambda i: (i, 0)), cache_out(), cache_out()],
        out_shape=[jax.ShapeDtypeStruct((n_seq * S_TILE, B_WIDTH), F32),
                   jax.ShapeDtypeStruct((n_seq, WINDOW, KV_WIDTH), F32),
                   jax.ShapeDtypeStruct((n_seq, WINDOW, KV_WIDTH), F32)],
        compiler_params=_params(1),
    )(proj, proj, proj, kc_all, vc_all, sink_all)


def kernel(x_prompt, x_sample, state_delta, state_conv, cache_swa_k, cache_swa_v, w_in, conv_w, a_log,
           dt_bias, norm_a_w, sinks, w_out, ln1_g, ln1_b, w_ffn_in, w_ffn_out, ln2_g, ln2_b):
    depth = w_in.shape[0]
    n_batch, seq, _ = x_prompt.shape
    n_dec, dec_seq, _ = x_sample.shape
    assert dec_seq == DEC_SEQ and seq % SEG == 0 and n_dec % SB == 0
    assert cache_swa_k.shape[2] == WINDOW
    alpha = (2 * depth) ** 0.25

    c0 = CONV_DIM + A_WIDTH
    q0 = c0 + 2 * H_A
    order = jnp.array(QB_HEAD_ORDER)
    w_q = w_in[:, :, q0:q0 + B_WIDTH].reshape(depth, D_MODEL, H_QB, HD_B)[:, :, order]
    w_in_r = jnp.concatenate(
        [w_in[:, :, :c0], w_q.reshape(depth, D_MODEL, B_WIDTH), w_in[:, :, q0 + B_WIDTH:],
         w_in[:, :, c0:q0], jnp.zeros((depth, D_MODEL, LANES - 2 * H_A), w_in.dtype)], axis=-1).astype(BF16)
    w_out_bq = w_out[:, A_WIDTH:].reshape(depth, H_QB, HD_B, D_MODEL)[:, order].reshape(depth, B_WIDTH, D_MODEL)
    w_out_b = jnp.concatenate([w_out[:, :A_WIDTH], w_out_bq], axis=1).astype(BF16)
    w_ffn_in_b = w_ffn_in.astype(BF16)
    w_ffn_out_b = w_ffn_out.astype(BF16)
    lane_pad = lambda t: jnp.pad(t, ((0, 0), (H_A, LANES - 2 * H_A)))[:, None, :]
    alog_v = lane_pad(a_log)
    dtb_v = lane_pad(dt_bias)
    nw_v = norm_a_w[:, None, :]
    sink_v = jnp.broadcast_to(sinks[:, :, None], (depth, H_QB, LANES))
    ln1_g3, ln1_b3, ln2_g3, ln2_b3 = (t[:, None, :] for t in (ln1_g, ln1_b, ln2_g, ln2_b))

    hist = jnp.pad(state_conv, ((0, 0), (0, 0), (0, S_TILE - (CONV_WIDTH - 1)), (0, 0)))
    hist = hist.reshape(depth, n_dec * S_TILE, CONV_DIM)
    kc = cache_swa_k.reshape(depth, n_dec, WINDOW, KV_WIDTH)
    vc = cache_swa_v.reshape(depth, n_dec, WINDOW, KV_WIDTH)

    xp = x_prompt.reshape(n_batch * seq, D_MODEL)
    xs = jnp.pad(x_sample, ((0, 0), (ROW0, S_TILE - ROW0 - DEC_SEQ), (0, 0))).reshape(n_dec * S_TILE, D_MODEL)

    outs = [[] for _ in range(8)]
    for l in range(depth):
        proj = _in_proj(xp, w_in_r, l)
        oa, s_p, conv_p = _gdn_prompt(proj, n_batch, seq, conv_w, alog_v, dtb_v, nw_v, l)
        ob = _swa_prompt(proj, n_batch, seq, sink_v, l)
        xp = _out_proj_ln(xp, oa, ob, w_out_b, ln1_g3, ln1_b3, l, alpha)
        xp = _ffn_ln(xp, w_ffn_in_b, w_ffn_out_b, ln2_g3, ln2_b3, l, alpha)
        proj3 = proj.reshape(n_batch, seq, PROJ_PAD)
        outs[0].append(s_p)
        outs[1].append(conv_p.reshape(n_batch, SUBLANES, CONV_DIM)[:, SUBLANES - (CONV_WIDTH - 1):])
        outs[2].append(proj3[:, seq - WINDOW:, COL_KB:COL_KB + KV_WIDTH].reshape(n_batch, WINDOW, H_KVB, HD_B))
        outs[3].append(proj3[:, seq - WINDOW:, COL_VB:COL_VB + KV_WIDTH].reshape(n_batch, WINDOW, H_KVB, HD_B))
        proj = _in_proj(xs, w_in_r, l)
        oa, s_s, conv_s = _gdn_sample(proj, hist, state_delta, conv_w, alog_v, dtb_v, nw_v, l)
        ob, k_s, v_s = _swa_sample(proj, kc, vc, sink_v, l)
        xs = _out_proj_ln(xs, oa, ob, w_out_b, ln1_g3, ln1_b3, l, alpha)
        xs = _ffn_ln(xs, w_ffn_in_b, w_ffn_out_b, ln2_g3, ln2_b3, l, alpha)
        outs[4].append(s_s)
        outs[5].append(conv_s.reshape(n_dec, S_TILE, CONV_DIM)[:, :CONV_WIDTH - 1])
        outs[6].append(k_s.reshape(n_dec, WINDOW, H_KVB, HD_B))
        outs[7].append(v_s.reshape(n_dec, WINDOW, H_KVB, HD_B))

    y_prompt = xp.reshape(n_batch, seq, D_MODEL)
    y_sample = xs.reshape(n_dec, S_TILE, D_MODEL)[:, ROW0:ROW0 + DEC_SEQ]
    return (y_prompt, y_sample) + tuple(jnp.stack(o) for o in outs)
```

```python
import functools
import math
from typing import NamedTuple

import jax
import jax.numpy as jnp
from jax import lax
from jax.experimental import pallas as pl
from jax.experimental.pallas import tpu as pltpu

F32 = jnp.float32
BF16 = jnp.bfloat16

D_MODEL = 1024
H_A = 4
DK_A = 128
DV_A = 128
CONV_WIDTH = 4
CONV_DIM = 2 * H_A * DK_A + H_A * DV_A
A_WIDTH = H_A * DV_A
CHUNK = 64
HD_B = 64
H_QB = 8
H_KVB = 2
GQA_GROUP = H_QB // H_KVB
B_WIDTH = H_QB * HD_B
KV_WIDTH = H_KVB * HD_B
WINDOW = 128
D_FF = 2816
EPS = 1e-6

LANES = 128
SUBLANES = 8

COL_Z = CONV_DIM
COL_QB = COL_Z + A_WIDTH
COL_KB = COL_QB + B_WIDTH
COL_VB = COL_KB + KV_WIDTH
COL_BA = COL_VB + KV_WIDTH
PROJ_PAD = COL_BA + LANES
QB_HEAD_ORDER = tuple(hk * GQA_GROUP + g for g in range(GQA_GROUP) for hk in range(H_KVB))

S_TILE = SUBLANES
ROW0 = CONV_WIDTH - 1
DEC_SEQ = 4
SB = 8

TM = 512
SEG = 512
N_CHUNK = SEG // CHUNK
PREP_CHUNKS = 4
FF_CHUNK = 256
SWA_QB = 2
NEG = -1e30
VMEM_LIMIT = 56 * 1024 * 1024


def _params(n_axes, vmem=None):
    return pltpu.CompilerParams(
        dimension_semantics=("arbitrary",) * n_axes,
        vmem_limit_bytes=vmem if vmem is not None else VMEM_LIMIT)


def _dot(a, b):
    return jnp.dot(a.astype(BF16), b.astype(BF16), preferred_element_type=F32)


def _dot_nt(a, b):
    return lax.dot_general(a.astype(BF16), b.astype(BF16), (((1,), (1,)), ((), ())),
                           preferred_element_type=F32)


def _sigmoid(x):
    return 1.0 / (1.0 + jnp.exp(-x))


def _softplus(x):
    return jnp.maximum(x, 0.0) + jnp.log(1.0 + jnp.exp(-jnp.abs(x)))


def _layernorm(h, g, b):
    mu = jnp.mean(h, axis=-1, keepdims=True)
    d = h - mu
    var = jnp.mean(d * d, axis=-1, keepdims=True)
    return d * lax.rsqrt(var + EPS) * g + b


def _in_proj_kernel(x_ref, w_ref, o_ref):
    xb = x_ref[...].astype(BF16)
    n = o_ref.shape[1]
    for n0 in range(0, n, 512):
        n1 = min(n0 + 512, n)
        o_ref[:, n0:n1] = jnp.dot(xb, w_ref[:, n0:n1], preferred_element_type=F32)


def _in_proj(x, w_all, layer):
    t = x.shape[0]
    return pl.pallas_call(
        _in_proj_kernel,
        name="in_proj",
        grid=(t // TM,),
        in_specs=[pl.BlockSpec((TM, D_MODEL), lambda i: (i, 0)),
                  pl.BlockSpec((None, D_MODEL, PROJ_PAD), lambda i: (layer, 0, 0))],
        out_specs=pl.BlockSpec((TM, PROJ_PAD), lambda i: (i, 0)),
        out_shape=jax.ShapeDtypeStruct((t, PROJ_PAD), F32),
        compiler_params=_params(1),
    )(x, w_all)


def _mix_ffn_kernel(alpha, x_ref, oa_ref, ob_ref, wm_ref, g1_ref, b1_ref, wi_ref, wo_ref, g2_ref, b2_ref, o_ref):
    m = (jnp.dot(oa_ref[...].astype(BF16), wm_ref[0:A_WIDTH, :], preferred_element_type=F32)
         + jnp.dot(ob_ref[...].astype(BF16), wm_ref[A_WIDTH:, :], preferred_element_type=F32))
    x1 = _layernorm(alpha * x_ref[...] + m, g1_ref[...], b1_ref[...])
    xb = x1.astype(BF16)
    acc = alpha * x1
    for c0 in range(0, D_FF, FF_CHUNK):
        gate = jnp.dot(xb, wi_ref[:, c0:c0 + FF_CHUNK], preferred_element_type=F32)
        up = jnp.dot(xb, wi_ref[:, D_FF + c0:D_FF + c0 + FF_CHUNK], preferred_element_type=F32)
        h = gate * _sigmoid(gate) * up
        acc = acc + jnp.dot(h.astype(BF16), wo_ref[c0:c0 + FF_CHUNK, :], preferred_element_type=F32)
    o_ref[...] = _layernorm(acc, g2_ref[...], b2_ref[...])


def _mix_ffn(x, oa, ob, wm_all, g1_all, b1_all, wi_all, wo_all, g2_all, b2_all, layer, alpha):
    t = x.shape[0]
    vec = lambda: pl.BlockSpec((None, 1, D_MODEL), lambda i: (layer, 0, 0))
    return pl.pallas_call(
        functools.partial(_mix_ffn_kernel, alpha),
        name="mix_ffn",
        grid=(t // TM,),
        in_specs=[pl.BlockSpec((TM, D_MODEL), lambda i: (i, 0)),
                  pl.BlockSpec((TM, A_WIDTH), lambda i: (i, 0)),
                  pl.BlockSpec((TM, B_WIDTH), lambda i: (i, 0)),
                  pl.BlockSpec((None, D_MODEL, D_MODEL), lambda i: (layer, 0, 0)),
                  vec(), vec(),
                  pl.BlockSpec((None, D_MODEL, 2 * D_FF), lambda i: (layer, 0, 0)),
                  pl.BlockSpec((None, D_FF, D_MODEL), lambda i: (layer, 0, 0)),
                  vec(), vec()],
        out_specs=pl.BlockSpec((TM, D_MODEL), lambda i: (i, 0)),
        out_shape=jax.ShapeDtypeStruct((t, D_MODEL), F32),
        compiler_params=_params(1),
    )(x, oa, ob, wm_all, g1_all, b1_all, wi_all, wo_all, g2_all, b2_all)


def _conv_silu(x, cw):
    y = cw[3:4, :] * x
    for j in range(1, CONV_WIDTH):
        y = y + cw[CONV_WIDTH - 1 - j:CONV_WIDTH - j, :] * pltpu.roll(x, j, axis=0)
    return y * _sigmoid(y)


def _gates(ba, alog, dtb):
    beta = _sigmoid(ba)
    g = -jnp.exp(alog) * _softplus(ba + dtb)
    return beta, g


def _cumsum_rows(g, c):
    rin = lax.broadcasted_iota(jnp.int32, g.shape, 0) & (c - 1)
    s = 1
    while s < c:
        g = g + jnp.where(rin >= s, pltpu.roll(g, s, axis=0), 0.0)
        s *= 2
    return g


class _Masks(NamedTuple):
    causal: jax.Array
    neg_strict: jax.Array
    eye: jax.Array
    unit: tuple


def _group_masks(c):
    r = LANES
    shift = int(math.log2(c))
    ri = lax.broadcasted_iota(jnp.int32, (r, r), 0)
    ci = lax.broadcasted_iota(jnp.int32, (r, r), 1)
    same = (ri >> shift) == (ci >> shift)
    return _Masks(
        causal=jnp.where(same & (ri >= ci), 1.0, 0.0),
        neg_strict=jnp.where(same & (ri > ci), -1.0, 0.0),
        eye=jnp.where(ri == ci, 1.0, 0.0),
        unit=tuple(jnp.where((ci >> shift) == i, 1.0, 0.0) for i in range(r // c)))


def _gdn_prepare(groups, c, mk):
    r = LANES
    n_units = r // c
    n_factors = int(math.log2(c))
    each = lambda f, *lists: [f(*a) for a in zip(*lists)]

    q, k, v, beta, gcol = (list(t) for t in zip(*groups))
    qn = each(lambda t: t * lax.rsqrt(jnp.sum(t * t, axis=-1, keepdims=True) + EPS) * (DK_A ** -0.5), q)
    kn = each(lambda t: t * lax.rsqrt(jnp.sum(t * t, axis=-1, keepdims=True) + EPS), k)
    gc = each(lambda t: jnp.broadcast_to(t, (r, r)), gcol)
    e = each(lambda t: jnp.exp(jnp.minimum(t - t.T, 0.0)), gc)
    kb = each(lambda a, b: a * b, kn, beta)
    kq = each(lambda a, b, d: _dot_nt(jnp.concatenate([a, b], axis=0), d), kb, qn, kn)
    bk = each(lambda a, b: a[:r] * (b * mk.neg_strict), kq, e)
    aqk = each(lambda a, b: a[r:] * (b * mk.causal), kq, e)

    p = each(lambda t: mk.eye + t, bk)
    bk = each(lambda t: _dot(t, t), bk)
    for _ in range(n_factors - 2):
        st = each(lambda a, b: _dot(jnp.concatenate([a, b], axis=0), b), p, bk)
        p = each(lambda a, b: a + b[:r], p, st)
        bk = each(lambda t: t[r:], st)
    p = each(lambda a, b: a + _dot(a, b), p, bk)

    eg = each(jnp.exp, gc)
    uw = each(lambda a, b, d, f, h: _dot(a, jnp.concatenate([b * d, f * h], axis=1)), p, v, beta, kb, eg)
    qd = each(lambda a, b: a * b, qn, eg)

    out = []
    for gi in range(len(groups)):
        g_last = [gc[gi][(i + 1) * c - 1:(i + 1) * c, :] for i in range(n_units)]
        gl = jnp.concatenate([jnp.broadcast_to(t, (c, r)) for t in g_last], axis=0)
        kdt = (kn[gi] * jnp.exp(gl - gc[gi])).T
        u = uw[gi][:, :DV_A]
        w = uw[gi][:, DV_A:]
        wq = jnp.concatenate([jnp.concatenate([w[i * c:(i + 1) * c], qd[gi][i * c:(i + 1) * c]], axis=0)
                              for i in range(n_units)], axis=0).astype(BF16)
        lhs = jnp.concatenate([aqk[gi]] + [kdt * mk.unit[i] for i in range(n_units)], axis=0).astype(BF16)
        out.append((u, wq, lhs, [jnp.exp(t) for t in g_last]))
    return out


def _gdn_apply(u, wq, lhs, egl, s_list, z, nw, c):
    r = LANES
    n_units = r // c
    v_new, q_s = [], []
    for i in range(n_units):
        res = jnp.dot(wq[2 * c * i:2 * c * (i + 1)], s_list[i].astype(BF16), preferred_element_type=F32)
        v_new.append(u[i * c:(i + 1) * c] - res[:c])
        q_s.append(res[c:])
    vn = jnp.concatenate(v_new, axis=0)
    res = jnp.dot(lhs, vn.astype(BF16), preferred_element_type=F32)
    o = jnp.concatenate(q_s, axis=0) + res[:r]
    s_new = [s_list[i] * egl[i] + res[r * (i + 1):r * (i + 2)] for i in range(n_units)]
    on = o * lax.rsqrt(jnp.mean(o * o, axis=-1, keepdims=True) + EPS) * nw * (z * _sigmoid(z))
    return on, s_new


def _gdn_prompt_kernel(n_seg, prev_ref, qkv_ref, z_ref, ba_ref, cw_ref, alog_ref, dtb_ref, nw_ref,
                       o_ref, s_ref, conv_ref, xbuf, s_scr, u_scr, wq_scr, lhs_scr, egl_scr):
    seg = pl.program_id(1)
    n_pairs = H_A // 2

    @pl.when(seg == 0)
    def _():
        s_scr[...] = jnp.zeros_like(s_scr)

    xbuf[0:SUBLANES, :] = jnp.where(seg > 0, prev_ref[...], 0.0)
    xbuf[SUBLANES:, :] = qkv_ref[...]
    cw = cw_ref[...]
    alog = alog_ref[...]
    dtb = dtb_ref[...]
    nw = nw_ref[...]
    mk = _group_masks(CHUNK)

    def prepare(it, carry):
        groups, where = [], []
        for j in range(PREP_CHUNKS):
            ci = it * PREP_CHUNKS + j
            r0 = pl.multiple_of(ci * CHUNK, CHUNK)
            y = _conv_silu(xbuf[pl.ds(r0, CHUNK + SUBLANES), :], cw)[SUBLANES:]
            beta_all, g_all = _gates(ba_ref[pl.ds(r0, CHUNK), :], alog, dtb)
            gc_all = _cumsum_rows(g_all, CHUNK)
            for pair in range(n_pairs):
                heads = (2 * pair, 2 * pair + 1)
                cols = lambda base: jnp.concatenate(
                    [y[:, base + h * LANES:base + (h + 1) * LANES] for h in heads], axis=0)
                beta = jnp.concatenate([beta_all[:, h:h + 1] for h in heads], axis=0)
                gcol = jnp.concatenate([gc_all[:, H_A + h:H_A + h + 1] for h in heads], axis=0)
                groups.append((cols(0), cols(H_A * DK_A), cols(2 * H_A * DK_A), beta, gcol))
                where.append((ci, pair))
        for (ci, pair), (u, wq, lhs, egl) in zip(where, _gdn_prepare(groups, CHUNK, mk)):
            u_scr[ci, pair] = u
            wq_scr[ci, pair] = wq
            lhs_scr[ci, pair] = lhs
            for i in range(2):
                egl_scr[ci, pair, i:i + 1, :] = egl[i]
        return carry

    lax.fori_loop(0, N_CHUNK // PREP_CHUNKS, prepare, 0)

    def scan(ci, carry):
        r0 = pl.multiple_of(ci * CHUNK, CHUNK)
        z = z_ref[pl.ds(r0, CHUNK), :]
        for pair in range(n_pairs):
            heads = (2 * pair, 2 * pair + 1)
            zz = jnp.concatenate([z[:, h * DV_A:(h + 1) * DV_A] for h in heads], axis=0)
            egl = [egl_scr[ci, pair, i:i + 1, :] for i in range(2)]
            on, s_new = _gdn_apply(u_scr[ci, pair], wq_scr[ci, pair], lhs_scr[ci, pair], egl,
                                   [s_scr[h] for h in heads], zz, nw, CHUNK)
            for i, h in enumerate(heads):
                o_ref[pl.ds(r0, CHUNK), h * DV_A:(h + 1) * DV_A] = on[i * CHUNK:(i + 1) * CHUNK]
                s_scr[h] = s_new[i]
        return carry

    lax.fori_loop(0, N_CHUNK, scan, 0)

    @pl.when(seg == n_seg - 1)
    def _():
        s_ref[...] = s_scr[...]
        conv_ref[...] = xbuf[SEG:SEG + SUBLANES, :]


def _gdn_prompt(proj, n_batch, seq, cw_all, alog_all, dtb_all, nw_all, layer):
    n_seg = seq // SEG
    seg8 = SEG // SUBLANES
    seq8 = seq // SUBLANES
    n_pairs = H_A // 2

    def prev_map(b, s):
        return (jnp.maximum(b * seq8 + s * seg8 - 1, 0), 0)

    row_map = lambda b, s: (b * n_seg + s, 0)
    vec = lambda: pl.BlockSpec((None, 1, LANES), lambda b, s: (layer, 0, 0))
    return pl.pallas_call(
        functools.partial(_gdn_prompt_kernel, n_seg),
        name="gdn_prompt",
        grid=(n_batch, n_seg),
        in_specs=[pl.BlockSpec((SUBLANES, CONV_DIM), prev_map),
                  pl.BlockSpec((SEG, CONV_DIM), row_map),
                  pl.BlockSpec((SEG, A_WIDTH), lambda b, s: (b * n_seg + s, COL_Z // A_WIDTH)),
                  pl.BlockSpec((SEG, LANES), lambda b, s: (b * n_seg + s, COL_BA // LANES)),
                  pl.BlockSpec((None, CONV_WIDTH, CONV_DIM), lambda b, s: (layer, 0, 0)),
                  vec(), vec(), vec()],
        out_specs=[pl.BlockSpec((SEG, A_WIDTH), row_map),
                   pl.BlockSpec((None, H_A, DK_A, DV_A), lambda b, s: (b, 0, 0, 0)),
                   pl.BlockSpec((SUBLANES, CONV_DIM), lambda b, s: (b, 0))],
        out_shape=[jax.ShapeDtypeStruct((n_batch * seq, A_WIDTH), F32),
                   jax.ShapeDtypeStruct((n_batch, H_A, DK_A, DV_A), F32),
                   jax.ShapeDtypeStruct((n_batch * SUBLANES, CONV_DIM), F32)],
        scratch_shapes=[pltpu.VMEM((SEG + SUBLANES, CONV_DIM), F32),
                        pltpu.VMEM((H_A, DK_A, DV_A), F32),
                        pltpu.VMEM((N_CHUNK, n_pairs, LANES, DV_A), F32),
                        pltpu.VMEM((N_CHUNK, n_pairs, 2 * LANES, DK_A), BF16),
                        pltpu.VMEM((N_CHUNK, n_pairs, 3 * LANES, LANES), BF16),
                        pltpu.VMEM((N_CHUNK, n_pairs, SUBLANES, LANES), F32)],
        compiler_params=_params(2),
    )(proj, proj, proj, proj, cw_all, alog_all, dtb_all, nw_all)


def _token_rows(n_rows):
    row = lax.broadcasted_iota(jnp.int32, (n_rows, 1), 0) & (S_TILE - 1)
    return (row >= ROW0) & (row < ROW0 + DEC_SEQ)


def _gdn_sample_kernel(qkv_ref, hist_ref, z_ref, ba_ref, cw_ref, alog_ref, dtb_ref, nw_ref, s_in_ref,
                       o_ref, s_out_ref, conv_ref):
    rows = SB * S_TILE
    valid = _token_rows(rows)
    x = jnp.where(valid, qkv_ref[...], 0.0) + hist_ref[...]
    conv_ref[...] = pltpu.roll(x, rows - DEC_SEQ, axis=0)
    y = _conv_silu(x, cw_ref[...])
    beta_all, g_all = _gates(ba_ref[...], alog_ref[...], dtb_ref[...])
    beta_all = jnp.where(valid, beta_all, 0.0)
    gc_all = _cumsum_rows(jnp.where(valid, g_all, 0.0), S_TILE)
    z = z_ref[...]
    ym = jnp.where(valid, y, 0.0)

    all_units = [(b, h) for b in range(SB) for h in range(H_A)]
    per_group = LANES // S_TILE
    unit_groups = [all_units[i:i + per_group] for i in range(0, len(all_units), per_group)]

    def stack(units, src, base, width=LANES):
        return jnp.concatenate(
            [src[b * S_TILE:(b + 1) * S_TILE, base + h * width:base + (h + 1) * width] for b, h in units], axis=0)

    mk = _group_masks(S_TILE)
    prepared = _gdn_prepare(
        [(stack(us, y, 0), stack(us, ym, H_A * DK_A), stack(us, ym, 2 * H_A * DK_A),
          stack(us, beta_all, 0, 1), stack(us, gc_all, H_A, 1)) for us in unit_groups], S_TILE, mk)
    for us, (u, wq, lhs, egl) in zip(unit_groups, prepared):
        on, s_new = _gdn_apply(u, wq, lhs, egl, [s_in_ref[b, h] for b, h in us], stack(us, z, 0), nw_ref[...],
                               S_TILE)
        for i, (b, h) in enumerate(us):
            o_ref[b * S_TILE:(b + 1) * S_TILE, h * DV_A:(h + 1) * DV_A] = on[i * S_TILE:(i + 1) * S_TILE]
            s_out_ref[b, h] = s_new[i]


def _gdn_sample(proj, hist, state_all, cw_all, alog_all, dtb_all, nw_all, layer):
    rows = SB * S_TILE
    n_seq = proj.shape[0] // S_TILE
    vec = lambda: pl.BlockSpec((None, 1, LANES), lambda i: (layer, 0, 0))
    return pl.pallas_call(
        _gdn_sample_kernel,
        name="gdn_sample",
        grid=(n_seq // SB,),
        in_specs=[pl.BlockSpec((rows, CONV_DIM), lambda i: (i, 0)),
                  pl.BlockSpec((None, rows, CONV_DIM), lambda i: (layer, i, 0)),
                  pl.BlockSpec((rows, A_WIDTH), lambda i: (i, COL_Z // A_WIDTH)),
                  pl.BlockSpec((rows, LANES), lambda i: (i, COL_BA // LANES)),
                  pl.BlockSpec((None, CONV_WIDTH, CONV_DIM), lambda i: (layer, 0, 0)),
                  vec(), vec(), vec(),
                  pl.BlockSpec((None, SB, H_A, DK_A, DV_A), lambda i: (layer, i, 0, 0, 0))],
        out_specs=[pl.BlockSpec((rows, A_WIDTH), lambda i: (i, 0)),
                   pl.BlockSpec((SB, H_A, DK_A, DV_A), lambda i: (i, 0, 0, 0)),
                   pl.BlockSpec((rows, CONV_DIM), lambda i: (i, 0))],
        out_shape=[jax.ShapeDtypeStruct((n_seq * S_TILE, A_WIDTH), F32),
                   jax.ShapeDtypeStruct((n_seq, H_A, DK_A, DV_A), F32),
                   jax.ShapeDtypeStruct((n_seq * S_TILE, CONV_DIM), F32)],
        compiler_params=_params(1),
    )(proj, hist, proj, proj, cw_all, alog_all, dtb_all, nw_all, state_all)


def _swa_heads(items, mask, sink_ref, q_rows):
    lane_head = lax.broadcasted_iota(jnp.int32, (1, LANES), 1) >> int(math.log2(HD_B))
    chains = [(it, hk) for it in range(len(items)) for hk in range(H_KVB)]
    each = lambda f, *lists: [f(*a) for a in zip(*lists)]
    qcat = [jnp.concatenate([q[:, gi * LANES:(gi + 1) * LANES] for gi in range(GQA_GROUP)], axis=0)
            * (HD_B ** -0.5) for q, _, _ in items]
    sk_head = [jnp.concatenate(
        [jnp.broadcast_to(sink_ref[hk * GQA_GROUP + gi:hk * GQA_GROUP + gi + 1, 0:1], (q_rows, 1))
         for gi in range(GQA_GROUP)], axis=0) for hk in range(H_KVB)]
    sk = [sk_head[hk] for _, hk in chains]
    kh = [jnp.where(lane_head == hk, items[it][1], 0.0) for it, hk in chains]
    vh = [jnp.where(lane_head == hk, items[it][2], 0.0) for it, hk in chains]
    s = [jnp.where(mask, _dot_nt(qcat[it], kh[ch]), NEG) for ch, (it, _) in enumerate(chains)]
    m = each(lambda a, b: jnp.maximum(jnp.max(a, axis=-1, keepdims=True), b), s, sk)
    p = each(lambda a, b: jnp.exp(a - b), s, m)
    den = each(lambda a, b, d: jnp.sum(a, axis=-1, keepdims=True) + jnp.exp(b - d), p, sk, m)
    o = each(lambda a, b, d: _dot(a, b) * (1.0 / d), p, vh, den)
    outs = []
    for it in range(len(items)):
        tot = o[H_KVB * it]
        for hk in range(1, H_KVB):
            tot = tot + o[H_KVB * it + hk]
        outs.append(jnp.concatenate([tot[gi * q_rows:(gi + 1) * q_rows] for gi in range(GQA_GROUP)], axis=1))
    return outs


def _swa_prompt_kernel(q_ref, kc_ref, kp_ref, vc_ref, vp_ref, sink_ref, o_ref):
    n0 = pl.program_id(1) * SWA_QB
    keys = 2 * WINDOW
    k_ext = jnp.concatenate([kp_ref[...], kc_ref[...]], axis=0)
    v_ext = jnp.concatenate([vp_ref[...], vc_ref[...]], axis=0)
    c = lax.broadcasted_iota(jnp.int32, (keys, 2 * WINDOW), 0)
    r = lax.broadcasted_iota(jnp.int32, (keys, 2 * WINDOW), 1) & (WINDOW - 1)
    band = (c > r) & (c <= r + WINDOW)
    shift = int(math.log2(HD_B))
    lane_head = lax.broadcasted_iota(jnp.int32, (1, LANES), 1) >> shift
    row = lax.broadcasted_iota(jnp.int32, (LANES, 1), 0)
    row_head = row >> shift
    each = lambda f, *lists: [f(*a) for a in zip(*lists)]
    n_gp = GQA_GROUP // 2

    chains = [(j, hk, gp) for j in range(SWA_QB) for hk in range(H_KVB) for gp in range(n_gp)]
    mask = [band & (c + (n0 + j) * WINDOW >= WINDOW) for j in range(SWA_QB)]
    kh, vth, qpair = {}, {}, {}
    for j in range(SWA_QB):
        k_all = k_ext[j * WINDOW:(j + 2) * WINDOW]
        vt = v_ext[j * WINDOW:(j + 2) * WINDOW].T
        qs = q_ref[j * WINDOW:(j + 1) * WINDOW, :] * (HD_B ** -0.5)
        for hk in range(H_KVB):
            kh[j, hk] = jnp.where(lane_head == hk, k_all, 0.0).astype(BF16)
            vth[j, hk] = jnp.where(row_head == hk, vt, jnp.where(row == (1 - hk) * HD_B, 1.0, 0.0)).astype(BF16)
        for gp in range(n_gp):
            qpair[j, gp] = jnp.concatenate([qs[:, (2 * gp) * LANES:(2 * gp + 1) * LANES],
                                            qs[:, (2 * gp + 1) * LANES:(2 * gp + 2) * LANES]],
                                           axis=0).astype(BF16)
    ones_row = [(1 - hk) * HD_B for _, hk, _ in chains]
    sk = [jnp.concatenate([sink_ref[hk * GQA_GROUP + 2 * gp + i:hk * GQA_GROUP + 2 * gp + i + 1, :]
                           for i in range(2)], axis=1) for _, hk, gp in chains]
    st = [jnp.where(mask[j], _dot_nt(kh[j, hk], qpair[j, gp]), NEG) for j, hk, gp in chains]
    m = each(lambda a, b: jnp.maximum(jnp.max(a, axis=0, keepdims=True), b), st, sk)
    pt = each(lambda a, b: jnp.exp(a - b).astype(BF16), st, m)
    ot = [jnp.dot(vth[j, hk], pt[ch], preferred_element_type=F32) for ch, (j, hk, _) in enumerate(chains)]
    den = each(lambda a, i, b, d: a[i:i + 1, :] + jnp.exp(b - d), ot, ones_row, sk, m)
    ot = [jnp.where(row_head == hk, ot[ch] * (1.0 / den[ch]), 0.0) for ch, (_, hk, _) in enumerate(chains)]
    for j in range(SWA_QB):
        tiles = []
        for gp in range(n_gp):
            tot = ot[chains.index((j, 0, gp))]
            for hk in range(1, H_KVB):
                tot = tot + ot[chains.index((j, hk, gp))]
            tiles += [tot[:, :WINDOW].T, tot[:, WINDOW:].T]
        o_ref[j * WINDOW:(j + 1) * WINDOW, :] = jnp.concatenate(tiles, axis=1)


def _swa_prompt(proj, n_batch, seq, sink_all, layer):
    nb = seq // WINDOW
    ns = nb // SWA_QB
    qrows = SWA_QB * WINDOW
    cur = lambda col: (lambda b, n: (b * ns + n, col))
    prev = lambda col: (lambda b, n: (b * nb + jnp.maximum(n * SWA_QB - 1, 0), col))
    return pl.pallas_call(
        _swa_prompt_kernel,
        name="swa_prompt",
        grid=(n_batch, ns),
        in_specs=[pl.BlockSpec((qrows, B_WIDTH), cur(COL_QB // B_WIDTH)),
                  pl.BlockSpec((qrows, KV_WIDTH), cur(COL_KB // KV_WIDTH)),
                  pl.BlockSpec((WINDOW, KV_WIDTH), prev(COL_KB // KV_WIDTH)),
                  pl.BlockSpec((qrows, KV_WIDTH), cur(COL_VB // KV_WIDTH)),
                  pl.BlockSpec((WINDOW, KV_WIDTH), prev(COL_VB // KV_WIDTH)),
                  pl.BlockSpec((None, H_QB, LANES), lambda b, n: (layer, 0, 0))],
        out_specs=pl.BlockSpec((qrows, B_WIDTH), lambda b, n: (b * ns + n, 0)),
        out_shape=jax.ShapeDtypeStruct((n_batch * seq, B_WIDTH), F32),
        compiler_params=_params(2),
    )(proj, proj, proj, proj, proj, sink_all)


def _shift_cache(cache, new_tile):
    rolled = pltpu.roll(cache, WINDOW - DEC_SEQ, axis=0)
    moved = pltpu.roll(new_tile, S_TILE - DEC_SEQ - ROW0, axis=0)
    row = lax.broadcasted_iota(jnp.int32, (S_TILE, 1), 0)
    tail = jnp.where(row >= S_TILE - DEC_SEQ, moved, rolled[WINDOW - S_TILE:])
    return jnp.concatenate([rolled[:WINDOW - S_TILE], tail], axis=0)


def _swa_sample_kernel(q_ref, k_ref, v_ref, kc_ref, vc_ref, sink_ref, o_ref, ko_ref, vo_ref):
    shape = (GQA_GROUP * S_TILE, 2 * WINDOW)
    r = (lax.broadcasted_iota(jnp.int32, shape, 0) & (S_TILE - 1)) - ROW0
    c = lax.broadcasted_iota(jnp.int32, shape, 1)
    j = c - WINDOW - ROW0
    mask = ((c < WINDOW) & (c > r)) | ((j >= 0) & (j < DEC_SEQ) & (j <= r))
    pad = jnp.zeros((WINDOW - S_TILE, KV_WIDTH), F32)
    items = []
    for b in range(SB):
        rows = slice(b * S_TILE, (b + 1) * S_TILE)
        k_new = k_ref[rows, :]
        v_new = v_ref[rows, :]
        items.append((q_ref[rows, :], jnp.concatenate([kc_ref[b], k_new, pad], axis=0),
                      jnp.concatenate([vc_ref[b], v_new, pad], axis=0)))
        ko_ref[b] = _shift_cache(kc_ref[b], k_new)
        vo_ref[b] = _shift_cache(vc_ref[b], v_new)
    for b, o in enumerate(_swa_heads(items, mask, sink_ref, S_TILE)):
        o_ref[b * S_TILE:(b + 1) * S_TILE, :] = o


def _swa_sample(proj, kc_all, vc_all, sink_all, layer):
    rows = SB * S_TILE
    n_seq = proj.shape[0] // S_TILE
    cache_in = lambda: pl.BlockSpec((None, SB, WINDOW, KV_WIDTH), lambda i: (layer, i, 0, 0))
    cache_out = lambda: pl.BlockSpec((SB, WINDOW, KV_WIDTH), lambda i: (i, 0, 0))
    return pl.pallas_call(
        _swa_sample_kernel,
        name="swa_sample",
        grid=(n_seq // SB,),
        in_specs=[pl.BlockSpec((rows, B_WIDTH), lambda i: (i, COL_QB // B_WIDTH)),
                  pl.BlockSpec((rows, KV_WIDTH), lambda i: (i, COL_KB // KV_WIDTH)),
                  pl.BlockSpec((rows, KV_WIDTH), lambda i: (i, COL_VB // KV_WIDTH)),
                  cache_in(), cache_in(),
                  pl.BlockSpec((None, H_QB, LANES), lambda i: (layer, 0, 0))],
        out_specs=[pl.BlockSpec((rows, B_WIDTH), lambda i: (i, 0)), cache_out(), cache_out()],
        out_shape=[jax.ShapeDtypeStruct((n_seq * S_TILE, B_WIDTH), F32),
                   jax.ShapeDtypeStruct((n_seq, WINDOW, KV_WIDTH), F32),
                   jax.ShapeDtypeStruct((n_seq, WINDOW, KV_WIDTH), F32)],
        compiler_params=_params(1),
    )(proj, proj, proj, kc_all, vc_all, sink_all)


def kernel(x_prompt, x_sample, state_delta, state_conv, cache_swa_k, cache_swa_v, w_in, conv_w, a_log,
           dt_bias, norm_a_w, sinks, w_out, ln1_g, ln1_b, w_ffn_in, w_ffn_out, ln2_g, ln2_b):
    depth = w_in.shape[0]
    n_batch, seq, _ = x_prompt.shape
    n_dec, dec_seq, _ = x_sample.shape
    assert dec_seq == DEC_SEQ and seq % SEG == 0 and n_dec % SB == 0
    assert cache_swa_k.shape[2] == WINDOW
    alpha = (2 * depth) ** 0.25

    c0 = CONV_DIM + A_WIDTH
    q0 = c0 + 2 * H_A
    order = jnp.array(QB_HEAD_ORDER)
    w_q = w_in[:, :, q0:q0 + B_WIDTH].reshape(depth, D_MODEL, H_QB, HD_B)[:, :, order]
    w_in_r = jnp.concatenate(
        [w_in[:, :, :c0], w_q.reshape(depth, D_MODEL, B_WIDTH), w_in[:, :, q0 + B_WIDTH:],
         w_in[:, :, c0:q0], jnp.zeros((depth, D_MODEL, LANES - 2 * H_A), w_in.dtype)], axis=-1).astype(BF16)
    w_out_bq = w_out[:, A_WIDTH:].reshape(depth, H_QB, HD_B, D_MODEL)[:, order].reshape(depth, B_WIDTH, D_MODEL)
    w_out_b = jnp.concatenate([w_out[:, :A_WIDTH], w_out_bq], axis=1).astype(BF16)
    w_ffn_in_b = w_ffn_in.astype(BF16)
    w_ffn_out_b = w_ffn_out.astype(BF16)
    lane_pad = lambda t: jnp.pad(t, ((0, 0), (H_A, LANES - 2 * H_A)))[:, None, :]
    alog_v = lane_pad(a_log)
    dtb_v = lane_pad(dt_bias)
    nw_v = norm_a_w[:, None, :]
    sink_v = jnp.broadcast_to(sinks[:, :, None], (depth, H_QB, LANES))
    ln1_g3, ln1_b3, ln2_g3, ln2_b3 = (t[:, None, :] for t in (ln1_g, ln1_b, ln2_g, ln2_b))

    hist = jnp.pad(state_conv, ((0, 0), (0, 0), (0, S_TILE - (CONV_WIDTH - 1)), (0, 0)))
    hist = hist.reshape(depth, n_dec * S_TILE, CONV_DIM)
    kc = cache_swa_k.reshape(depth, n_dec, WINDOW, KV_WIDTH)
    vc = cache_swa_v.reshape(depth, n_dec, WINDOW, KV_WIDTH)

    xp = x_prompt.reshape(n_batch * seq, D_MODEL)
    xs = jnp.pad(x_sample, ((0, 0), (ROW0, S_TILE - ROW0 - DEC_SEQ), (0, 0))).reshape(n_dec * S_TILE, D_MODEL)

    outs = [[] for _ in range(8)]
    for l in range(depth):
        proj = _in_proj(xp, w_in_r, l)
        oa, s_p, conv_p = _gdn_prompt(proj, n_batch, seq, conv_w, alog_v, dtb_v, nw_v, l)
        ob = _swa_prompt(proj, n_batch, seq, sink_v, l)
        xp = _mix_ffn(xp, oa, ob, w_out_b, ln1_g3, ln1_b3, w_ffn_in_b, w_ffn_out_b, ln2_g3, ln2_b3, l, alpha)
        proj3 = proj.reshape(n_batch, seq, PROJ_PAD)
        outs[0].append(s_p)
        outs[1].append(conv_p.reshape(n_batch, SUBLANES, CONV_DIM)[:, SUBLANES - (CONV_WIDTH - 1):])
        outs[2].append(proj3[:, seq - WINDOW:, COL_KB:COL_KB + KV_WIDTH].reshape(n_batch, WINDOW, H_KVB, HD_B))
        outs[3].append(proj3[:, seq - WINDOW:, COL_VB:COL_VB + KV_WIDTH].reshape(n_batch, WINDOW, H_KVB, HD_B))
        proj = _in_proj(xs, w_in_r, l)
        oa, s_s, conv_s = _gdn_sample(proj, hist, state_delta, conv_w, alog_v, dtb_v, nw_v, l)
        ob, k_s, v_s = _swa_sample(proj, kc, vc, sink_v, l)
        xs = _mix_ffn(xs, oa, ob, w_out_b, ln1_g3, ln1_b3, w_ffn_in_b, w_ffn_out_b, ln2_g3, ln2_b3, l, alpha)
        outs[4].append(s_s)
        outs[5].append(conv_s.reshape(n_dec, S_TILE, CONV_DIM)[:, :CONV_WIDTH - 1])
        outs[6].append(k_s.reshape(n_dec, WINDOW, H_KVB, HD_B))
        outs[7].append(v_s.reshape(n_dec, WINDOW, H_KVB, HD_B))

    y_prompt = xp.reshape(n_batch, seq, D_MODEL)
    y_sample = xs.reshape(n_dec, S_TILE, D_MODEL)[:, ROW0:ROW0 + DEC_SEQ]
    return (y_prompt, y_sample) + tuple(jnp.stack(o) for o in outs)
```

```python
import functools
import math
from typing import NamedTuple

import jax
import jax.numpy as jnp
from jax import lax
from jax.experimental import pallas as pl
from jax.experimental.pallas import tpu as pltpu

F32 = jnp.float32
BF16 = jnp.bfloat16

D_MODEL = 1024
H_A = 4
DK_A = 128
DV_A = 128
CONV_WIDTH = 4
CONV_DIM = 2 * H_A * DK_A + H_A * DV_A
A_WIDTH = H_A * DV_A
CHUNK = 64
HD_B = 64
H_QB = 8
H_KVB = 2
GQA_GROUP = H_QB // H_KVB
B_WIDTH = H_QB * HD_B
KV_WIDTH = H_KVB * HD_B
WINDOW = 128
D_FF = 2816
EPS = 1e-6

LANES = 128
SUBLANES = 8

COL_Z = CONV_DIM
COL_QB = COL_Z + A_WIDTH
COL_KB = COL_QB + B_WIDTH
COL_VB = COL_KB + KV_WIDTH
COL_BA = COL_VB + KV_WIDTH
PROJ_PAD = COL_BA + LANES
QB_HEAD_ORDER = tuple(hk * GQA_GROUP + g for g in range(GQA_GROUP) for hk in range(H_KVB))

S_TILE = SUBLANES
ROW0 = CONV_WIDTH - 1
DEC_SEQ = 4
SB = 8

TM = 512
SEG = 512
N_CHUNK = SEG // CHUNK
PREP_CHUNKS = 2
N_SEQ = 2
FF_CHUNK = 256
SWA_QB = 4
NEG = -1e30
VMEM_LIMIT = 56 * 1024 * 1024


def _params(n_axes, vmem=None):
    return pltpu.CompilerParams(
        dimension_semantics=("arbitrary",) * n_axes,
        vmem_limit_bytes=vmem if vmem is not None else VMEM_LIMIT)


def _dot(a, b):
    return jnp.dot(a.astype(BF16), b.astype(BF16), preferred_element_type=F32)


def _dot_nt(a, b):
    return lax.dot_general(a.astype(BF16), b.astype(BF16), (((1,), (1,)), ((), ())),
                           preferred_element_type=F32)


def _sigmoid(x):
    return 1.0 / (1.0 + jnp.exp(-x))


def _softplus(x):
    return jnp.maximum(x, 0.0) + jnp.log(1.0 + jnp.exp(-jnp.abs(x)))


def _layernorm(h, g, b):
    mu = jnp.mean(h, axis=-1, keepdims=True)
    d = h - mu
    var = jnp.mean(d * d, axis=-1, keepdims=True)
    return d * lax.rsqrt(var + EPS) * g + b


def _in_proj_kernel(x_ref, w_ref, o_ref):
    xb = x_ref[...].astype(BF16)
    n = o_ref.shape[1]
    for n0 in range(0, n, 512):
        n1 = min(n0 + 512, n)
        o_ref[:, n0:n1] = jnp.dot(xb, w_ref[:, n0:n1], preferred_element_type=F32)


def _in_proj(x, w_all, layer):
    t = x.shape[0]
    return pl.pallas_call(
        _in_proj_kernel,
        name="in_proj",
        grid=(t // TM,),
        in_specs=[pl.BlockSpec((TM, D_MODEL), lambda i: (i, 0)),
                  pl.BlockSpec((None, D_MODEL, PROJ_PAD), lambda i: (layer, 0, 0))],
        out_specs=pl.BlockSpec((TM, PROJ_PAD), lambda i: (i, 0)),
        out_shape=jax.ShapeDtypeStruct((t, PROJ_PAD), F32),
        compiler_params=_params(1),
    )(x, w_all)


def _mix_ffn_kernel(alpha, x_ref, oa_ref, ob_ref, wm_ref, g1_ref, b1_ref, wi_ref, wo_ref, g2_ref, b2_ref, o_ref):
    m = (jnp.dot(oa_ref[...].astype(BF16), wm_ref[0:A_WIDTH, :], preferred_element_type=F32)
         + jnp.dot(ob_ref[...].astype(BF16), wm_ref[A_WIDTH:, :], preferred_element_type=F32))
    x1 = _layernorm(alpha * x_ref[...] + m, g1_ref[...], b1_ref[...])
    xb = x1.astype(BF16)
    acc = alpha * x1
    for c0 in range(0, D_FF, FF_CHUNK):
        gate = jnp.dot(xb, wi_ref[:, c0:c0 + FF_CHUNK], preferred_element_type=F32)
        up = jnp.dot(xb, wi_ref[:, D_FF + c0:D_FF + c0 + FF_CHUNK], preferred_element_type=F32)
        h = gate * _sigmoid(gate) * up
        acc = acc + jnp.dot(h.astype(BF16), wo_ref[c0:c0 + FF_CHUNK, :], preferred_element_type=F32)
    o_ref[...] = _layernorm(acc, g2_ref[...], b2_ref[...])


def _mix_ffn(x, oa, ob, wm_all, g1_all, b1_all, wi_all, wo_all, g2_all, b2_all, layer, alpha):
    t = x.shape[0]
    vec = lambda: pl.BlockSpec((None, 1, D_MODEL), lambda i: (layer, 0, 0))
    return pl.pallas_call(
        functools.partial(_mix_ffn_kernel, alpha),
        name="mix_ffn",
        grid=(t // TM,),
        in_specs=[pl.BlockSpec((TM, D_MODEL), lambda i: (i, 0)),
                  pl.BlockSpec((TM, A_WIDTH), lambda i: (i, 0)),
                  pl.BlockSpec((TM, B_WIDTH), lambda i: (i, 0)),
                  pl.BlockSpec((None, D_MODEL, D_MODEL), lambda i: (layer, 0, 0)),
                  vec(), vec(),
                  pl.BlockSpec((None, D_MODEL, 2 * D_FF), lambda i: (layer, 0, 0)),
                  pl.BlockSpec((None, D_FF, D_MODEL), lambda i: (layer, 0, 0)),
                  vec(), vec()],
        out_specs=pl.BlockSpec((TM, D_MODEL), lambda i: (i, 0)),
        out_shape=jax.ShapeDtypeStruct((t, D_MODEL), F32),
        compiler_params=_params(1),
    )(x, oa, ob, wm_all, g1_all, b1_all, wi_all, wo_all, g2_all, b2_all)


def _conv_silu(x, cw):
    y = cw[3:4, :] * x
    for j in range(1, CONV_WIDTH):
        y = y + cw[CONV_WIDTH - 1 - j:CONV_WIDTH - j, :] * pltpu.roll(x, j, axis=0)
    return y * _sigmoid(y)


def _gates(ba, alog, dtb):
    beta = _sigmoid(ba)
    g = -jnp.exp(alog) * _softplus(ba + dtb)
    return beta, g


def _cumsum_rows(g, c):
    rin = lax.broadcasted_iota(jnp.int32, g.shape, 0) & (c - 1)
    s = 1
    while s < c:
        g = g + jnp.where(rin >= s, pltpu.roll(g, s, axis=0), 0.0)
        s *= 2
    return g


class _Masks(NamedTuple):
    causal: jax.Array
    neg_strict: jax.Array
    eye: jax.Array
    unit: tuple


def _group_masks(c):
    r = LANES
    shift = int(math.log2(c))
    ri = lax.broadcasted_iota(jnp.int32, (r, r), 0)
    ci = lax.broadcasted_iota(jnp.int32, (r, r), 1)
    same = (ri >> shift) == (ci >> shift)
    return _Masks(
        causal=jnp.where(same & (ri >= ci), 1.0, 0.0),
        neg_strict=jnp.where(same & (ri > ci), -1.0, 0.0),
        eye=jnp.where(ri == ci, 1.0, 0.0),
        unit=tuple(jnp.where((ci >> shift) == i, 1.0, 0.0) for i in range(r // c)))


def _gdn_prepare(groups, c, mk):
    r = LANES
    n_units = r // c
    n_factors = int(math.log2(c))
    each = lambda f, *lists: [f(*a) for a in zip(*lists)]

    q, k, v, beta, gcol = (list(t) for t in zip(*groups))
    qn = each(lambda t: t * lax.rsqrt(jnp.sum(t * t, axis=-1, keepdims=True) + EPS) * (DK_A ** -0.5), q)
    kn = each(lambda t: t * lax.rsqrt(jnp.sum(t * t, axis=-1, keepdims=True) + EPS), k)
    gc = each(lambda t: jnp.broadcast_to(t, (r, r)), gcol)
    e = each(lambda t: jnp.exp(jnp.minimum(t - t.T, 0.0)), gc)
    kb = each(lambda a, b: a * b, kn, beta)
    kq = each(lambda a, b, d: _dot_nt(jnp.concatenate([a, b], axis=0), d), kb, qn, kn)
    bk = each(lambda a, b: a[:r] * (b * mk.neg_strict), kq, e)
    aqk = each(lambda a, b: a[r:] * (b * mk.causal), kq, e)

    p = each(lambda t: mk.eye + t, bk)
    bk = each(lambda t: _dot(t, t), bk)
    for _ in range(n_factors - 2):
        st = each(lambda a, b: _dot(jnp.concatenate([a, b], axis=0), b), p, bk)
        p = each(lambda a, b: a + b[:r], p, st)
        bk = each(lambda t: t[r:], st)
    p = each(lambda a, b: a + _dot(a, b), p, bk)

    eg = each(jnp.exp, gc)
    uw = each(lambda a, b, d, f, h: _dot(a, jnp.concatenate([b * d, f * h], axis=1)), p, v, beta, kb, eg)
    qd = each(lambda a, b: a * b, qn, eg)

    out = []
    for gi in range(len(groups)):
        g_last = [gc[gi][(i + 1) * c - 1:(i + 1) * c, :] for i in range(n_units)]
        gl = jnp.concatenate([jnp.broadcast_to(t, (c, r)) for t in g_last], axis=0)
        kdt = (kn[gi] * jnp.exp(gl - gc[gi])).T
        u = uw[gi][:, :DV_A]
        w = uw[gi][:, DV_A:]
        wq = jnp.concatenate([jnp.concatenate([w[i * c:(i + 1) * c], qd[gi][i * c:(i + 1) * c]], axis=0)
                              for i in range(n_units)], axis=0).astype(BF16)
        lhs = jnp.concatenate([aqk[gi]] + [kdt * mk.unit[i] for i in range(n_units)], axis=0).astype(BF16)
        out.append((u, wq, lhs, [jnp.exp(t) for t in g_last]))
    return out


def _gdn_apply(items, nw, c):
    r = LANES
    n_units = r // c
    res1 = [[jnp.dot(wq[2 * c * i:2 * c * (i + 1)], st[i].astype(BF16), preferred_element_type=F32)
             for i in range(n_units)] for _, wq, _, _, st, _ in items]
    vn = [jnp.concatenate([it[0][i * c:(i + 1) * c] - r1[i][:c] for i in range(n_units)], axis=0)
          for it, r1 in zip(items, res1)]
    res2 = [jnp.dot(it[2], v.astype(BF16), preferred_element_type=F32) for it, v in zip(items, vn)]
    out = []
    for (_, _, _, egl, st, z), r1, r2 in zip(items, res1, res2):
        o = jnp.concatenate([r1[i][c:] for i in range(n_units)], axis=0) + r2[:r]
        s_new = [st[i] * egl[i] + r2[r * (i + 1):r * (i + 2)] for i in range(n_units)]
        out.append((o * lax.rsqrt(jnp.mean(o * o, axis=-1, keepdims=True) + EPS) * nw * (z * _sigmoid(z)), s_new))
    return out


def _gdn_prompt_kernel(n_seg, prev_ref, qkv_ref, z_ref, ba_ref, cw_ref, alog_ref, dtb_ref, nw_ref,
                       o_ref, s_ref, conv_ref, xbuf, s_scr, u_scr, wq_scr, lhs_scr, egl_scr):
    seg = pl.program_id(1)
    n_pairs = H_A // 2

    @pl.when(seg == 0)
    def _():
        s_scr[...] = jnp.zeros_like(s_scr)

    xbuf[:, 0:SUBLANES, :] = jnp.where(seg > 0, prev_ref[...], 0.0)
    xbuf[:, SUBLANES:, :] = qkv_ref[...]
    cw = cw_ref[...]
    alog = alog_ref[...]
    dtb = dtb_ref[...]
    nw = nw_ref[...]
    mk = _group_masks(CHUNK)

    def prepare(it, carry):
        groups, where = [], []
        for j in range(PREP_CHUNKS):
            ci = it * PREP_CHUNKS + j
            r0 = pl.multiple_of(ci * CHUNK, CHUNK)
            for q in range(N_SEQ):
                y = _conv_silu(xbuf[q, pl.ds(r0, CHUNK + SUBLANES), :], cw)[SUBLANES:]
                beta_all, g_all = _gates(ba_ref[q, pl.ds(r0, CHUNK), :], alog, dtb)
                gc_all = _cumsum_rows(g_all, CHUNK)
                for pair in range(n_pairs):
                    heads = (2 * pair, 2 * pair + 1)
                    cols = lambda base: jnp.concatenate(
                        [y[:, base + h * LANES:base + (h + 1) * LANES] for h in heads], axis=0)
                    beta = jnp.concatenate([beta_all[:, h:h + 1] for h in heads], axis=0)
                    gcol = jnp.concatenate([gc_all[:, H_A + h:H_A + h + 1] for h in heads], axis=0)
                    groups.append((cols(0), cols(H_A * DK_A), cols(2 * H_A * DK_A), beta, gcol))
                    where.append((ci, q, pair))
        for (ci, q, pair), (u, wq, lhs, egl) in zip(where, _gdn_prepare(groups, CHUNK, mk)):
            u_scr[ci, q, pair] = u
            wq_scr[ci, q, pair] = wq
            lhs_scr[ci, q, pair] = lhs
            for i in range(2):
                egl_scr[ci, q, pair, i:i + 1, :] = egl[i]
        return carry

    lax.fori_loop(0, N_CHUNK // PREP_CHUNKS, prepare, 0)

    def scan(ci, carry):
        r0 = pl.multiple_of(ci * CHUNK, CHUNK)
        items, where = [], []
        for q in range(N_SEQ):
            z = z_ref[q, pl.ds(r0, CHUNK), :]
            for pair in range(n_pairs):
                heads = (2 * pair, 2 * pair + 1)
                zz = jnp.concatenate([z[:, h * DV_A:(h + 1) * DV_A] for h in heads], axis=0)
                egl = [egl_scr[ci, q, pair, i:i + 1, :] for i in range(2)]
                items.append((u_scr[ci, q, pair], wq_scr[ci, q, pair], lhs_scr[ci, q, pair], egl,
                              [s_scr[q, h] for h in heads], zz))
                where.append((q, heads))
        for (q, heads), (on, s_new) in zip(where, _gdn_apply(items, nw, CHUNK)):
            for i, h in enumerate(heads):
                o_ref[q, pl.ds(r0, CHUNK), h * DV_A:(h + 1) * DV_A] = on[i * CHUNK:(i + 1) * CHUNK]
                s_scr[q, h] = s_new[i]
        return carry

    lax.fori_loop(0, N_CHUNK, scan, 0)

    @pl.when(seg == n_seg - 1)
    def _():
        s_ref[:, 0] = s_scr[...]
        conv_ref[:, 0] = xbuf[:, SEG:SEG + SUBLANES, :]


def _gdn_prompt(proj, n_batch, seq, cw_all, alog_all, dtb_all, nw_all, layer):
    n_seg = seq // SEG
    seg8 = SEG // SUBLANES
    seq8 = seq // SUBLANES
    n_pairs = H_A // 2
    nb = n_batch // N_SEQ
    proj3 = proj.reshape(N_SEQ, nb * seq, PROJ_PAD)
    blk = lambda width, col: pl.BlockSpec((N_SEQ, SEG, width), lambda bb, s: (0, bb * n_seg + s, col))
    vec = lambda: pl.BlockSpec((None, 1, LANES), lambda bb, s: (layer, 0, 0))
    o, s_out, conv = pl.pallas_call(
        functools.partial(_gdn_prompt_kernel, n_seg),
        name="gdn_prompt",
        grid=(nb, n_seg),
        in_specs=[pl.BlockSpec((N_SEQ, SUBLANES, CONV_DIM),
                               lambda bb, s: (0, jnp.maximum(bb * seq8 + s * seg8 - 1, 0), 0)),
                  blk(CONV_DIM, 0), blk(A_WIDTH, COL_Z // A_WIDTH), blk(LANES, COL_BA // LANES),
                  pl.BlockSpec((None, CONV_WIDTH, CONV_DIM), lambda bb, s: (layer, 0, 0)),
                  vec(), vec(), vec()],
        out_specs=[blk(A_WIDTH, 0),
                   pl.BlockSpec((N_SEQ, 1, H_A, DK_A, DV_A), lambda bb, s: (0, bb, 0, 0, 0)),
                   pl.BlockSpec((N_SEQ, 1, SUBLANES, CONV_DIM), lambda bb, s: (0, bb, 0, 0))],
        out_shape=[jax.ShapeDtypeStruct((N_SEQ, nb * seq, A_WIDTH), F32),
                   jax.ShapeDtypeStruct((N_SEQ, nb, H_A, DK_A, DV_A), F32),
                   jax.ShapeDtypeStruct((N_SEQ, nb, SUBLANES, CONV_DIM), F32)],
        scratch_shapes=[pltpu.VMEM((N_SEQ, SEG + SUBLANES, CONV_DIM), F32),
                        pltpu.VMEM((N_SEQ, H_A, DK_A, DV_A), F32),
                        pltpu.VMEM((N_CHUNK, N_SEQ, n_pairs, LANES, DV_A), F32),
                        pltpu.VMEM((N_CHUNK, N_SEQ, n_pairs, 2 * LANES, DK_A), BF16),
                        pltpu.VMEM((N_CHUNK, N_SEQ, n_pairs, 3 * LANES, LANES), BF16),
                        pltpu.VMEM((N_CHUNK, N_SEQ, n_pairs, SUBLANES, LANES), F32)],
        compiler_params=_params(2),
    )(proj3, proj3, proj3, proj3, cw_all, alog_all, dtb_all, nw_all)
    return (o.reshape(n_batch * seq, A_WIDTH), s_out.reshape(n_batch, H_A, DK_A, DV_A),
            conv.reshape(n_batch * SUBLANES, CONV_DIM))


def _token_rows(n_rows):
    row = lax.broadcasted_iota(jnp.int32, (n_rows, 1), 0) & (S_TILE - 1)
    return (row >= ROW0) & (row < ROW0 + DEC_SEQ)


def _gdn_sample_kernel(qkv_ref, hist_ref, z_ref, ba_ref, cw_ref, alog_ref, dtb_ref, nw_ref, s_in_ref,
                       o_ref, s_out_ref, conv_ref):
    rows = SB * S_TILE
    valid = _token_rows(rows)
    x = jnp.where(valid, qkv_ref[...], 0.0) + hist_ref[...]
    conv_ref[...] = pltpu.roll(x, rows - DEC_SEQ, axis=0)
    y = _conv_silu(x, cw_ref[...])
    beta_all, g_all = _gates(ba_ref[...], alog_ref[...], dtb_ref[...])
    beta_all = jnp.where(valid, beta_all, 0.0)
    gc_all = _cumsum_rows(jnp.where(valid, g_all, 0.0), S_TILE)
    z = z_ref[...]
    ym = jnp.where(valid, y, 0.0)

    all_units = [(b, h) for b in range(SB) for h in range(H_A)]
    per_group = LANES // S_TILE
    unit_groups = [all_units[i:i + per_group] for i in range(0, len(all_units), per_group)]

    def stack(units, src, base, width=LANES):
        return jnp.concatenate(
            [src[b * S_TILE:(b + 1) * S_TILE, base + h * width:base + (h + 1) * width] for b, h in units], axis=0)

    mk = _group_masks(S_TILE)
    prepared = _gdn_prepare(
        [(stack(us, y, 0), stack(us, ym, H_A * DK_A), stack(us, ym, 2 * H_A * DK_A),
          stack(us, beta_all, 0, 1), stack(us, gc_all, H_A, 1)) for us in unit_groups], S_TILE, mk)
    items = [(u, wq, lhs, egl, [s_in_ref[b, h] for b, h in us], stack(us, z, 0))
             for us, (u, wq, lhs, egl) in zip(unit_groups, prepared)]
    for us, (on, s_new) in zip(unit_groups, _gdn_apply(items, nw_ref[...], S_TILE)):
        for i, (b, h) in enumerate(us):
            o_ref[b * S_TILE:(b + 1) * S_TILE, h * DV_A:(h + 1) * DV_A] = on[i * S_TILE:(i + 1) * S_TILE]
            s_out_ref[b, h] = s_new[i]


def _gdn_sample(proj, hist, state_all, cw_all, alog_all, dtb_all, nw_all, layer):
    rows = SB * S_TILE
    n_seq = proj.shape[0] // S_TILE
    vec = lambda: pl.BlockSpec((None, 1, LANES), lambda i: (layer, 0, 0))
    return pl.pallas_call(
        _gdn_sample_kernel,
        name="gdn_sample",
        grid=(n_seq // SB,),
        in_specs=[pl.BlockSpec((rows, CONV_DIM), lambda i: (i, 0)),
                  pl.BlockSpec((None, rows, CONV_DIM), lambda i: (layer, i, 0)),
                  pl.BlockSpec((rows, A_WIDTH), lambda i: (i, COL_Z // A_WIDTH)),
                  pl.BlockSpec((rows, LANES), lambda i: (i, COL_BA // LANES)),
                  pl.BlockSpec((None, CONV_WIDTH, CONV_DIM), lambda i: (layer, 0, 0)),
                  vec(), vec(), vec(),
                  pl.BlockSpec((None, SB, H_A, DK_A, DV_A), lambda i: (layer, i, 0, 0, 0))],
        out_specs=[pl.BlockSpec((rows, A_WIDTH), lambda i: (i, 0)),
                   pl.BlockSpec((SB, H_A, DK_A, DV_A), lambda i: (i, 0, 0, 0)),
                   pl.BlockSpec((rows, CONV_DIM), lambda i: (i, 0))],
        out_shape=[jax.ShapeDtypeStruct((n_seq * S_TILE, A_WIDTH), F32),
                   jax.ShapeDtypeStruct((n_seq, H_A, DK_A, DV_A), F32),
                   jax.ShapeDtypeStruct((n_seq * S_TILE, CONV_DIM), F32)],
        compiler_params=_params(1),
    )(proj, hist, proj, proj, cw_all, alog_all, dtb_all, nw_all, state_all)


def _swa_heads(items, mask, sink_ref, q_rows):
    lane_head = lax.broadcasted_iota(jnp.int32, (1, LANES), 1) >> int(math.log2(HD_B))
    chains = [(it, hk) for it in range(len(items)) for hk in range(H_KVB)]
    each = lambda f, *lists: [f(*a) for a in zip(*lists)]
    qcat = [jnp.concatenate([q[:, gi * LANES:(gi + 1) * LANES] for gi in range(GQA_GROUP)], axis=0)
            * (HD_B ** -0.5) for q, _, _ in items]
    sk_head = [jnp.concatenate(
        [jnp.broadcast_to(sink_ref[hk * GQA_GROUP + gi:hk * GQA_GROUP + gi + 1, 0:1], (q_rows, 1))
         for gi in range(GQA_GROUP)], axis=0) for hk in range(H_KVB)]
    sk = [sk_head[hk] for _, hk in chains]
    kh = [jnp.where(lane_head == hk, items[it][1], 0.0) for it, hk in chains]
    vh = [jnp.where(lane_head == hk, items[it][2], 0.0) for it, hk in chains]
    s = [jnp.where(mask, _dot_nt(qcat[it], kh[ch]), NEG) for ch, (it, _) in enumerate(chains)]
    m = each(lambda a, b: jnp.maximum(jnp.max(a, axis=-1, keepdims=True), b), s, sk)
    p = each(lambda a, b: jnp.exp(a - b), s, m)
    den = each(lambda a, b, d: jnp.sum(a, axis=-1, keepdims=True) + jnp.exp(b - d), p, sk, m)
    o = each(lambda a, b, d: _dot(a, b) * (1.0 / d), p, vh, den)
    outs = []
    for it in range(len(items)):
        tot = o[H_KVB * it]
        for hk in range(1, H_KVB):
            tot = tot + o[H_KVB * it + hk]
        outs.append(jnp.concatenate([tot[gi * q_rows:(gi + 1) * q_rows] for gi in range(GQA_GROUP)], axis=1))
    return outs


def _swa_prompt_kernel(q_ref, kc_ref, kp_ref, vc_ref, vp_ref, sink_ref, o_ref):
    n0 = pl.program_id(1) * SWA_QB
    keys = 2 * WINDOW
    k_ext = jnp.concatenate([kp_ref[...], kc_ref[...]], axis=0)
    v_ext = jnp.concatenate([vp_ref[...], vc_ref[...]], axis=0)
    c = lax.broadcasted_iota(jnp.int32, (keys, 2 * WINDOW), 0)
    r = lax.broadcasted_iota(jnp.int32, (keys, 2 * WINDOW), 1) & (WINDOW - 1)
    band = (c > r) & (c <= r + WINDOW)
    shift = int(math.log2(HD_B))
    lane_head = lax.broadcasted_iota(jnp.int32, (1, LANES), 1) >> shift
    row = lax.broadcasted_iota(jnp.int32, (LANES, 1), 0)
    row_head = row >> shift
    each = lambda f, *lists: [f(*a) for a in zip(*lists)]
    n_gp = GQA_GROUP // 2

    chains = [(j, hk, gp) for j in range(SWA_QB) for hk in range(H_KVB) for gp in range(n_gp)]
    mask = [band & (c + (n0 + j) * WINDOW >= WINDOW) for j in range(SWA_QB)]
    kh, vth, qpair = {}, {}, {}
    for j in range(SWA_QB):
        k_all = k_ext[j * WINDOW:(j + 2) * WINDOW]
        vt = v_ext[j * WINDOW:(j + 2) * WINDOW].T
        qs = q_ref[j * WINDOW:(j + 1) * WINDOW, :] * (HD_B ** -0.5)
        for hk in range(H_KVB):
            kh[j, hk] = jnp.where(lane_head == hk, k_all, 0.0).astype(BF16)
            vth[j, hk] = jnp.where(row_head == hk, vt, jnp.where(row == (1 - hk) * HD_B, 1.0, 0.0)).astype(BF16)
        for gp in range(n_gp):
            qpair[j, gp] = jnp.concatenate([qs[:, (2 * gp) * LANES:(2 * gp + 1) * LANES],
                                            qs[:, (2 * gp + 1) * LANES:(2 * gp + 2) * LANES]],
                                           axis=0).astype(BF16)
    ones_row = [(1 - hk) * HD_B for _, hk, _ in chains]
    sk = [jnp.concatenate([sink_ref[hk * GQA_GROUP + 2 * gp + i:hk * GQA_GROUP + 2 * gp + i + 1, :]
                           for i in range(2)], axis=1) for _, hk, gp in chains]
    st = [jnp.where(mask[j], _dot_nt(kh[j, hk], qpair[j, gp]), NEG) for j, hk, gp in chains]
    m = each(lambda a, b: jnp.maximum(jnp.max(a, axis=0, keepdims=True), b), st, sk)
    pt = each(lambda a, b: jnp.exp(a - b).astype(BF16), st, m)
    ot = [jnp.dot(vth[j, hk], pt[ch], preferred_element_type=F32) for ch, (j, hk, _) in enumerate(chains)]
    den = each(lambda a, i, b, d: a[i:i + 1, :] + jnp.exp(b - d), ot, ones_row, sk, m)
    ot = [jnp.where(row_head == hk, ot[ch] * (1.0 / den[ch]), 0.0) for ch, (_, hk, _) in enumerate(chains)]
    for j in range(SWA_QB):
        tiles = []
        for gp in range(n_gp):
            tot = ot[chains.index((j, 0, gp))]
            for hk in range(1, H_KVB):
                tot = tot + ot[chains.index((j, hk, gp))]
            tiles += [tot[:, :WINDOW].T, tot[:, WINDOW:].T]
        o_ref[j * WINDOW:(j + 1) * WINDOW, :] = jnp.concatenate(tiles, axis=1)


def _swa_prompt(proj, n_batch, seq, sink_all, layer):
    nb = seq // WINDOW
    ns = nb // SWA_QB
    qrows = SWA_QB * WINDOW
    cur = lambda col: (lambda b, n: (b * ns + n, col))
    prev = lambda col: (lambda b, n: (b * nb + jnp.maximum(n * SWA_QB - 1, 0), col))
    return pl.pallas_call(
        _swa_prompt_kernel,
        name="swa_prompt",
        grid=(n_batch, ns),
        in_specs=[pl.BlockSpec((qrows, B_WIDTH), cur(COL_QB // B_WIDTH)),
                  pl.BlockSpec((qrows, KV_WIDTH), cur(COL_KB // KV_WIDTH)),
                  pl.BlockSpec((WINDOW, KV_WIDTH), prev(COL_KB // KV_WIDTH)),
                  pl.BlockSpec((qrows, KV_WIDTH), cur(COL_VB // KV_WIDTH)),
                  pl.BlockSpec((WINDOW, KV_WIDTH), prev(COL_VB // KV_WIDTH)),
                  pl.BlockSpec((None, H_QB, LANES), lambda b, n: (layer, 0, 0))],
        out_specs=pl.BlockSpec((qrows, B_WIDTH), lambda b, n: (b * ns + n, 0)),
        out_shape=jax.ShapeDtypeStruct((n_batch * seq, B_WIDTH), F32),
        compiler_params=_params(2),
    )(proj, proj, proj, proj, proj, sink_all)


def _shift_cache(cache, new_tile):
    rolled = pltpu.roll(cache, WINDOW - DEC_SEQ, axis=0)
    moved = pltpu.roll(new_tile, S_TILE - DEC_SEQ - ROW0, axis=0)
    row = lax.broadcasted_iota(jnp.int32, (S_TILE, 1), 0)
    tail = jnp.where(row >= S_TILE - DEC_SEQ, moved, rolled[WINDOW - S_TILE:])
    return jnp.concatenate([rolled[:WINDOW - S_TILE], tail], axis=0)


def _swa_sample_kernel(q_ref, k_ref, v_ref, kc_ref, vc_ref, sink_ref, o_ref, ko_ref, vo_ref):
    shape = (GQA_GROUP * S_TILE, 2 * WINDOW)
    r = (lax.broadcasted_iota(jnp.int32, shape, 0) & (S_TILE - 1)) - ROW0
    c = lax.broadcasted_iota(jnp.int32, shape, 1)
    j = c - WINDOW - ROW0
    mask = ((c < WINDOW) & (c > r)) | ((j >= 0) & (j < DEC_SEQ) & (j <= r))
    pad = jnp.zeros((WINDOW - S_TILE, KV_WIDTH), F32)
    items = []
    for b in range(SB):
        rows = slice(b * S_TILE, (b + 1) * S_TILE)
        k_new = k_ref[rows, :]
        v_new = v_ref[rows, :]
        items.append((q_ref[rows, :], jnp.concatenate([kc_ref[b], k_new, pad], axis=0),
                      jnp.concatenate([vc_ref[b], v_new, pad], axis=0)))
        ko_ref[b] = _shift_cache(kc_ref[b], k_new)
        vo_ref[b] = _shift_cache(vc_ref[b], v_new)
    for b, o in enumerate(_swa_heads(items, mask, sink_ref, S_TILE)):
        o_ref[b * S_TILE:(b + 1) * S_TILE, :] = o


def _swa_sample(proj, kc_all, vc_all, sink_all, layer):
    rows = SB * S_TILE
    n_seq = proj.shape[0] // S_TILE
    cache_in = lambda: pl.BlockSpec((None, SB, WINDOW, KV_WIDTH), lambda i: (layer, i, 0, 0))
    cache_out = lambda: pl.BlockSpec((SB, WINDOW, KV_WIDTH), lambda i: (i, 0, 0))
    return pl.pallas_call(
        _swa_sample_kernel,
        name="swa_sample",
        grid=(n_seq // SB,),
        in_specs=[pl.BlockSpec((rows, B_WIDTH), lambda i: (i, COL_QB // B_WIDTH)),
                  pl.BlockSpec((rows, KV_WIDTH), lambda i: (i, COL_KB // KV_WIDTH)),
                  pl.BlockSpec((rows, KV_WIDTH), lambda i: (i, COL_VB // KV_WIDTH)),
                  cache_in(), cache_in(),
                  pl.BlockSpec((None, H_QB, LANES), lambda i: (layer, 0, 0))],
        out_specs=[pl.BlockSpec((rows, B_WIDTH), lambda i: (i, 0)), cache_out(), cache_out()],
        out_shape=[jax.ShapeDtypeStruct((n_seq * S_TILE, B_WIDTH), F32),
                   jax.ShapeDtypeStruct((n_seq, WINDOW, KV_WIDTH), F32),
                   jax.ShapeDtypeStruct((n_seq, WINDOW, KV_WIDTH), F32)],
        compiler_params=_params(1),
    )(proj, proj, proj, kc_all, vc_all, sink_all)


def kernel(x_prompt, x_sample, state_delta, state_conv, cache_swa_k, cache_swa_v, w_in, conv_w, a_log,
           dt_bias, norm_a_w, sinks, w_out, ln1_g, ln1_b, w_ffn_in, w_ffn_out, ln2_g, ln2_b):
    depth = w_in.shape[0]
    n_batch, seq, _ = x_prompt.shape
    n_dec, dec_seq, _ = x_sample.shape
    assert dec_seq == DEC_SEQ and seq % SEG == 0 and n_dec % SB == 0 and n_batch % N_SEQ == 0
    assert cache_swa_k.shape[2] == WINDOW
    alpha = (2 * depth) ** 0.25

    c0 = CONV_DIM + A_WIDTH
    q0 = c0 + 2 * H_A
    order = jnp.array(QB_HEAD_ORDER)
    w_q = w_in[:, :, q0:q0 + B_WIDTH].reshape(depth, D_MODEL, H_QB, HD_B)[:, :, order]
    w_in_r = jnp.concatenate(
        [w_in[:, :, :c0], w_q.reshape(depth, D_MODEL, B_WIDTH), w_in[:, :, q0 + B_WIDTH:],
         w_in[:, :, c0:q0], jnp.zeros((depth, D_MODEL, LANES - 2 * H_A), w_in.dtype)], axis=-1).astype(BF16)
    w_out_bq = w_out[:, A_WIDTH:].reshape(depth, H_QB, HD_B, D_MODEL)[:, order].reshape(depth, B_WIDTH, D_MODEL)
    w_out_b = jnp.concatenate([w_out[:, :A_WIDTH], w_out_bq], axis=1).astype(BF16)
    w_ffn_in_b = w_ffn_in.astype(BF16)
    w_ffn_out_b = w_ffn_out.astype(BF16)
    lane_pad = lambda t: jnp.pad(t, ((0, 0), (H_A, LANES - 2 * H_A)))[:, None, :]
    alog_v = lane_pad(a_log)
    dtb_v = lane_pad(dt_bias)
    nw_v = norm_a_w[:, None, :]
    sink_v = jnp.broadcast_to(sinks[:, :, None], (depth, H_QB, LANES))
    ln1_g3, ln1_b3, ln2_g3, ln2_b3 = (t[:, None, :] for t in (ln1_g, ln1_b, ln2_g, ln2_b))

    hist = jnp.pad(state_conv, ((0, 0), (0, 0), (0, S_TILE - (CONV_WIDTH - 1)), (0, 0)))
    hist = hist.reshape(depth, n_dec * S_TILE, CONV_DIM)
    kc = cache_swa_k.reshape(depth, n_dec, WINDOW, KV_WIDTH)
    vc = cache_swa_v.reshape(depth, n_dec, WINDOW, KV_WIDTH)

    xp = x_prompt.reshape(n_batch * seq, D_MODEL)
    xs = jnp.pad(x_sample, ((0, 0), (ROW0, S_TILE - ROW0 - DEC_SEQ), (0, 0))).reshape(n_dec * S_TILE, D_MODEL)

    outs = [[] for _ in range(8)]
    for l in range(depth):
        proj = _in_proj(xp, w_in_r, l)
        oa, s_p, conv_p = _gdn_prompt(proj, n_batch, seq, conv_w, alog_v, dtb_v, nw_v, l)
        ob = _swa_prompt(proj, n_batch, seq, sink_v, l)
        xp = _mix_ffn(xp, oa, ob, w_out_b, ln1_g3, ln1_b3, w_ffn_in_b, w_ffn_out_b, ln2_g3, ln2_b3, l, alpha)
        proj3 = proj.reshape(n_batch, seq, PROJ_PAD)
        outs[0].append(s_p)
        outs[1].append(conv_p.reshape(n_batch, SUBLANES, CONV_DIM)[:, SUBLANES - (CONV_WIDTH - 1):])
        outs[2].append(proj3[:, seq - WINDOW:, COL_KB:COL_KB + KV_WIDTH].reshape(n_batch, WINDOW, H_KVB, HD_B))
        outs[3].append(proj3[:, seq - WINDOW:, COL_VB:COL_VB + KV_WIDTH].reshape(n_batch, WINDOW, H_KVB, HD_B))
        proj = _in_proj(xs, w_in_r, l)
        oa, s_s, conv_s = _gdn_sample(proj, hist, state_delta, conv_w, alog_v, dtb_v, nw_v, l)
        ob, k_s, v_s = _swa_sample(proj, kc, vc, sink_v, l)
        xs = _mix_ffn(xs, oa, ob, w_out_b, ln1_g3, ln1_b3, w_ffn_in_b, w_ffn_out_b, ln2_g3, ln2_b3, l, alpha)
        outs[4].append(s_s)
        outs[5].append(conv_s.reshape(n_dec, S_TILE, CONV_DIM)[:, :CONV_WIDTH - 1])
        outs[6].append(k_s.reshape(n_dec, WINDOW, H_KVB, HD_B))
        outs[7].append(v_s.reshape(n_dec, WINDOW, H_KVB, HD_B))

    y_prompt = xp.reshape(n_batch, seq, D_MODEL)
    y_sample = xs.reshape(n_dec, S_TILE, D_MODEL)[:, ROW0:ROW0 + DEC_SEQ]
    return (y_prompt, y_sample) + tuple(jnp.stack(o) for o in outs)
```

```python
import functools
import math
from typing import NamedTuple

import jax
import jax.numpy as jnp
from jax import lax
from jax.experimental import pallas as pl
from jax.experimental.pallas import tpu as pltpu

F32 = jnp.float32
BF16 = jnp.bfloat16

D_MODEL = 1024
H_A = 4
DK_A = 128
DV_A = 128
CONV_WIDTH = 4
CONV_DIM = 2 * H_A * DK_A + H_A * DV_A
A_WIDTH = H_A * DV_A
CHUNK = 64
HD_B = 64
H_QB = 8
H_KVB = 2
GQA_GROUP = H_QB // H_KVB
B_WIDTH = H_QB * HD_B
KV_WIDTH = H_KVB * HD_B
WINDOW = 128
D_FF = 2816
EPS = 1e-6

LANES = 128
SUBLANES = 8

COL_Z = CONV_DIM
COL_QB = COL_Z + A_WIDTH
COL_KB = COL_QB + B_WIDTH
COL_VB = COL_KB + KV_WIDTH
COL_BA = COL_VB + KV_WIDTH
PROJ_PAD = COL_BA + LANES
QB_HEAD_ORDER = tuple(hk * GQA_GROUP + g for g in range(GQA_GROUP) for hk in range(H_KVB))

S_TILE = SUBLANES
ROW0 = CONV_WIDTH - 1
DEC_SEQ = 4
SB = 16

TM = 512
SEG = 256
N_CHUNK = SEG // CHUNK
PREP_CHUNKS = 1
N_SEQ = 4
FF_CHUNK = 256
SWA_QB = 4
NEG = -1e30
VMEM_LIMIT = 56 * 1024 * 1024


def _params(n_axes, vmem=None):
    return pltpu.CompilerParams(
        dimension_semantics=("arbitrary",) * n_axes,
        vmem_limit_bytes=vmem if vmem is not None else VMEM_LIMIT)


def _dot(a, b):
    return jnp.dot(a.astype(BF16), b.astype(BF16), preferred_element_type=F32)


def _dot_nt(a, b):
    return lax.dot_general(a.astype(BF16), b.astype(BF16), (((1,), (1,)), ((), ())),
                           preferred_element_type=F32)


def _sigmoid(x):
    return 1.0 / (1.0 + jnp.exp(-x))


def _softplus(x):
    return jnp.maximum(x, 0.0) + jnp.log(1.0 + jnp.exp(-jnp.abs(x)))


def _layernorm(h, g, b):
    mu = jnp.mean(h, axis=-1, keepdims=True)
    d = h - mu
    var = jnp.mean(d * d, axis=-1, keepdims=True)
    return d * lax.rsqrt(var + EPS) * g + b


def _in_proj_kernel(x_ref, w_ref, o_ref):
    xb = x_ref[...].astype(BF16)
    n = o_ref.shape[1]
    for n0 in range(0, n, 512):
        n1 = min(n0 + 512, n)
        o_ref[:, n0:n1] = jnp.dot(xb, w_ref[:, n0:n1], preferred_element_type=F32)


def _in_proj(x, w_all, layer):
    t = x.shape[0]
    return pl.pallas_call(
        _in_proj_kernel,
        name="in_proj",
        grid=(t // TM,),
        in_specs=[pl.BlockSpec((TM, D_MODEL), lambda i: (i, 0)),
                  pl.BlockSpec((None, D_MODEL, PROJ_PAD), lambda i: (layer, 0, 0))],
        out_specs=pl.BlockSpec((TM, PROJ_PAD), lambda i: (i, 0)),
        out_shape=jax.ShapeDtypeStruct((t, PROJ_PAD), F32),
        compiler_params=_params(1),
    )(x, w_all)


def _mix_ffn_kernel(alpha, x_ref, oa_ref, ob_ref, wm_ref, g1_ref, b1_ref, wi_ref, wo_ref, g2_ref, b2_ref, o_ref):
    m = (jnp.dot(oa_ref[...].astype(BF16), wm_ref[0:A_WIDTH, :], preferred_element_type=F32)
         + jnp.dot(ob_ref[...].astype(BF16), wm_ref[A_WIDTH:, :], preferred_element_type=F32))
    x1 = _layernorm(alpha * x_ref[...] + m, g1_ref[...], b1_ref[...])
    xb = x1.astype(BF16)
    acc = alpha * x1
    for c0 in range(0, D_FF, FF_CHUNK):
        gate = jnp.dot(xb, wi_ref[:, c0:c0 + FF_CHUNK], preferred_element_type=F32)
        up = jnp.dot(xb, wi_ref[:, D_FF + c0:D_FF + c0 + FF_CHUNK], preferred_element_type=F32)
        h = gate * _sigmoid(gate) * up
        acc = acc + jnp.dot(h.astype(BF16), wo_ref[c0:c0 + FF_CHUNK, :], preferred_element_type=F32)
    o_ref[...] = _layernorm(acc, g2_ref[...], b2_ref[...])


def _mix_ffn(x, oa, ob, wm_all, g1_all, b1_all, wi_all, wo_all, g2_all, b2_all, layer, alpha):
    t = x.shape[0]
    vec = lambda: pl.BlockSpec((None, 1, D_MODEL), lambda i: (layer, 0, 0))
    return pl.pallas_call(
        functools.partial(_mix_ffn_kernel, alpha),
        name="mix_ffn",
        grid=(t // TM,),
        in_specs=[pl.BlockSpec((TM, D_MODEL), lambda i: (i, 0)),
                  pl.BlockSpec((TM, A_WIDTH), lambda i: (i, 0)),
                  pl.BlockSpec((TM, B_WIDTH), lambda i: (i, 0)),
                  pl.BlockSpec((None, D_MODEL, D_MODEL), lambda i: (layer, 0, 0)),
                  vec(), vec(),
                  pl.BlockSpec((None, D_MODEL, 2 * D_FF), lambda i: (layer, 0, 0)),
                  pl.BlockSpec((None, D_FF, D_MODEL), lambda i: (layer, 0, 0)),
                  vec(), vec()],
        out_specs=pl.BlockSpec((TM, D_MODEL), lambda i: (i, 0)),
        out_shape=jax.ShapeDtypeStruct((t, D_MODEL), F32),
        compiler_params=_params(1),
    )(x, oa, ob, wm_all, g1_all, b1_all, wi_all, wo_all, g2_all, b2_all)


def _conv_silu(x, cw):
    y = cw[3:4, :] * x
    for j in range(1, CONV_WIDTH):
        y = y + cw[CONV_WIDTH - 1 - j:CONV_WIDTH - j, :] * pltpu.roll(x, j, axis=0)
    return y * _sigmoid(y)


def _gates(ba, alog, dtb):
    beta = _sigmoid(ba)
    g = -jnp.exp(alog) * _softplus(ba + dtb)
    return beta, g


def _cumsum_rows(g, c):
    rin = lax.broadcasted_iota(jnp.int32, g.shape, 0) & (c - 1)
    s = 1
    while s < c:
        g = g + jnp.where(rin >= s, pltpu.roll(g, s, axis=0), 0.0)
        s *= 2
    return g


class _Masks(NamedTuple):
    causal: jax.Array
    neg_strict: jax.Array
    eye: jax.Array
    unit: tuple


def _group_masks(c):
    r = LANES
    shift = int(math.log2(c))
    ri = lax.broadcasted_iota(jnp.int32, (r, r), 0)
    ci = lax.broadcasted_iota(jnp.int32, (r, r), 1)
    same = (ri >> shift) == (ci >> shift)
    return _Masks(
        causal=jnp.where(same & (ri >= ci), 1.0, 0.0),
        neg_strict=jnp.where(same & (ri > ci), -1.0, 0.0),
        eye=jnp.where(ri == ci, 1.0, 0.0),
        unit=tuple(jnp.where((ci >> shift) == i, 1.0, 0.0) for i in range(r // c)))


def _gdn_prepare(groups, c, mk):
    r = LANES
    n_units = r // c
    n_factors = int(math.log2(c))
    each = lambda f, *lists: [f(*a) for a in zip(*lists)]

    q, k, v, beta, gcol = (list(t) for t in zip(*groups))
    qn = each(lambda t: t * lax.rsqrt(jnp.sum(t * t, axis=-1, keepdims=True) + EPS) * (DK_A ** -0.5), q)
    kn = each(lambda t: t * lax.rsqrt(jnp.sum(t * t, axis=-1, keepdims=True) + EPS), k)
    gc = each(lambda t: jnp.broadcast_to(t, (r, r)), gcol)
    e = each(lambda t: jnp.exp(jnp.minimum(t - t.T, 0.0)), gc)
    kb = each(lambda a, b: a * b, kn, beta)
    kq = each(lambda a, b, d: _dot_nt(jnp.concatenate([a, b], axis=0), d), kb, qn, kn)
    bk = each(lambda a, b: a[:r] * (b * mk.neg_strict), kq, e)
    aqk = each(lambda a, b: a[r:] * (b * mk.causal), kq, e)

    p = each(lambda t: mk.eye + t, bk)
    bk = each(lambda t: _dot(t, t), bk)
    for _ in range(n_factors - 2):
        st = each(lambda a, b: _dot(jnp.concatenate([a, b], axis=0), b), p, bk)
        p = each(lambda a, b: a + b[:r], p, st)
        bk = each(lambda t: t[r:], st)
    p = each(lambda a, b: a + _dot(a, b), p, bk)

    eg = each(jnp.exp, gc)
    uw = each(lambda a, b, d, f, h: _dot(a, jnp.concatenate([b * d, f * h], axis=1)), p, v, beta, kb, eg)
    qd = each(lambda a, b: a * b, qn, eg)

    out = []
    for gi in range(len(groups)):
        g_last = [gc[gi][(i + 1) * c - 1:(i + 1) * c, :] for i in range(n_units)]
        gl = jnp.concatenate([jnp.broadcast_to(t, (c, r)) for t in g_last], axis=0)
        kdt = (kn[gi] * jnp.exp(gl - gc[gi])).T
        u = uw[gi][:, :DV_A]
        w = uw[gi][:, DV_A:]
        wq = jnp.concatenate([jnp.concatenate([w[i * c:(i + 1) * c], qd[gi][i * c:(i + 1) * c]], axis=0)
                              for i in range(n_units)], axis=0).astype(BF16)
        lhs = jnp.concatenate([aqk[gi]] + [kdt * mk.unit[i] for i in range(n_units)], axis=0).astype(BF16)
        out.append((u, wq, lhs, [jnp.exp(t) for t in g_last]))
    return out


def _gdn_apply(items, nw, c):
    r = LANES
    n_units = r // c
    res1 = [[jnp.dot(wq[2 * c * i:2 * c * (i + 1)], st[i].astype(BF16), preferred_element_type=F32)
             for i in range(n_units)] for _, wq, _, _, st, _ in items]
    vn = [jnp.concatenate([it[0][i * c:(i + 1) * c] - r1[i][:c] for i in range(n_units)], axis=0)
          for it, r1 in zip(items, res1)]
    res2 = [jnp.dot(it[2], v.astype(BF16), preferred_element_type=F32) for it, v in zip(items, vn)]
    out = []
    for (_, _, _, egl, st, z), r1, r2 in zip(items, res1, res2):
        o = jnp.concatenate([r1[i][c:] for i in range(n_units)], axis=0) + r2[:r]
        s_new = [st[i] * egl[i] + r2[r * (i + 1):r * (i + 2)] for i in range(n_units)]
        out.append((o * lax.rsqrt(jnp.mean(o * o, axis=-1, keepdims=True) + EPS) * nw * (z * _sigmoid(z)), s_new))
    return out


def _gdn_prompt_kernel(n_seg, prev_ref, qkv_ref, z_ref, ba_ref, cw_ref, alog_ref, dtb_ref, nw_ref,
                       o_ref, s_ref, conv_ref, xbuf, s_scr, u_scr, wq_scr, lhs_scr, egl_scr):
    seg = pl.program_id(1)
    n_pairs = H_A // 2

    @pl.when(seg == 0)
    def _():
        s_scr[...] = jnp.zeros_like(s_scr)

    xbuf[:, 0:SUBLANES, :] = jnp.where(seg > 0, prev_ref[...], 0.0)
    xbuf[:, SUBLANES:, :] = qkv_ref[...]
    cw = cw_ref[...]
    alog = alog_ref[...]
    dtb = dtb_ref[...]
    nw = nw_ref[...]
    mk = _group_masks(CHUNK)

    def prepare(it, carry):
        groups, where = [], []
        for j in range(PREP_CHUNKS):
            ci = it * PREP_CHUNKS + j
            r0 = pl.multiple_of(ci * CHUNK, CHUNK)
            for q in range(N_SEQ):
                y = _conv_silu(xbuf[q, pl.ds(r0, CHUNK + SUBLANES), :], cw)[SUBLANES:]
                beta_all, g_all = _gates(ba_ref[q, pl.ds(r0, CHUNK), :], alog, dtb)
                gc_all = _cumsum_rows(g_all, CHUNK)
                for pair in range(n_pairs):
                    heads = (2 * pair, 2 * pair + 1)
                    cols = lambda base: jnp.concatenate(
                        [y[:, base + h * LANES:base + (h + 1) * LANES] for h in heads], axis=0)
                    beta = jnp.concatenate([beta_all[:, h:h + 1] for h in heads], axis=0)
                    gcol = jnp.concatenate([gc_all[:, H_A + h:H_A + h + 1] for h in heads], axis=0)
                    groups.append((cols(0), cols(H_A * DK_A), cols(2 * H_A * DK_A), beta, gcol))
                    where.append((ci, q, pair))
        for (ci, q, pair), (u, wq, lhs, egl) in zip(where, _gdn_prepare(groups, CHUNK, mk)):
            u_scr[ci, q, pair] = u
            wq_scr[ci, q, pair] = wq
            lhs_scr[ci, q, pair] = lhs
            for i in range(2):
                egl_scr[ci, q, pair, i:i + 1, :] = egl[i]
        return carry

    lax.fori_loop(0, N_CHUNK // PREP_CHUNKS, prepare, 0)

    def scan(ci, carry):
        r0 = pl.multiple_of(ci * CHUNK, CHUNK)
        items, where = [], []
        for q in range(N_SEQ):
            z = z_ref[q, pl.ds(r0, CHUNK), :]
            for pair in range(n_pairs):
                heads = (2 * pair, 2 * pair + 1)
                zz = jnp.concatenate([z[:, h * DV_A:(h + 1) * DV_A] for h in heads], axis=0)
                egl = [egl_scr[ci, q, pair, i:i + 1, :] for i in range(2)]
                items.append((u_scr[ci, q, pair], wq_scr[ci, q, pair], lhs_scr[ci, q, pair], egl,
                              [s_scr[q, h] for h in heads], zz))
                where.append((q, heads))
        for (q, heads), (on, s_new) in zip(where, _gdn_apply(items, nw, CHUNK)):
            for i, h in enumerate(heads):
                o_ref[q, pl.ds(r0, CHUNK), h * DV_A:(h + 1) * DV_A] = on[i * CHUNK:(i + 1) * CHUNK]
                s_scr[q, h] = s_new[i]
        return carry

    lax.fori_loop(0, N_CHUNK, scan, 0)

    @pl.when(seg == n_seg - 1)
    def _():
        s_ref[:, 0] = s_scr[...]
        conv_ref[:, 0] = xbuf[:, SEG:SEG + SUBLANES, :]


def _gdn_prompt(proj, n_batch, seq, cw_all, alog_all, dtb_all, nw_all, layer):
    n_seg = seq // SEG
    seg8 = SEG // SUBLANES
    seq8 = seq // SUBLANES
    n_pairs = H_A // 2
    nb = n_batch // N_SEQ
    proj3 = proj.reshape(N_SEQ, nb * seq, PROJ_PAD)
    blk = lambda width, col: pl.BlockSpec((N_SEQ, SEG, width), lambda bb, s: (0, bb * n_seg + s, col))
    vec = lambda: pl.BlockSpec((None, 1, LANES), lambda bb, s: (layer, 0, 0))
    o, s_out, conv = pl.pallas_call(
        functools.partial(_gdn_prompt_kernel, n_seg),
        name="gdn_prompt",
        grid=(nb, n_seg),
        in_specs=[pl.BlockSpec((N_SEQ, SUBLANES, CONV_DIM),
                               lambda bb, s: (0, jnp.maximum(bb * seq8 + s * seg8 - 1, 0), 0)),
                  blk(CONV_DIM, 0), blk(A_WIDTH, COL_Z // A_WIDTH), blk(LANES, COL_BA // LANES),
                  pl.BlockSpec((None, CONV_WIDTH, CONV_DIM), lambda bb, s: (layer, 0, 0)),
                  vec(), vec(), vec()],
        out_specs=[blk(A_WIDTH, 0),
                   pl.BlockSpec((N_SEQ, 1, H_A, DK_A, DV_A), lambda bb, s: (0, bb, 0, 0, 0)),
                   pl.BlockSpec((N_SEQ, 1, SUBLANES, CONV_DIM), lambda bb, s: (0, bb, 0, 0))],
        out_shape=[jax.ShapeDtypeStruct((N_SEQ, nb * seq, A_WIDTH), F32),
                   jax.ShapeDtypeStruct((N_SEQ, nb, H_A, DK_A, DV_A), F32),
                   jax.ShapeDtypeStruct((N_SEQ, nb, SUBLANES, CONV_DIM), F32)],
        scratch_shapes=[pltpu.VMEM((N_SEQ, SEG + SUBLANES, CONV_DIM), F32),
                        pltpu.VMEM((N_SEQ, H_A, DK_A, DV_A), F32),
                        pltpu.VMEM((N_CHUNK, N_SEQ, n_pairs, LANES, DV_A), F32),
                        pltpu.VMEM((N_CHUNK, N_SEQ, n_pairs, 2 * LANES, DK_A), BF16),
                        pltpu.VMEM((N_CHUNK, N_SEQ, n_pairs, 3 * LANES, LANES), BF16),
                        pltpu.VMEM((N_CHUNK, N_SEQ, n_pairs, SUBLANES, LANES), F32)],
        compiler_params=_params(2),
    )(proj3, proj3, proj3, proj3, cw_all, alog_all, dtb_all, nw_all)
    return (o.reshape(n_batch * seq, A_WIDTH), s_out.reshape(n_batch, H_A, DK_A, DV_A),
            conv.reshape(n_batch * SUBLANES, CONV_DIM))


def _token_rows(n_rows):
    row = lax.broadcasted_iota(jnp.int32, (n_rows, 1), 0) & (S_TILE - 1)
    return (row >= ROW0) & (row < ROW0 + DEC_SEQ)


def _gdn_sample_kernel(qkv_ref, hist_ref, z_ref, ba_ref, cw_ref, alog_ref, dtb_ref, nw_ref, s_in_ref, *rest):
    o_ref, s_out_ref, conv_ref = rest[-3:]
    rows = SB * S_TILE
    valid = _token_rows(rows)
    x = jnp.where(valid, qkv_ref[...], 0.0) + hist_ref[...]
    conv_ref[...] = pltpu.roll(x, rows - DEC_SEQ, axis=0)
    y = _conv_silu(x, cw_ref[...])
    beta_all, g_all = _gates(ba_ref[...], alog_ref[...], dtb_ref[...])
    beta_all = jnp.where(valid, beta_all, 0.0)
    gc_all = _cumsum_rows(jnp.where(valid, g_all, 0.0), S_TILE)
    z = z_ref[...]
    ym = jnp.where(valid, y, 0.0)

    all_units = [(b, h) for b in range(SB) for h in range(H_A)]
    per_group = LANES // S_TILE
    unit_groups = [all_units[i:i + per_group] for i in range(0, len(all_units), per_group)]

    def stack(units, src, base, width=LANES):
        return jnp.concatenate(
            [src[b * S_TILE:(b + 1) * S_TILE, base + h * width:base + (h + 1) * width] for b, h in units], axis=0)

    mk = _group_masks(S_TILE)
    prepared = _gdn_prepare(
        [(stack(us, y, 0), stack(us, ym, H_A * DK_A), stack(us, ym, 2 * H_A * DK_A),
          stack(us, beta_all, 0, 1), stack(us, gc_all, H_A, 1)) for us in unit_groups], S_TILE, mk)
    items = [(u, wq, lhs, egl, [s_in_ref[b, h] for b, h in us], stack(us, z, 0))
             for us, (u, wq, lhs, egl) in zip(unit_groups, prepared)]
    for us, (on, s_new) in zip(unit_groups, _gdn_apply(items, nw_ref[...], S_TILE)):
        for i, (b, h) in enumerate(us):
            o_ref[b * S_TILE:(b + 1) * S_TILE, h * DV_A:(h + 1) * DV_A] = on[i * S_TILE:(i + 1) * S_TILE]
            s_out_ref[b, h] = s_new[i]


def _gdn_sample(proj, hist, state_all, cw_all, alog_all, dtb_all, nw_all, layer, s_acc):
    rows = SB * S_TILE
    n_seq = proj.shape[0] // S_TILE
    depth = state_all.shape[0]
    vec = lambda: pl.BlockSpec((None, 1, LANES), lambda i: (layer, 0, 0))
    in_specs = [pl.BlockSpec((rows, CONV_DIM), lambda i: (i, 0)),
                pl.BlockSpec((None, rows, CONV_DIM), lambda i: (layer, i, 0)),
                pl.BlockSpec((rows, A_WIDTH), lambda i: (i, COL_Z // A_WIDTH)),
                pl.BlockSpec((rows, LANES), lambda i: (i, COL_BA // LANES)),
                pl.BlockSpec((None, CONV_WIDTH, CONV_DIM), lambda i: (layer, 0, 0)),
                vec(), vec(), vec(),
                pl.BlockSpec((None, SB, H_A, DK_A, DV_A), lambda i: (layer, i, 0, 0, 0))]
    in_specs.append(pl.BlockSpec(memory_space=pl.ANY))
    args = [proj, hist, proj, proj, cw_all, alog_all, dtb_all, nw_all, state_all, s_acc]
    aliases = {len(args) - 1: 1}
    return pl.pallas_call(
        _gdn_sample_kernel,
        name="gdn_sample",
        grid=(n_seq // SB,),
        in_specs=in_specs,
        out_specs=[pl.BlockSpec((rows, A_WIDTH), lambda i: (i, 0)),
                   pl.BlockSpec((None, SB, H_A, DK_A, DV_A), lambda i: (layer, i, 0, 0, 0)),
                   pl.BlockSpec((rows, CONV_DIM), lambda i: (i, 0))],
        out_shape=[jax.ShapeDtypeStruct((n_seq * S_TILE, A_WIDTH), F32),
                   jax.ShapeDtypeStruct((depth, n_seq, H_A, DK_A, DV_A), F32),
                   jax.ShapeDtypeStruct((n_seq * S_TILE, CONV_DIM), F32)],
        input_output_aliases=aliases,
        compiler_params=_params(1),
    )(*args)


def _swa_heads(items, mask, sink_ref, q_rows):
    lane_head = lax.broadcasted_iota(jnp.int32, (1, LANES), 1) >> int(math.log2(HD_B))
    chains = [(it, hk) for it in range(len(items)) for hk in range(H_KVB)]
    each = lambda f, *lists: [f(*a) for a in zip(*lists)]
    qcat = [jnp.concatenate([q[:, gi * LANES:(gi + 1) * LANES] for gi in range(GQA_GROUP)], axis=0)
            * (HD_B ** -0.5) for q, _, _ in items]
    sk_head = [jnp.concatenate(
        [jnp.broadcast_to(sink_ref[hk * GQA_GROUP + gi:hk * GQA_GROUP + gi + 1, 0:1], (q_rows, 1))
         for gi in range(GQA_GROUP)], axis=0) for hk in range(H_KVB)]
    sk = [sk_head[hk] for _, hk in chains]
    kh = [jnp.where(lane_head == hk, items[it][1], 0.0) for it, hk in chains]
    vh = [jnp.where(lane_head == hk, items[it][2], 0.0) for it, hk in chains]
    s = [jnp.where(mask, _dot_nt(qcat[it], kh[ch]), NEG) for ch, (it, _) in enumerate(chains)]
    m = each(lambda a, b: jnp.maximum(jnp.max(a, axis=-1, keepdims=True), b), s, sk)
    p = each(lambda a, b: jnp.exp(a - b), s, m)
    den = each(lambda a, b, d: jnp.sum(a, axis=-1, keepdims=True) + jnp.exp(b - d), p, sk, m)
    o = each(lambda a, b, d: _dot(a, b) * (1.0 / d), p, vh, den)
    outs = []
    for it in range(len(items)):
        tot = o[H_KVB * it]
        for hk in range(1, H_KVB):
            tot = tot + o[H_KVB * it + hk]
        outs.append(jnp.concatenate([tot[gi * q_rows:(gi + 1) * q_rows] for gi in range(GQA_GROUP)], axis=1))
    return outs


def _swa_prompt_kernel(q_ref, kc_ref, kp_ref, vc_ref, vp_ref, sink_ref, o_ref):
    n0 = pl.program_id(1) * SWA_QB
    keys = 2 * WINDOW
    k_ext = jnp.concatenate([kp_ref[...], kc_ref[...]], axis=0)
    v_ext = jnp.concatenate([vp_ref[...], vc_ref[...]], axis=0)
    c = lax.broadcasted_iota(jnp.int32, (keys, 2 * WINDOW), 0)
    r = lax.broadcasted_iota(jnp.int32, (keys, 2 * WINDOW), 1) & (WINDOW - 1)
    band = (c > r) & (c <= r + WINDOW)
    shift = int(math.log2(HD_B))
    lane_head = lax.broadcasted_iota(jnp.int32, (1, LANES), 1) >> shift
    row = lax.broadcasted_iota(jnp.int32, (LANES, 1), 0)
    row_head = row >> shift
    each = lambda f, *lists: [f(*a) for a in zip(*lists)]
    n_gp = GQA_GROUP // 2

    chains = [(j, hk, gp) for j in range(SWA_QB) for hk in range(H_KVB) for gp in range(n_gp)]
    mask = [band & (c + (n0 + j) * WINDOW >= WINDOW) for j in range(SWA_QB)]
    kh, vth, qpair = {}, {}, {}
    for j in range(SWA_QB):
        k_all = k_ext[j * WINDOW:(j + 2) * WINDOW]
        vt = v_ext[j * WINDOW:(j + 2) * WINDOW].T
        qs = q_ref[j * WINDOW:(j + 1) * WINDOW, :] * (HD_B ** -0.5)
        for hk in range(H_KVB):
            kh[j, hk] = jnp.where(lane_head == hk, k_all, 0.0).astype(BF16)
            vth[j, hk] = jnp.where(row_head == hk, vt, jnp.where(row == (1 - hk) * HD_B, 1.0, 0.0)).astype(BF16)
        for gp in range(n_gp):
            qpair[j, gp] = jnp.concatenate([qs[:, (2 * gp) * LANES:(2 * gp + 1) * LANES],
                                            qs[:, (2 * gp + 1) * LANES:(2 * gp + 2) * LANES]],
                                           axis=0).astype(BF16)
    ones_row = [(1 - hk) * HD_B for _, hk, _ in chains]
    sk = [jnp.concatenate([sink_ref[hk * GQA_GROUP + 2 * gp + i:hk * GQA_GROUP + 2 * gp + i + 1, :]
                           for i in range(2)], axis=1) for _, hk, gp in chains]
    st = [jnp.where(mask[j], _dot_nt(kh[j, hk], qpair[j, gp]), NEG) for j, hk, gp in chains]
    m = each(lambda a, b: jnp.maximum(jnp.max(a, axis=0, keepdims=True), b), st, sk)
    pt = each(lambda a, b: jnp.exp(a - b).astype(BF16), st, m)
    ot = [jnp.dot(vth[j, hk], pt[ch], preferred_element_type=F32) for ch, (j, hk, _) in enumerate(chains)]
    den = each(lambda a, i, b, d: a[i:i + 1, :] + jnp.exp(b - d), ot, ones_row, sk, m)
    ot = [jnp.where(row_head == hk, ot[ch] * (1.0 / den[ch]), 0.0) for ch, (_, hk, _) in enumerate(chains)]
    for j in range(SWA_QB):
        tiles = []
        for gp in range(n_gp):
            tot = ot[chains.index((j, 0, gp))]
            for hk in range(1, H_KVB):
                tot = tot + ot[chains.index((j, hk, gp))]
            tiles += [tot[:, :WINDOW].T, tot[:, WINDOW:].T]
        o_ref[j * WINDOW:(j + 1) * WINDOW, :] = jnp.concatenate(tiles, axis=1)


def _swa_prompt(proj, n_batch, seq, sink_all, layer):
    nb = seq // WINDOW
    ns = nb // SWA_QB
    qrows = SWA_QB * WINDOW
    cur = lambda col: (lambda b, n: (b * ns + n, col))
    prev = lambda col: (lambda b, n: (b * nb + jnp.maximum(n * SWA_QB - 1, 0), col))
    return pl.pallas_call(
        _swa_prompt_kernel,
        name="swa_prompt",
        grid=(n_batch, ns),
        in_specs=[pl.BlockSpec((qrows, B_WIDTH), cur(COL_QB // B_WIDTH)),
                  pl.BlockSpec((qrows, KV_WIDTH), cur(COL_KB // KV_WIDTH)),
                  pl.BlockSpec((WINDOW, KV_WIDTH), prev(COL_KB // KV_WIDTH)),
                  pl.BlockSpec((qrows, KV_WIDTH), cur(COL_VB // KV_WIDTH)),
                  pl.BlockSpec((WINDOW, KV_WIDTH), prev(COL_VB // KV_WIDTH)),
                  pl.BlockSpec((None, H_QB, LANES), lambda b, n: (layer, 0, 0))],
        out_specs=pl.BlockSpec((qrows, B_WIDTH), lambda b, n: (b * ns + n, 0)),
        out_shape=jax.ShapeDtypeStruct((n_batch * seq, B_WIDTH), F32),
        compiler_params=_params(2),
    )(proj, proj, proj, proj, proj, sink_all)


def _shift_cache(cache, new_tile):
    rolled = pltpu.roll(cache, WINDOW - DEC_SEQ, axis=0)
    moved = pltpu.roll(new_tile, S_TILE - DEC_SEQ - ROW0, axis=0)
    row = lax.broadcasted_iota(jnp.int32, (S_TILE, 1), 0)
    tail = jnp.where(row >= S_TILE - DEC_SEQ, moved, rolled[WINDOW - S_TILE:])
    return jnp.concatenate([rolled[:WINDOW - S_TILE], tail], axis=0)


def _swa_sample_kernel(q_ref, k_ref, v_ref, kc_ref, vc_ref, sink_ref, *rest):
    o_ref, ko_ref, vo_ref = rest[-3:]
    shape = (GQA_GROUP * S_TILE, 2 * WINDOW)
    r = (lax.broadcasted_iota(jnp.int32, shape, 0) & (S_TILE - 1)) - ROW0
    c = lax.broadcasted_iota(jnp.int32, shape, 1)
    j = c - WINDOW - ROW0
    mask = ((c < WINDOW) & (c > r)) | ((j >= 0) & (j < DEC_SEQ) & (j <= r))
    pad = jnp.zeros((WINDOW - S_TILE, KV_WIDTH), F32)
    items = []
    for b in range(SB):
        rows = slice(b * S_TILE, (b + 1) * S_TILE)
        k_new = k_ref[rows, :]
        v_new = v_ref[rows, :]
        items.append((q_ref[rows, :], jnp.concatenate([kc_ref[b], k_new, pad], axis=0),
                      jnp.concatenate([vc_ref[b], v_new, pad], axis=0)))
        ko_ref[b] = _shift_cache(kc_ref[b], k_new)
        vo_ref[b] = _shift_cache(vc_ref[b], v_new)
    for b, o in enumerate(_swa_heads(items, mask, sink_ref, S_TILE)):
        o_ref[b * S_TILE:(b + 1) * S_TILE, :] = o


def _swa_sample(proj, kc_all, vc_all, sink_all, layer, k_acc, v_acc):
    rows = SB * S_TILE
    n_seq = proj.shape[0] // S_TILE
    depth = kc_all.shape[0]
    cache = lambda: pl.BlockSpec((None, SB, WINDOW, KV_WIDTH), lambda i: (layer, i, 0, 0))
    in_specs = [pl.BlockSpec((rows, B_WIDTH), lambda i: (i, COL_QB // B_WIDTH)),
                pl.BlockSpec((rows, KV_WIDTH), lambda i: (i, COL_KB // KV_WIDTH)),
                pl.BlockSpec((rows, KV_WIDTH), lambda i: (i, COL_VB // KV_WIDTH)),
                cache(), cache(),
                pl.BlockSpec((None, H_QB, LANES), lambda i: (layer, 0, 0))]
    in_specs += [pl.BlockSpec(memory_space=pl.ANY)] * 2
    args = [proj, proj, proj, kc_all, vc_all, sink_all, k_acc, v_acc]
    aliases = {len(args) - 2: 1, len(args) - 1: 2}
    return pl.pallas_call(
        _swa_sample_kernel,
        name="swa_sample",
        grid=(n_seq // SB,),
        in_specs=in_specs,
        out_specs=[pl.BlockSpec((rows, B_WIDTH), lambda i: (i, 0)), cache(), cache()],
        out_shape=[jax.ShapeDtypeStruct((n_seq * S_TILE, B_WIDTH), F32),
                   jax.ShapeDtypeStruct((depth, n_seq, WINDOW, KV_WIDTH), F32),
                   jax.ShapeDtypeStruct((depth, n_seq, WINDOW, KV_WIDTH), F32)],
        input_output_aliases=aliases,
        compiler_params=_params(1),
    )(*args)


def kernel(x_prompt, x_sample, state_delta, state_conv, cache_swa_k, cache_swa_v, w_in, conv_w, a_log,
           dt_bias, norm_a_w, sinks, w_out, ln1_g, ln1_b, w_ffn_in, w_ffn_out, ln2_g, ln2_b):
    depth = w_in.shape[0]
    n_batch, seq, _ = x_prompt.shape
    n_dec, dec_seq, _ = x_sample.shape
    assert dec_seq == DEC_SEQ and seq % SEG == 0 and n_dec % SB == 0 and n_batch % N_SEQ == 0
    assert cache_swa_k.shape[2] == WINDOW
    alpha = (2 * depth) ** 0.25

    c0 = CONV_DIM + A_WIDTH
    q0 = c0 + 2 * H_A
    order = jnp.array(QB_HEAD_ORDER)
    w_q = w_in[:, :, q0:q0 + B_WIDTH].reshape(depth, D_MODEL, H_QB, HD_B)[:, :, order]
    w_in_r = jnp.concatenate(
        [w_in[:, :, :c0], w_q.reshape(depth, D_MODEL, B_WIDTH), w_in[:, :, q0 + B_WIDTH:],
         w_in[:, :, c0:q0], jnp.zeros((depth, D_MODEL, LANES - 2 * H_A), w_in.dtype)], axis=-1).astype(BF16)
    w_out_bq = w_out[:, A_WIDTH:].reshape(depth, H_QB, HD_B, D_MODEL)[:, order].reshape(depth, B_WIDTH, D_MODEL)
    w_out_b = jnp.concatenate([w_out[:, :A_WIDTH], w_out_bq], axis=1).astype(BF16)
    w_ffn_in_b = w_ffn_in.astype(BF16)
    w_ffn_out_b = w_ffn_out.astype(BF16)
    lane_pad = lambda t: jnp.pad(t, ((0, 0), (H_A, LANES - 2 * H_A)))[:, None, :]
    alog_v = lane_pad(a_log)
    dtb_v = lane_pad(dt_bias)
    nw_v = norm_a_w[:, None, :]
    sink_v = jnp.broadcast_to(sinks[:, :, None], (depth, H_QB, LANES))
    ln1_g3, ln1_b3, ln2_g3, ln2_b3 = (t[:, None, :] for t in (ln1_g, ln1_b, ln2_g, ln2_b))

    hist = jnp.pad(state_conv, ((0, 0), (0, 0), (0, S_TILE - (CONV_WIDTH - 1)), (0, 0)))
    hist = hist.reshape(depth, n_dec * S_TILE, CONV_DIM)
    kc = cache_swa_k.reshape(depth, n_dec, WINDOW, KV_WIDTH)
    vc = cache_swa_v.reshape(depth, n_dec, WINDOW, KV_WIDTH)

    xp = x_prompt.reshape(n_batch * seq, D_MODEL)
    xs = jnp.pad(x_sample, ((0, 0), (ROW0, S_TILE - ROW0 - DEC_SEQ), (0, 0))).reshape(n_dec * S_TILE, D_MODEL)

    outs = [[] for _ in range(8)]
    s_acc = jnp.zeros((depth, n_dec, H_A, DK_A, DV_A), F32)
    k_acc = jnp.zeros((depth, n_dec, WINDOW, KV_WIDTH), F32)
    v_acc = jnp.zeros((depth, n_dec, WINDOW, KV_WIDTH), F32)
    for l in range(depth):
        proj = _in_proj(xp, w_in_r, l)
        oa, s_p, conv_p = _gdn_prompt(proj, n_batch, seq, conv_w, alog_v, dtb_v, nw_v, l)
        ob = _swa_prompt(proj, n_batch, seq, sink_v, l)
        xp = _mix_ffn(xp, oa, ob, w_out_b, ln1_g3, ln1_b3, w_ffn_in_b, w_ffn_out_b, ln2_g3, ln2_b3, l, alpha)
        proj3 = proj.reshape(n_batch, seq, PROJ_PAD)
        outs[0].append(s_p)
        outs[1].append(conv_p.reshape(n_batch, SUBLANES, CONV_DIM)[:, SUBLANES - (CONV_WIDTH - 1):])
        outs[2].append(proj3[:, seq - WINDOW:, COL_KB:COL_KB + KV_WIDTH].reshape(n_batch, WINDOW, H_KVB, HD_B))
        outs[3].append(proj3[:, seq - WINDOW:, COL_VB:COL_VB + KV_WIDTH].reshape(n_batch, WINDOW, H_KVB, HD_B))
        proj = _in_proj(xs, w_in_r, l)
        oa, s_acc, conv_s = _gdn_sample(proj, hist, state_delta, conv_w, alog_v, dtb_v, nw_v, l, s_acc)
        ob, k_acc, v_acc = _swa_sample(proj, kc, vc, sink_v, l, k_acc, v_acc)
        xs = _mix_ffn(xs, oa, ob, w_out_b, ln1_g3, ln1_b3, w_ffn_in_b, w_ffn_out_b, ln2_g3, ln2_b3, l, alpha)
        outs[5].append(conv_s.reshape(n_dec, S_TILE, CONV_DIM)[:, :CONV_WIDTH - 1])

    y_prompt = xp.reshape(n_batch, seq, D_MODEL)
    y_sample = xs.reshape(n_dec, S_TILE, D_MODEL)[:, ROW0:ROW0 + DEC_SEQ]
    stacked = [jnp.stack(o) if o else None for o in outs]
    stacked[4] = s_acc
    stacked[6] = k_acc.reshape(depth, n_dec, WINDOW, H_KVB, HD_B)
    stacked[7] = v_acc.reshape(depth, n_dec, WINDOW, H_KVB, HD_B)
    return (y_prompt, y_sample) + tuple(stacked)
```

```python
import functools
import math
from typing import NamedTuple

import jax
import jax.numpy as jnp
from jax import lax
from jax.experimental import pallas as pl
from jax.experimental.pallas import tpu as pltpu

F32 = jnp.float32
BF16 = jnp.bfloat16

D_MODEL = 1024
H_A = 4
DK_A = 128
DV_A = 128
CONV_WIDTH = 4
CONV_DIM = 2 * H_A * DK_A + H_A * DV_A
A_WIDTH = H_A * DV_A
CHUNK = 64
HD_B = 64
H_QB = 8
H_KVB = 2
GQA_GROUP = H_QB // H_KVB
B_WIDTH = H_QB * HD_B
KV_WIDTH = H_KVB * HD_B
WINDOW = 128
D_FF = 2816
EPS = 1e-6

LANES = 128
SUBLANES = 8

COL_Z = CONV_DIM
COL_QB = COL_Z + A_WIDTH
COL_KB = COL_QB + B_WIDTH
COL_VB = COL_KB + KV_WIDTH
COL_BA = COL_VB + KV_WIDTH
PROJ_PAD = COL_BA + LANES
QB_HEAD_ORDER = tuple(hk * GQA_GROUP + g for g in range(GQA_GROUP) for hk in range(H_KVB))

S_TILE = SUBLANES
ROW0 = CONV_WIDTH - 1
DEC_SEQ = 4
SB = 16

TM = 512
SEG = 256
N_CHUNK = SEG // CHUNK
PREP_CHUNKS = 2
N_SEQ = 4
FF_CHUNK = 256
SWA_QB = 4
NEG = -1e30
VMEM_LIMIT = 56 * 1024 * 1024


def _params(n_axes, vmem=None):
    return pltpu.CompilerParams(
        dimension_semantics=("arbitrary",) * n_axes,
        vmem_limit_bytes=vmem if vmem is not None else VMEM_LIMIT)


def _dot(a, b):
    return jnp.dot(a.astype(BF16), b.astype(BF16), preferred_element_type=F32)


def _dot_nt(a, b):
    return lax.dot_general(a.astype(BF16), b.astype(BF16), (((1,), (1,)), ((), ())),
                           preferred_element_type=F32)


def _sigmoid(x):
    return 1.0 / (1.0 + jnp.exp(-x))


def _silu(x):
    h = 0.5 * x
    return h + h * jnp.tanh(h)


def _softplus(x):
    return jnp.maximum(x, 0.0) + jnp.log(1.0 + jnp.exp(-jnp.abs(x)))


def _layernorm(h, g, b):
    mu = jnp.mean(h, axis=-1, keepdims=True)
    d = h - mu
    var = jnp.mean(d * d, axis=-1, keepdims=True)
    return d * lax.rsqrt(var + EPS) * g + b


def _in_proj_kernel(x_ref, w_ref, o_ref):
    xb = x_ref[...].astype(BF16)
    n = o_ref.shape[1]
    for n0 in range(0, n, 512):
        n1 = min(n0 + 512, n)
        o_ref[:, n0:n1] = jnp.dot(xb, w_ref[:, n0:n1], preferred_element_type=F32)


def _in_proj(x, w_all, layer):
    t = x.shape[0]
    return pl.pallas_call(
        _in_proj_kernel,
        name="in_proj",
        grid=(t // TM,),
        in_specs=[pl.BlockSpec((TM, D_MODEL), lambda i: (i, 0)),
                  pl.BlockSpec((None, D_MODEL, PROJ_PAD), lambda i: (layer, 0, 0))],
        out_specs=pl.BlockSpec((TM, PROJ_PAD), lambda i: (i, 0)),
        out_shape=jax.ShapeDtypeStruct((t, PROJ_PAD), F32),
        compiler_params=_params(1),
    )(x, w_all)


def _mix_ffn_kernel(alpha, x_ref, oa_ref, ob_ref, wm_ref, g1_ref, b1_ref, wi_ref, wo_ref, g2_ref, b2_ref, o_ref):
    m = (jnp.dot(oa_ref[...].astype(BF16), wm_ref[0:A_WIDTH, :], preferred_element_type=F32)
         + jnp.dot(ob_ref[...].astype(BF16), wm_ref[A_WIDTH:, :], preferred_element_type=F32))
    x1 = _layernorm(alpha * x_ref[...] + m, g1_ref[...], b1_ref[...])
    xb = x1.astype(BF16)
    acc = alpha * x1
    for c0 in range(0, D_FF, FF_CHUNK):
        gate = jnp.dot(xb, wi_ref[:, c0:c0 + FF_CHUNK], preferred_element_type=F32)
        up = jnp.dot(xb, wi_ref[:, D_FF + c0:D_FF + c0 + FF_CHUNK], preferred_element_type=F32)
        h = _silu(gate) * up
        acc = acc + jnp.dot(h.astype(BF16), wo_ref[c0:c0 + FF_CHUNK, :], preferred_element_type=F32)
    o_ref[...] = _layernorm(acc, g2_ref[...], b2_ref[...])


def _mix_ffn(x, oa, ob, wm_all, g1_all, b1_all, wi_all, wo_all, g2_all, b2_all, layer, alpha):
    t = x.shape[0]
    vec = lambda: pl.BlockSpec((None, 1, D_MODEL), lambda i: (layer, 0, 0))
    return pl.pallas_call(
        functools.partial(_mix_ffn_kernel, alpha),
        name="mix_ffn",
        grid=(t // TM,),
        in_specs=[pl.BlockSpec((TM, D_MODEL), lambda i: (i, 0)),
                  pl.BlockSpec((TM, A_WIDTH), lambda i: (i, 0)),
                  pl.BlockSpec((TM, B_WIDTH), lambda i: (i, 0)),
                  pl.BlockSpec((None, D_MODEL, D_MODEL), lambda i: (layer, 0, 0)),
                  vec(), vec(),
                  pl.BlockSpec((None, D_MODEL, 2 * D_FF), lambda i: (layer, 0, 0)),
                  pl.BlockSpec((None, D_FF, D_MODEL), lambda i: (layer, 0, 0)),
                  vec(), vec()],
        out_specs=pl.BlockSpec((TM, D_MODEL), lambda i: (i, 0)),
        out_shape=jax.ShapeDtypeStruct((t, D_MODEL), F32),
        compiler_params=_params(1),
    )(x, oa, ob, wm_all, g1_all, b1_all, wi_all, wo_all, g2_all, b2_all)


def _conv_silu(x, cw):
    y = cw[3:4, :] * x
    for j in range(1, CONV_WIDTH):
        y = y + cw[CONV_WIDTH - 1 - j:CONV_WIDTH - j, :] * pltpu.roll(x, j, axis=0)
    return _silu(y)


def _gates(ba, alog, dtb):
    beta = _sigmoid(ba)
    g = -jnp.exp(alog) * _softplus(ba + dtb)
    return beta, g


def _cumsum_rows(g, c):
    rin = lax.broadcasted_iota(jnp.int32, g.shape, 0) & (c - 1)
    s = 1
    while s < c:
        g = g + jnp.where(rin >= s, pltpu.roll(g, s, axis=0), 0.0)
        s *= 2
    return g


class _Masks(NamedTuple):
    causal: jax.Array
    neg_strict: jax.Array
    eye: jax.Array
    unit: tuple


def _group_masks(c):
    r = LANES
    shift = int(math.log2(c))
    ri = lax.broadcasted_iota(jnp.int32, (r, r), 0)
    ci = lax.broadcasted_iota(jnp.int32, (r, r), 1)
    same = (ri >> shift) == (ci >> shift)
    return _Masks(
        causal=jnp.where(same & (ri >= ci), 1.0, 0.0),
        neg_strict=jnp.where(same & (ri > ci), -1.0, 0.0),
        eye=jnp.where(ri == ci, 1.0, 0.0),
        unit=tuple(jnp.where((ci >> shift) == i, 1.0, 0.0) for i in range(r // c)))


def _gdn_prepare(groups, c, mk):
    r = LANES
    n_units = r // c
    n_factors = int(math.log2(c))
    each = lambda f, *lists: [f(*a) for a in zip(*lists)]

    q, k, v, beta, gcol = (list(t) for t in zip(*groups))
    qn = each(lambda t: t * lax.rsqrt(jnp.sum(t * t, axis=-1, keepdims=True) + EPS) * (DK_A ** -0.5), q)
    kn = each(lambda t: t * lax.rsqrt(jnp.sum(t * t, axis=-1, keepdims=True) + EPS), k)
    gc = each(lambda t: jnp.broadcast_to(t, (r, r)), gcol)
    e = each(lambda t: jnp.exp(jnp.minimum(t - t.T, 0.0)), gc)
    kb = each(lambda a, b: a * b, kn, beta)
    kq = each(lambda a, b, d: _dot_nt(jnp.concatenate([a, b], axis=0), d), kb, qn, kn)
    bk = each(lambda a, b: a[:r] * (b * mk.neg_strict), kq, e)
    aqk = each(lambda a, b: a[r:] * (b * mk.causal), kq, e)

    p = each(lambda t: mk.eye + t, bk)
    bk = each(lambda t: _dot(t, t), bk)
    for _ in range(n_factors - 2):
        st = each(lambda a, b: _dot(jnp.concatenate([a, b], axis=0), b), p, bk)
        p = each(lambda a, b: a + b[:r], p, st)
        bk = each(lambda t: t[r:], st)
    p = each(lambda a, b: a + _dot(a, b), p, bk)

    eg = each(jnp.exp, gc)
    uw = each(lambda a, b, d, f, h: _dot(a, jnp.concatenate([b * d, f * h], axis=1)), p, v, beta, kb, eg)
    qd = each(lambda a, b: a * b, qn, eg)

    out = []
    for gi in range(len(groups)):
        g_last = [gc[gi][(i + 1) * c - 1:(i + 1) * c, :] for i in range(n_units)]
        gl = jnp.concatenate([jnp.broadcast_to(t, (c, r)) for t in g_last], axis=0)
        kdt = (kn[gi] * jnp.exp(gl - gc[gi])).T
        u = uw[gi][:, :DV_A]
        w = uw[gi][:, DV_A:]
        wq = jnp.concatenate([jnp.concatenate([w[i * c:(i + 1) * c], qd[gi][i * c:(i + 1) * c]], axis=0)
                              for i in range(n_units)], axis=0).astype(BF16)
        lhs = jnp.concatenate([aqk[gi]] + [kdt * mk.unit[i] for i in range(n_units)], axis=0).astype(BF16)
        out.append((u, wq, lhs, [jnp.exp(t) for t in g_last]))
    return out


def _gdn_apply(items, nw, c):
    r = LANES
    n_units = r // c
    res1 = [[jnp.dot(wq[2 * c * i:2 * c * (i + 1)], st[i].astype(BF16), preferred_element_type=F32)
             for i in range(n_units)] for _, wq, _, _, st, _ in items]
    vn = [jnp.concatenate([it[0][i * c:(i + 1) * c] - r1[i][:c] for i in range(n_units)], axis=0)
          for it, r1 in zip(items, res1)]
    res2 = [jnp.dot(it[2], v.astype(BF16), preferred_element_type=F32) for it, v in zip(items, vn)]
    out = []
    for (_, _, _, egl, st, z), r1, r2 in zip(items, res1, res2):
        o = jnp.concatenate([r1[i][c:] for i in range(n_units)], axis=0) + r2[:r]
        s_new = [st[i] * egl[i] + r2[r * (i + 1):r * (i + 2)] for i in range(n_units)]
        out.append((o * lax.rsqrt(jnp.mean(o * o, axis=-1, keepdims=True) + EPS) * nw * _silu(z), s_new))
    return out


def _gdn_prompt_kernel(n_seg, prev_ref, qkv_ref, z_ref, ba_ref, cw_ref, alog_ref, dtb_ref, nw_ref,
                       o_ref, s_ref, conv_ref, xbuf, s_scr, u_scr, wq_scr, lhs_scr, egl_scr):
    seg = pl.program_id(1)
    n_pairs = H_A // 2

    @pl.when(seg == 0)
    def _():
        s_scr[...] = jnp.zeros_like(s_scr)

    xbuf[:, 0:SUBLANES, :] = jnp.where(seg > 0, prev_ref[...], 0.0)
    xbuf[:, SUBLANES:, :] = qkv_ref[...]
    cw = cw_ref[...]
    alog = alog_ref[...]
    dtb = dtb_ref[...]
    nw = nw_ref[...]
    mk = _group_masks(CHUNK)

    def prepare(it, carry):
        groups, where = [], []
        for j in range(PREP_CHUNKS):
            ci = it * PREP_CHUNKS + j
            r0 = pl.multiple_of(ci * CHUNK, CHUNK)
            for q in range(N_SEQ):
                y = _conv_silu(xbuf[q, pl.ds(r0, CHUNK + SUBLANES), :], cw)[SUBLANES:]
                beta_all, g_all = _gates(ba_ref[q, pl.ds(r0, CHUNK), :], alog, dtb)
                gc_all = _cumsum_rows(g_all, CHUNK)
                for pair in range(n_pairs):
                    heads = (2 * pair, 2 * pair + 1)
                    cols = lambda base: jnp.concatenate(
                        [y[:, base + h * LANES:base + (h + 1) * LANES] for h in heads], axis=0)
                    beta = jnp.concatenate([beta_all[:, h:h + 1] for h in heads], axis=0)
                    gcol = jnp.concatenate([gc_all[:, H_A + h:H_A + h + 1] for h in heads], axis=0)
                    groups.append((cols(0), cols(H_A * DK_A), cols(2 * H_A * DK_A), beta, gcol))
                    where.append((ci, q, pair))
        for (ci, q, pair), (u, wq, lhs, egl) in zip(where, _gdn_prepare(groups, CHUNK, mk)):
            u_scr[ci, q, pair] = u
            wq_scr[ci, q, pair] = wq
            lhs_scr[ci, q, pair] = lhs
            for i in range(2):
                egl_scr[ci, q, pair, i:i + 1, :] = egl[i]
        return carry

    lax.fori_loop(0, N_CHUNK // PREP_CHUNKS, prepare, 0)

    def scan(ci, carry):
        r0 = pl.multiple_of(ci * CHUNK, CHUNK)
        items, where = [], []
        for q in range(N_SEQ):
            z = z_ref[q, pl.ds(r0, CHUNK), :]
            for pair in range(n_pairs):
                heads = (2 * pair, 2 * pair + 1)
                zz = jnp.concatenate([z[:, h * DV_A:(h + 1) * DV_A] for h in heads], axis=0)
                egl = [egl_scr[ci, q, pair, i:i + 1, :] for i in range(2)]
                items.append((u_scr[ci, q, pair], wq_scr[ci, q, pair], lhs_scr[ci, q, pair], egl,
                              [s_scr[q, h] for h in heads], zz))
                where.append((q, heads))
        for (q, heads), (on, s_new) in zip(where, _gdn_apply(items, nw, CHUNK)):
            for i, h in enumerate(heads):
                o_ref[q, pl.ds(r0, CHUNK), h * DV_A:(h + 1) * DV_A] = on[i * CHUNK:(i + 1) * CHUNK]
                s_scr[q, h] = s_new[i]
        return carry

    lax.fori_loop(0, N_CHUNK, scan, 0)

    @pl.when(seg == n_seg - 1)
    def _():
        s_ref[:, 0] = s_scr[...]
        conv_ref[:, 0] = xbuf[:, SEG:SEG + SUBLANES, :]


def _gdn_prompt(proj, n_batch, seq, cw_all, alog_all, dtb_all, nw_all, layer):
    n_seg = seq // SEG
    seg8 = SEG // SUBLANES
    seq8 = seq // SUBLANES
    n_pairs = H_A // 2
    nb = n_batch // N_SEQ
    proj3 = proj.reshape(N_SEQ, nb * seq, PROJ_PAD)
    blk = lambda width, col: pl.BlockSpec((N_SEQ, SEG, width), lambda bb, s: (0, bb * n_seg + s, col))
    vec = lambda: pl.BlockSpec((None, 1, LANES), lambda bb, s: (layer, 0, 0))
    o, s_out, conv = pl.pallas_call(
        functools.partial(_gdn_prompt_kernel, n_seg),
        name="gdn_prompt",
        grid=(nb, n_seg),
        in_specs=[pl.BlockSpec((N_SEQ, SUBLANES, CONV_DIM),
                               lambda bb, s: (0, jnp.maximum(bb * seq8 + s * seg8 - 1, 0), 0)),
                  blk(CONV_DIM, 0), blk(A_WIDTH, COL_Z // A_WIDTH), blk(LANES, COL_BA // LANES),
                  pl.BlockSpec((None, CONV_WIDTH, CONV_DIM), lambda bb, s: (layer, 0, 0)),
                  vec(), vec(), vec()],
        out_specs=[blk(A_WIDTH, 0),
                   pl.BlockSpec((N_SEQ, 1, H_A, DK_A, DV_A), lambda bb, s: (0, bb, 0, 0, 0)),
                   pl.BlockSpec((N_SEQ, 1, SUBLANES, CONV_DIM), lambda bb, s: (0, bb, 0, 0))],
        out_shape=[jax.ShapeDtypeStruct((N_SEQ, nb * seq, A_WIDTH), F32),
                   jax.ShapeDtypeStruct((N_SEQ, nb, H_A, DK_A, DV_A), F32),
                   jax.ShapeDtypeStruct((N_SEQ, nb, SUBLANES, CONV_DIM), F32)],
        scratch_shapes=[pltpu.VMEM((N_SEQ, SEG + SUBLANES, CONV_DIM), F32),
                        pltpu.VMEM((N_SEQ, H_A, DK_A, DV_A), F32),
                        pltpu.VMEM((N_CHUNK, N_SEQ, n_pairs, LANES, DV_A), F32),
                        pltpu.VMEM((N_CHUNK, N_SEQ, n_pairs, 2 * LANES, DK_A), BF16),
                        pltpu.VMEM((N_CHUNK, N_SEQ, n_pairs, 3 * LANES, LANES), BF16),
                        pltpu.VMEM((N_CHUNK, N_SEQ, n_pairs, SUBLANES, LANES), F32)],
        compiler_params=_params(2),
    )(proj3, proj3, proj3, proj3, cw_all, alog_all, dtb_all, nw_all)
    return (o.reshape(n_batch * seq, A_WIDTH), s_out.reshape(n_batch, H_A, DK_A, DV_A),
            conv.reshape(n_batch * SUBLANES, CONV_DIM))


def _token_rows(n_rows):
    row = lax.broadcasted_iota(jnp.int32, (n_rows, 1), 0) & (S_TILE - 1)
    return (row >= ROW0) & (row < ROW0 + DEC_SEQ)


def _gdn_sample_kernel(qkv_ref, hist_ref, z_ref, ba_ref, cw_ref, alog_ref, dtb_ref, nw_ref, s_in_ref, *rest):
    o_ref, s_out_ref, conv_ref = rest[-3:]
    rows = SB * S_TILE
    valid = _token_rows(rows)
    x = jnp.where(valid, qkv_ref[...], 0.0) + hist_ref[...]
    conv_ref[...] = pltpu.roll(x, rows - DEC_SEQ, axis=0)
    y = _conv_silu(x, cw_ref[...])
    beta_all, g_all = _gates(ba_ref[...], alog_ref[...], dtb_ref[...])
    beta_all = jnp.where(valid, beta_all, 0.0)
    gc_all = _cumsum_rows(jnp.where(valid, g_all, 0.0), S_TILE)
    z = z_ref[...]
    ym = jnp.where(valid, y, 0.0)

    all_units = [(b, h) for b in range(SB) for h in range(H_A)]
    per_group = LANES // S_TILE
    unit_groups = [all_units[i:i + per_group] for i in range(0, len(all_units), per_group)]

    def stack(units, src, base, width=LANES):
        return jnp.concatenate(
            [src[b * S_TILE:(b + 1) * S_TILE, base + h * width:base + (h + 1) * width] for b, h in units], axis=0)

    mk = _group_masks(S_TILE)
    prepared = _gdn_prepare(
        [(stack(us, y, 0), stack(us, ym, H_A * DK_A), stack(us, ym, 2 * H_A * DK_A),
          stack(us, beta_all, 0, 1), stack(us, gc_all, H_A, 1)) for us in unit_groups], S_TILE, mk)
    items = [(u, wq, lhs, egl, [s_in_ref[b, h] for b, h in us], stack(us, z, 0))
             for us, (u, wq, lhs, egl) in zip(unit_groups, prepared)]
    for us, (on, s_new) in zip(unit_groups, _gdn_apply(items, nw_ref[...], S_TILE)):
        for i, (b, h) in enumerate(us):
            o_ref[b * S_TILE:(b + 1) * S_TILE, h * DV_A:(h + 1) * DV_A] = on[i * S_TILE:(i + 1) * S_TILE]
            s_out_ref[b, h] = s_new[i]


def _gdn_sample(proj, hist, state_all, cw_all, alog_all, dtb_all, nw_all, layer, s_acc):
    rows = SB * S_TILE
    n_seq = proj.shape[0] // S_TILE
    depth = state_all.shape[0]
    vec = lambda: pl.BlockSpec((None, 1, LANES), lambda i: (layer, 0, 0))
    in_specs = [pl.BlockSpec((rows, CONV_DIM), lambda i: (i, 0)),
                pl.BlockSpec((None, rows, CONV_DIM), lambda i: (layer, i, 0)),
                pl.BlockSpec((rows, A_WIDTH), lambda i: (i, COL_Z // A_WIDTH)),
                pl.BlockSpec((rows, LANES), lambda i: (i, COL_BA // LANES)),
                pl.BlockSpec((None, CONV_WIDTH, CONV_DIM), lambda i: (layer, 0, 0)),
                vec(), vec(), vec(),
                pl.BlockSpec((None, SB, H_A, DK_A, DV_A), lambda i: (layer, i, 0, 0, 0))]
    in_specs.append(pl.BlockSpec(memory_space=pl.ANY))
    args = [proj, hist, proj, proj, cw_all, alog_all, dtb_all, nw_all, state_all, s_acc]
    aliases = {len(args) - 1: 1}
    return pl.pallas_call(
        _gdn_sample_kernel,
        name="gdn_sample",
        grid=(n_seq // SB,),
        in_specs=in_specs,
        out_specs=[pl.BlockSpec((rows, A_WIDTH), lambda i: (i, 0)),
                   pl.BlockSpec((None, SB, H_A, DK_A, DV_A), lambda i: (layer, i, 0, 0, 0)),
                   pl.BlockSpec((rows, CONV_DIM), lambda i: (i, 0))],
        out_shape=[jax.ShapeDtypeStruct((n_seq * S_TILE, A_WIDTH), F32),
                   jax.ShapeDtypeStruct((depth, n_seq, H_A, DK_A, DV_A), F32),
                   jax.ShapeDtypeStruct((n_seq * S_TILE, CONV_DIM), F32)],
        input_output_aliases=aliases,
        compiler_params=_params(1),
    )(*args)


def _swa_heads(items, mask, sink_ref, q_rows):
    lane_head = lax.broadcasted_iota(jnp.int32, (1, LANES), 1) >> int(math.log2(HD_B))
    chains = [(it, hk) for it in range(len(items)) for hk in range(H_KVB)]
    each = lambda f, *lists: [f(*a) for a in zip(*lists)]
    qcat = [jnp.concatenate([q[:, gi * LANES:(gi + 1) * LANES] for gi in range(GQA_GROUP)], axis=0)
            * (HD_B ** -0.5) for q, _, _ in items]
    sk_head = [jnp.concatenate(
        [jnp.broadcast_to(sink_ref[hk * GQA_GROUP + gi:hk * GQA_GROUP + gi + 1, 0:1], (q_rows, 1))
         for gi in range(GQA_GROUP)], axis=0) for hk in range(H_KVB)]
    sk = [sk_head[hk] for _, hk in chains]
    kh = [jnp.where(lane_head == hk, items[it][1], 0.0) for it, hk in chains]
    vh = [jnp.where(lane_head == hk, items[it][2], 0.0) for it, hk in chains]
    s = [jnp.where(mask, _dot_nt(qcat[it], kh[ch]), NEG) for ch, (it, _) in enumerate(chains)]
    m = each(lambda a, b: jnp.maximum(jnp.max(a, axis=-1, keepdims=True), b), s, sk)
    p = each(lambda a, b: jnp.exp(a - b), s, m)
    den = each(lambda a, b, d: jnp.sum(a, axis=-1, keepdims=True) + jnp.exp(b - d), p, sk, m)
    o = each(lambda a, b, d: _dot(a, b) * (1.0 / d), p, vh, den)
    outs = []
    for it in range(len(items)):
        tot = o[H_KVB * it]
        for hk in range(1, H_KVB):
            tot = tot + o[H_KVB * it + hk]
        outs.append(jnp.concatenate([tot[gi * q_rows:(gi + 1) * q_rows] for gi in range(GQA_GROUP)], axis=1))
    return outs


def _swa_prompt_kernel(q_ref, kc_ref, kp_ref, vc_ref, vp_ref, sink_ref, o_ref):
    n0 = pl.program_id(1) * SWA_QB
    keys = 2 * WINDOW
    k_ext = jnp.concatenate([kp_ref[...], kc_ref[...]], axis=0)
    v_ext = jnp.concatenate([vp_ref[...], vc_ref[...]], axis=0)
    c = lax.broadcasted_iota(jnp.int32, (keys, 2 * WINDOW), 0)
    r = lax.broadcasted_iota(jnp.int32, (keys, 2 * WINDOW), 1) & (WINDOW - 1)
    band = (c > r) & (c <= r + WINDOW)
    shift = int(math.log2(HD_B))
    lane_head = lax.broadcasted_iota(jnp.int32, (1, LANES), 1) >> shift
    row = lax.broadcasted_iota(jnp.int32, (LANES, 1), 0)
    row_head = row >> shift
    each = lambda f, *lists: [f(*a) for a in zip(*lists)]
    n_gp = GQA_GROUP // 2

    chains = [(j, hk, gp) for j in range(SWA_QB) for hk in range(H_KVB) for gp in range(n_gp)]
    mask = [band & (c + (n0 + j) * WINDOW >= WINDOW) for j in range(SWA_QB)]
    kh, vth, qpair = {}, {}, {}
    for j in range(SWA_QB):
        k_all = k_ext[j * WINDOW:(j + 2) * WINDOW]
        vt = v_ext[j * WINDOW:(j + 2) * WINDOW].T
        qs = q_ref[j * WINDOW:(j + 1) * WINDOW, :] * (HD_B ** -0.5)
        for hk in range(H_KVB):
            kh[j, hk] = jnp.where(lane_head == hk, k_all, 0.0).astype(BF16)
            vth[j, hk] = jnp.where(row_head == hk, vt, jnp.where(row == (1 - hk) * HD_B, 1.0, 0.0)).astype(BF16)
        for gp in range(n_gp):
            qpair[j, gp] = jnp.concatenate([qs[:, (2 * gp) * LANES:(2 * gp + 1) * LANES],
                                            qs[:, (2 * gp + 1) * LANES:(2 * gp + 2) * LANES]],
                                           axis=0).astype(BF16)
    ones_row = [(1 - hk) * HD_B for _, hk, _ in chains]
    sk = [jnp.concatenate([sink_ref[hk * GQA_GROUP + 2 * gp + i:hk * GQA_GROUP + 2 * gp + i + 1, :]
                           for i in range(2)], axis=1) for _, hk, gp in chains]
    st = [jnp.where(mask[j], _dot_nt(kh[j, hk], qpair[j, gp]), NEG) for j, hk, gp in chains]
    m = each(lambda a, b: jnp.maximum(jnp.max(a, axis=0, keepdims=True), b), st, sk)
    pt = each(lambda a, b: jnp.exp(a - b).astype(BF16), st, m)
    ot = [jnp.dot(vth[j, hk], pt[ch], preferred_element_type=F32) for ch, (j, hk, _) in enumerate(chains)]
    den = each(lambda a, i, b, d: a[i:i + 1, :] + jnp.exp(b - d), ot, ones_row, sk, m)
    ot = [jnp.where(row_head == hk, ot[ch] * (1.0 / den[ch]), 0.0) for ch, (_, hk, _) in enumerate(chains)]
    for j in range(SWA_QB):
        tiles = []
        for gp in range(n_gp):
            tot = ot[chains.index((j, 0, gp))]
            for hk in range(1, H_KVB):
                tot = tot + ot[chains.index((j, hk, gp))]
            tiles += [tot[:, :WINDOW].T, tot[:, WINDOW:].T]
        o_ref[j * WINDOW:(j + 1) * WINDOW, :] = jnp.concatenate(tiles, axis=1)


def _swa_prompt(proj, n_batch, seq, sink_all, layer):
    nb = seq // WINDOW
    ns = nb // SWA_QB
    qrows = SWA_QB * WINDOW
    cur = lambda col: (lambda b, n: (b * ns + n, col))
    prev = lambda col: (lambda b, n: (b * nb + jnp.maximum(n * SWA_QB - 1, 0), col))
    return pl.pallas_call(
        _swa_prompt_kernel,
        name="swa_prompt",
        grid=(n_batch, ns),
        in_specs=[pl.BlockSpec((qrows, B_WIDTH), cur(COL_QB // B_WIDTH)),
                  pl.BlockSpec((qrows, KV_WIDTH), cur(COL_KB // KV_WIDTH)),
                  pl.BlockSpec((WINDOW, KV_WIDTH), prev(COL_KB // KV_WIDTH)),
                  pl.BlockSpec((qrows, KV_WIDTH), cur(COL_VB // KV_WIDTH)),
                  pl.BlockSpec((WINDOW, KV_WIDTH), prev(COL_VB // KV_WIDTH)),
                  pl.BlockSpec((None, H_QB, LANES), lambda b, n: (layer, 0, 0))],
        out_specs=pl.BlockSpec((qrows, B_WIDTH), lambda b, n: (b * ns + n, 0)),
        out_shape=jax.ShapeDtypeStruct((n_batch * seq, B_WIDTH), F32),
        compiler_params=_params(2),
    )(proj, proj, proj, proj, proj, sink_all)


def _shift_cache(cache, new_tile):
    rolled = pltpu.roll(cache, WINDOW - DEC_SEQ, axis=0)
    moved = pltpu.roll(new_tile, S_TILE - DEC_SEQ - ROW0, axis=0)
    row = lax.broadcasted_iota(jnp.int32, (S_TILE, 1), 0)
    tail = jnp.where(row >= S_TILE - DEC_SEQ, moved, rolled[WINDOW - S_TILE:])
    return jnp.concatenate([rolled[:WINDOW - S_TILE], tail], axis=0)


def _swa_sample_kernel(q_ref, k_ref, v_ref, kc_ref, vc_ref, sink_ref, *rest):
    o_ref, ko_ref, vo_ref = rest[-3:]
    shape = (GQA_GROUP * S_TILE, 2 * WINDOW)
    r = (lax.broadcasted_iota(jnp.int32, shape, 0) & (S_TILE - 1)) - ROW0
    c = lax.broadcasted_iota(jnp.int32, shape, 1)
    j = c - WINDOW - ROW0
    mask = ((c < WINDOW) & (c > r)) | ((j >= 0) & (j < DEC_SEQ) & (j <= r))
    pad = jnp.zeros((WINDOW - S_TILE, KV_WIDTH), F32)
    items = []
    for b in range(SB):
        rows = slice(b * S_TILE, (b + 1) * S_TILE)
        k_new = k_ref[rows, :]
        v_new = v_ref[rows, :]
        items.append((q_ref[rows, :], jnp.concatenate([kc_ref[b], k_new, pad], axis=0),
                      jnp.concatenate([vc_ref[b], v_new, pad], axis=0)))
        ko_ref[b] = _shift_cache(kc_ref[b], k_new)
        vo_ref[b] = _shift_cache(vc_ref[b], v_new)
    for b, o in enumerate(_swa_heads(items, mask, sink_ref, S_TILE)):
        o_ref[b * S_TILE:(b + 1) * S_TILE, :] = o


def _swa_sample(proj, kc_all, vc_all, sink_all, layer, k_acc, v_acc):
    rows = SB * S_TILE
    n_seq = proj.shape[0] // S_TILE
    depth = kc_all.shape[0]
    cache = lambda: pl.BlockSpec((None, SB, WINDOW, KV_WIDTH), lambda i: (layer, i, 0, 0))
    in_specs = [pl.BlockSpec((rows, B_WIDTH), lambda i: (i, COL_QB // B_WIDTH)),
                pl.BlockSpec((rows, KV_WIDTH), lambda i: (i, COL_KB // KV_WIDTH)),
                pl.BlockSpec((rows, KV_WIDTH), lambda i: (i, COL_VB // KV_WIDTH)),
                cache(), cache(),
                pl.BlockSpec((None, H_QB, LANES), lambda i: (layer, 0, 0))]
    in_specs += [pl.BlockSpec(memory_space=pl.ANY)] * 2
    args = [proj, proj, proj, kc_all, vc_all, sink_all, k_acc, v_acc]
    aliases = {len(args) - 2: 1, len(args) - 1: 2}
    return pl.pallas_call(
        _swa_sample_kernel,
        name="swa_sample",
        grid=(n_seq // SB,),
        in_specs=in_specs,
        out_specs=[pl.BlockSpec((rows, B_WIDTH), lambda i: (i, 0)), cache(), cache()],
        out_shape=[jax.ShapeDtypeStruct((n_seq * S_TILE, B_WIDTH), F32),
                   jax.ShapeDtypeStruct((depth, n_seq, WINDOW, KV_WIDTH), F32),
                   jax.ShapeDtypeStruct((depth, n_seq, WINDOW, KV_WIDTH), F32)],
        input_output_aliases=aliases,
        compiler_params=_params(1),
    )(*args)


def kernel(x_prompt, x_sample, state_delta, state_conv, cache_swa_k, cache_swa_v, w_in, conv_w, a_log,
           dt_bias, norm_a_w, sinks, w_out, ln1_g, ln1_b, w_ffn_in, w_ffn_out, ln2_g, ln2_b):
    depth = w_in.shape[0]
    n_batch, seq, _ = x_prompt.shape
    n_dec, dec_seq, _ = x_sample.shape
    assert dec_seq == DEC_SEQ and seq % SEG == 0 and n_dec % SB == 0 and n_batch % N_SEQ == 0
    assert cache_swa_k.shape[2] == WINDOW
    alpha = (2 * depth) ** 0.25

    c0 = CONV_DIM + A_WIDTH
    q0 = c0 + 2 * H_A
    order = jnp.array(QB_HEAD_ORDER)
    w_q = w_in[:, :, q0:q0 + B_WIDTH].reshape(depth, D_MODEL, H_QB, HD_B)[:, :, order]
    w_in_r = jnp.concatenate(
        [w_in[:, :, :c0], w_q.reshape(depth, D_MODEL, B_WIDTH), w_in[:, :, q0 + B_WIDTH:],
         w_in[:, :, c0:q0], jnp.zeros((depth, D_MODEL, LANES - 2 * H_A), w_in.dtype)], axis=-1).astype(BF16)
    w_out_bq = w_out[:, A_WIDTH:].reshape(depth, H_QB, HD_B, D_MODEL)[:, order].reshape(depth, B_WIDTH, D_MODEL)
    w_out_b = jnp.concatenate([w_out[:, :A_WIDTH], w_out_bq], axis=1).astype(BF16)
    w_ffn_in_b = w_ffn_in.astype(BF16)
    w_ffn_out_b = w_ffn_out.astype(BF16)
    lane_pad = lambda t: jnp.pad(t, ((0, 0), (H_A, LANES - 2 * H_A)))[:, None, :]
    alog_v = lane_pad(a_log)
    dtb_v = lane_pad(dt_bias)
    nw_v = norm_a_w[:, None, :]
    sink_v = jnp.broadcast_to(sinks[:, :, None], (depth, H_QB, LANES))
    ln1_g3, ln1_b3, ln2_g3, ln2_b3 = (t[:, None, :] for t in (ln1_g, ln1_b, ln2_g, ln2_b))

    hist = jnp.pad(state_conv, ((0, 0), (0, 0), (0, S_TILE - (CONV_WIDTH - 1)), (0, 0)))
    hist = hist.reshape(depth, n_dec * S_TILE, CONV_DIM)
    kc = cache_swa_k.reshape(depth, n_dec, WINDOW, KV_WIDTH)
    vc = cache_swa_v.reshape(depth, n_dec, WINDOW, KV_WIDTH)

    xp = x_prompt.reshape(n_batch * seq, D_MODEL)
    xs = jnp.pad(x_sample, ((0, 0), (ROW0, S_TILE - ROW0 - DEC_SEQ), (0, 0))).reshape(n_dec * S_TILE, D_MODEL)

    outs = [[] for _ in range(8)]
    s_acc = jnp.zeros((depth, n_dec, H_A, DK_A, DV_A), F32)
    k_acc = jnp.zeros((depth, n_dec, WINDOW, KV_WIDTH), F32)
    v_acc = jnp.zeros((depth, n_dec, WINDOW, KV_WIDTH), F32)
    for l in range(depth):
        proj = _in_proj(xp, w_in_r, l)
        oa, s_p, conv_p = _gdn_prompt(proj, n_batch, seq, conv_w, alog_v, dtb_v, nw_v, l)
        ob = _swa_prompt(proj, n_batch, seq, sink_v, l)
        xp = _mix_ffn(xp, oa, ob, w_out_b, ln1_g3, ln1_b3, w_ffn_in_b, w_ffn_out_b, ln2_g3, ln2_b3, l, alpha)
        proj3 = proj.reshape(n_batch, seq, PROJ_PAD)
        outs[0].append(s_p)
        outs[1].append(conv_p.reshape(n_batch, SUBLANES, CONV_DIM)[:, SUBLANES - (CONV_WIDTH - 1):])
        outs[2].append(proj3[:, seq - WINDOW:, COL_KB:COL_KB + KV_WIDTH].reshape(n_batch, WINDOW, H_KVB, HD_B))
        outs[3].append(proj3[:, seq - WINDOW:, COL_VB:COL_VB + KV_WIDTH].reshape(n_batch, WINDOW, H_KVB, HD_B))
        proj = _in_proj(xs, w_in_r, l)
        oa, s_acc, conv_s = _gdn_sample(proj, hist, state_delta, conv_w, alog_v, dtb_v, nw_v, l, s_acc)
        ob, k_acc, v_acc = _swa_sample(proj, kc, vc, sink_v, l, k_acc, v_acc)
        xs = _mix_ffn(xs, oa, ob, w_out_b, ln1_g3, ln1_b3, w_ffn_in_b, w_ffn_out_b, ln2_g3, ln2_b3, l, alpha)
        outs[5].append(conv_s.reshape(n_dec, S_TILE, CONV_DIM)[:, :CONV_WIDTH - 1])

    y_prompt = xp.reshape(n_batch, seq, D_MODEL)
    y_sample = xs.reshape(n_dec, S_TILE, D_MODEL)[:, ROW0:ROW0 + DEC_SEQ]
    stacked = [jnp.stack(o) if o else None for o in outs]
    stacked[4] = s_acc
    stacked[6] = k_acc.reshape(depth, n_dec, WINDOW, H_KVB, HD_B)
    stacked[7] = v_acc.reshape(depth, n_dec, WINDOW, H_KVB, HD_B)
    return (y_prompt, y_sample) + tuple(stacked)
```

```python
import functools
import math
from typing import NamedTuple

import jax
import jax.numpy as jnp
from jax import lax
from jax.experimental import pallas as pl
from jax.experimental.pallas import tpu as pltpu

F32 = jnp.float32
BF16 = jnp.bfloat16

D_MODEL = 1024
H_A = 4
DK_A = 128
DV_A = 128
CONV_WIDTH = 4
CONV_DIM = 2 * H_A * DK_A + H_A * DV_A
A_WIDTH = H_A * DV_A
CHUNK = 64
HD_B = 64
H_QB = 8
H_KVB = 2
GQA_GROUP = H_QB // H_KVB
B_WIDTH = H_QB * HD_B
KV_WIDTH = H_KVB * HD_B
WINDOW = 128
D_FF = 2816
EPS = 1e-6

LANES = 128
SUBLANES = 8

COL_Z = CONV_DIM
COL_QB = COL_Z + A_WIDTH
COL_KB = COL_QB + B_WIDTH
COL_VB = COL_KB + KV_WIDTH
COL_BA = COL_VB + KV_WIDTH
PROJ_PAD = COL_BA + LANES
QB_HEAD_ORDER = tuple(hk * GQA_GROUP + g for g in range(GQA_GROUP) for hk in range(H_KVB))

S_TILE = SUBLANES
ROW0 = CONV_WIDTH - 1
DEC_SEQ = 4
SB = 16

TM = 512
SEG = 128
N_CHUNK = SEG // CHUNK
PREP_CHUNKS = 1
N_SEQ = 8
FF_CHUNK = 256
SWA_QB = 4
NEG = -1e30
LOG2E = math.log2(math.e)
VMEM_LIMIT = 56 * 1024 * 1024


def _params(n_axes, vmem=None):
    return pltpu.CompilerParams(
        dimension_semantics=("arbitrary",) * n_axes,
        vmem_limit_bytes=vmem if vmem is not None else VMEM_LIMIT)


def _dot(a, b):
    return jnp.dot(a.astype(BF16), b.astype(BF16), preferred_element_type=F32)


def _dot_nt(a, b):
    return lax.dot_general(a.astype(BF16), b.astype(BF16), (((1,), (1,)), ((), ())),
                           preferred_element_type=F32)


def _sigmoid(x):
    return 1.0 / (1.0 + jnp.exp(-x))


def _silu(x):
    h = 0.5 * x
    return h + h * jnp.tanh(h)


def _softplus(x):
    return jnp.maximum(x, 0.0) + jnp.log(1.0 + jnp.exp(-jnp.abs(x)))


def _layernorm(h, g, b):
    mu = jnp.mean(h, axis=-1, keepdims=True)
    d = h - mu
    var = jnp.mean(d * d, axis=-1, keepdims=True)
    return d * lax.rsqrt(var + EPS) * g + b


def _in_proj_kernel(x_ref, w_ref, o_ref):
    xb = x_ref[...].astype(BF16)
    n = o_ref.shape[1]
    for n0 in range(0, n, 512):
        n1 = min(n0 + 512, n)
        o_ref[:, n0:n1] = jnp.dot(xb, w_ref[:, n0:n1], preferred_element_type=F32)


def _in_proj(x, w_all, layer):
    t = x.shape[0]
    return pl.pallas_call(
        _in_proj_kernel,
        name="in_proj",
        grid=(t // TM,),
        in_specs=[pl.BlockSpec((TM, D_MODEL), lambda i: (i, 0)),
                  pl.BlockSpec((None, D_MODEL, PROJ_PAD), lambda i: (layer, 0, 0))],
        out_specs=pl.BlockSpec((TM, PROJ_PAD), lambda i: (i, 0)),
        out_shape=jax.ShapeDtypeStruct((t, PROJ_PAD), F32),
        compiler_params=_params(1),
    )(x, w_all)


def _mix_ffn_kernel(alpha, x_ref, oa_ref, ob_ref, wm_ref, g1_ref, b1_ref, wi_ref, wo_ref, g2_ref, b2_ref, o_ref):
    m = (jnp.dot(oa_ref[...].astype(BF16), wm_ref[0:A_WIDTH, :], preferred_element_type=F32)
         + jnp.dot(ob_ref[...].astype(BF16), wm_ref[A_WIDTH:, :], preferred_element_type=F32))
    x1 = _layernorm(alpha * x_ref[...] + m, g1_ref[...], b1_ref[...])
    xb = x1.astype(BF16)
    acc = alpha * x1
    for c0 in range(0, D_FF, FF_CHUNK):
        gate = jnp.dot(xb, wi_ref[:, c0:c0 + FF_CHUNK], preferred_element_type=F32)
        up = jnp.dot(xb, wi_ref[:, D_FF + c0:D_FF + c0 + FF_CHUNK], preferred_element_type=F32)
        h = _silu(gate) * up
        acc = acc + jnp.dot(h.astype(BF16), wo_ref[c0:c0 + FF_CHUNK, :], preferred_element_type=F32)
    o_ref[...] = _layernorm(acc, g2_ref[...], b2_ref[...])


def _mix_ffn(x, oa, ob, wm_all, g1_all, b1_all, wi_all, wo_all, g2_all, b2_all, layer, alpha):
    t = x.shape[0]
    vec = lambda: pl.BlockSpec((None, 1, D_MODEL), lambda i: (layer, 0, 0))
    return pl.pallas_call(
        functools.partial(_mix_ffn_kernel, alpha),
        name="mix_ffn",
        grid=(t // TM,),
        in_specs=[pl.BlockSpec((TM, D_MODEL), lambda i: (i, 0)),
                  pl.BlockSpec((TM, A_WIDTH), lambda i: (i, 0)),
                  pl.BlockSpec((TM, B_WIDTH), lambda i: (i, 0)),
                  pl.BlockSpec((None, D_MODEL, D_MODEL), lambda i: (layer, 0, 0)),
                  vec(), vec(),
                  pl.BlockSpec((None, D_MODEL, 2 * D_FF), lambda i: (layer, 0, 0)),
                  pl.BlockSpec((None, D_FF, D_MODEL), lambda i: (layer, 0, 0)),
                  vec(), vec()],
        out_specs=pl.BlockSpec((TM, D_MODEL), lambda i: (i, 0)),
        out_shape=jax.ShapeDtypeStruct((t, D_MODEL), F32),
        compiler_params=_params(1),
    )(x, oa, ob, wm_all, g1_all, b1_all, wi_all, wo_all, g2_all, b2_all)


def _conv_silu(x, cw):
    y = cw[3:4, :] * x
    for j in range(1, CONV_WIDTH):
        y = y + cw[CONV_WIDTH - 1 - j:CONV_WIDTH - j, :] * pltpu.roll(x, j, axis=0)
    return _silu(y)


def _gates(ba, alog, dtb):
    beta = _sigmoid(ba)
    g = -jnp.exp(alog) * _softplus(ba + dtb)
    return beta, g


def _cumsum_rows(g, c):
    rin = lax.broadcasted_iota(jnp.int32, g.shape, 0) & (c - 1)
    s = 1
    while s < c:
        g = g + jnp.where(rin >= s, pltpu.roll(g, s, axis=0), 0.0)
        s *= 2
    return g


class _Masks(NamedTuple):
    causal: jax.Array
    neg_strict: jax.Array
    eye: jax.Array
    unit: tuple


def _group_masks(c):
    r = LANES
    shift = int(math.log2(c))
    ri = lax.broadcasted_iota(jnp.int32, (r, r), 0)
    ci = lax.broadcasted_iota(jnp.int32, (r, r), 1)
    same = (ri >> shift) == (ci >> shift)
    return _Masks(
        causal=jnp.where(same & (ri >= ci), 1.0, 0.0),
        neg_strict=jnp.where(same & (ri > ci), -1.0, 0.0),
        eye=jnp.where(ri == ci, 1.0, 0.0),
        unit=tuple(jnp.where((ci >> shift) == i, 1.0, 0.0) for i in range(r // c)))


def _gdn_prepare(groups, c, mk):
    r = LANES
    n_units = r // c
    n_factors = int(math.log2(c))
    each = lambda f, *lists: [f(*a) for a in zip(*lists)]

    q, k, v, beta, gcol = (list(t) for t in zip(*groups))
    qn = each(lambda t: t * lax.rsqrt(jnp.sum(t * t, axis=-1, keepdims=True) + EPS) * (DK_A ** -0.5), q)
    kn = each(lambda t: t * lax.rsqrt(jnp.sum(t * t, axis=-1, keepdims=True) + EPS), k)
    gc = each(lambda t: jnp.broadcast_to(t, (r, r)), gcol)
    e = each(lambda t: jnp.exp(jnp.minimum(t - t.T, 0.0)), gc)
    kb = each(lambda a, b: a * b, kn, beta)
    kq = each(lambda a, b, d: _dot_nt(jnp.concatenate([a, b], axis=0), d), kb, qn, kn)
    bk = each(lambda a, b: a[:r] * (b * mk.neg_strict), kq, e)
    aqk = each(lambda a, b: a[r:] * (b * mk.causal), kq, e)

    p = each(lambda t: mk.eye + t, bk)
    bk = each(lambda t: _dot(t, t), bk)
    for _ in range(n_factors - 2):
        st = each(lambda a, b: _dot(jnp.concatenate([a, b], axis=0), b), p, bk)
        p = each(lambda a, b: a + b[:r], p, st)
        bk = each(lambda t: t[r:], st)
    p = each(lambda a, b: a + _dot(a, b), p, bk)

    eg = each(jnp.exp, gc)
    uw = each(lambda a, b, d, f, h: _dot(a, jnp.concatenate([b * d, f * h], axis=1)), p, v, beta, kb, eg)
    qd = each(lambda a, b: a * b, qn, eg)

    out = []
    for gi in range(len(groups)):
        g_last = [gc[gi][(i + 1) * c - 1:(i + 1) * c, :] for i in range(n_units)]
        gl = jnp.concatenate([jnp.broadcast_to(t, (c, r)) for t in g_last], axis=0)
        kdt = (kn[gi] * jnp.exp(gl - gc[gi])).T
        u = uw[gi][:, :DV_A]
        w = uw[gi][:, DV_A:]
        wq = jnp.concatenate([jnp.concatenate([w[i * c:(i + 1) * c], qd[gi][i * c:(i + 1) * c]], axis=0)
                              for i in range(n_units)], axis=0).astype(BF16)
        lhs = jnp.concatenate([aqk[gi]] + [kdt * mk.unit[i] for i in range(n_units)], axis=0).astype(BF16)
        out.append((u, wq, lhs, [jnp.exp(t) for t in g_last]))
    return out


def _gdn_apply(items, nw, c):
    r = LANES
    n_units = r // c
    res1 = [[jnp.dot(wq[2 * c * i:2 * c * (i + 1)], st[i].astype(BF16), preferred_element_type=F32)
             for i in range(n_units)] for _, wq, _, _, st, _ in items]
    vn = [jnp.concatenate([it[0][i * c:(i + 1) * c] - r1[i][:c] for i in range(n_units)], axis=0)
          for it, r1 in zip(items, res1)]
    res2 = [jnp.dot(it[2], v.astype(BF16), preferred_element_type=F32) for it, v in zip(items, vn)]
    out = []
    for (_, _, _, egl, st, z), r1, r2 in zip(items, res1, res2):
        o = jnp.concatenate([r1[i][c:] for i in range(n_units)], axis=0) + r2[:r]
        s_new = [st[i] * egl[i] + r2[r * (i + 1):r * (i + 2)] for i in range(n_units)]
        out.append((o * lax.rsqrt(jnp.mean(o * o, axis=-1, keepdims=True) + EPS) * nw * _silu(z), s_new))
    return out


def _gdn_prompt_kernel(n_seg, prev_ref, qkv_ref, z_ref, ba_ref, cw_ref, alog_ref, dtb_ref, nw_ref,
                       o_ref, s_ref, conv_ref, s_scr, u_scr, wq_scr, lhs_scr, egl_scr):
    seg = pl.program_id(1)
    n_pairs = H_A // 2

    @pl.when(seg == 0)
    def _():
        s_scr[...] = jnp.zeros_like(s_scr)

    cw = cw_ref[...]
    alog = alog_ref[...]
    dtb = dtb_ref[...]
    nw = nw_ref[...]
    mk = _group_masks(CHUNK)

    def prepare(it, carry):
        groups, where = [], []
        for j in range(PREP_CHUNKS):
            ci = it * PREP_CHUNKS + j
            r0 = pl.multiple_of(ci * CHUNK, CHUNK)
            for q in range(N_SEQ):
                before = jnp.where(ci > 0, qkv_ref[q, pl.ds(pl.multiple_of(jnp.maximum(r0 - SUBLANES, 0), SUBLANES),
                                                            SUBLANES), :],
                                   jnp.where(seg > 0, prev_ref[q], 0.0))
                y = _conv_silu(jnp.concatenate([before, qkv_ref[q, pl.ds(r0, CHUNK), :]], axis=0), cw)[SUBLANES:]
                beta_all, g_all = _gates(ba_ref[q, pl.ds(r0, CHUNK), :], alog, dtb)
                gc_all = _cumsum_rows(g_all, CHUNK)
                for pair in range(n_pairs):
                    heads = (2 * pair, 2 * pair + 1)
                    cols = lambda base: jnp.concatenate(
                        [y[:, base + h * LANES:base + (h + 1) * LANES] for h in heads], axis=0)
                    beta = jnp.concatenate([beta_all[:, h:h + 1] for h in heads], axis=0)
                    gcol = jnp.concatenate([gc_all[:, H_A + h:H_A + h + 1] for h in heads], axis=0)
                    groups.append((cols(0), cols(H_A * DK_A), cols(2 * H_A * DK_A), beta, gcol))
                    where.append((ci, q, pair))
        for (ci, q, pair), (u, wq, lhs, egl) in zip(where, _gdn_prepare(groups, CHUNK, mk)):
            u_scr[ci, q, pair] = u
            wq_scr[ci, q, pair] = wq
            lhs_scr[ci, q, pair] = lhs
            for i in range(2):
                egl_scr[ci, q, pair, i:i + 1, :] = egl[i]
        return carry

    lax.fori_loop(0, N_CHUNK // PREP_CHUNKS, prepare, 0)

    def scan(ci, carry):
        r0 = pl.multiple_of(ci * CHUNK, CHUNK)
        items, where = [], []
        for q in range(N_SEQ):
            z = z_ref[q, pl.ds(r0, CHUNK), :]
            for pair in range(n_pairs):
                heads = (2 * pair, 2 * pair + 1)
                zz = jnp.concatenate([z[:, h * DV_A:(h + 1) * DV_A] for h in heads], axis=0)
                egl = [egl_scr[ci, q, pair, i:i + 1, :] for i in range(2)]
                items.append((u_scr[ci, q, pair], wq_scr[ci, q, pair], lhs_scr[ci, q, pair], egl,
                              [s_scr[q, h] for h in heads], zz))
                where.append((q, heads))
        for (q, heads), (on, s_new) in zip(where, _gdn_apply(items, nw, CHUNK)):
            for i, h in enumerate(heads):
                o_ref[q, pl.ds(r0, CHUNK), h * DV_A:(h + 1) * DV_A] = on[i * CHUNK:(i + 1) * CHUNK]
                s_scr[q, h] = s_new[i]
        return carry

    lax.fori_loop(0, N_CHUNK, scan, 0)

    @pl.when(seg == n_seg - 1)
    def _():
        s_ref[:, 0] = s_scr[...]
        conv_ref[:, 0] = qkv_ref[:, SEG - SUBLANES:SEG, :]


def _gdn_prompt(proj, n_batch, seq, cw_all, alog_all, dtb_all, nw_all, layer):
    n_seg = seq // SEG
    seg8 = SEG // SUBLANES
    seq8 = seq // SUBLANES
    n_pairs = H_A // 2
    nb = n_batch // N_SEQ
    proj3 = proj.reshape(N_SEQ, nb * seq, PROJ_PAD)
    blk = lambda width, col: pl.BlockSpec((N_SEQ, SEG, width), lambda bb, s: (0, bb * n_seg + s, col))
    vec = lambda: pl.BlockSpec((None, 1, LANES), lambda bb, s: (layer, 0, 0))
    o, s_out, conv = pl.pallas_call(
        functools.partial(_gdn_prompt_kernel, n_seg),
        name="gdn_prompt",
        grid=(nb, n_seg),
        in_specs=[pl.BlockSpec((N_SEQ, SUBLANES, CONV_DIM),
                               lambda bb, s: (0, jnp.maximum(bb * seq8 + s * seg8 - 1, 0), 0)),
                  blk(CONV_DIM, 0), blk(A_WIDTH, COL_Z // A_WIDTH), blk(LANES, COL_BA // LANES),
                  pl.BlockSpec((None, CONV_WIDTH, CONV_DIM), lambda bb, s: (layer, 0, 0)),
                  vec(), vec(), vec()],
        out_specs=[blk(A_WIDTH, 0),
                   pl.BlockSpec((N_SEQ, 1, H_A, DK_A, DV_A), lambda bb, s: (0, bb, 0, 0, 0)),
                   pl.BlockSpec((N_SEQ, 1, SUBLANES, CONV_DIM), lambda bb, s: (0, bb, 0, 0))],
        out_shape=[jax.ShapeDtypeStruct((N_SEQ, nb * seq, A_WIDTH), F32),
                   jax.ShapeDtypeStruct((N_SEQ, nb, H_A, DK_A, DV_A), F32),
                   jax.ShapeDtypeStruct((N_SEQ, nb, SUBLANES, CONV_DIM), F32)],
        scratch_shapes=[pltpu.VMEM((N_SEQ, H_A, DK_A, DV_A), F32),
                        pltpu.VMEM((N_CHUNK, N_SEQ, n_pairs, LANES, DV_A), F32),
                        pltpu.VMEM((N_CHUNK, N_SEQ, n_pairs, 2 * LANES, DK_A), BF16),
                        pltpu.VMEM((N_CHUNK, N_SEQ, n_pairs, 3 * LANES, LANES), BF16),
                        pltpu.VMEM((N_CHUNK, N_SEQ, n_pairs, SUBLANES, LANES), F32)],
        compiler_params=_params(2),
    )(proj3, proj3, proj3, proj3, cw_all, alog_all, dtb_all, nw_all)
    return (o.reshape(n_batch * seq, A_WIDTH), s_out.reshape(n_batch, H_A, DK_A, DV_A),
            conv.reshape(n_batch * SUBLANES, CONV_DIM))


def _token_rows(n_rows):
    row = lax.broadcasted_iota(jnp.int32, (n_rows, 1), 0) & (S_TILE - 1)
    return (row >= ROW0) & (row < ROW0 + DEC_SEQ)


def _gdn_sample_kernel(qkv_ref, hist_ref, z_ref, ba_ref, cw_ref, alog_ref, dtb_ref, nw_ref, s_in_ref, *rest):
    o_ref, s_out_ref, conv_ref = rest[-3:]
    rows = SB * S_TILE
    valid = _token_rows(rows)
    x = jnp.where(valid, qkv_ref[...], 0.0) + hist_ref[...]
    conv_ref[...] = pltpu.roll(x, rows - DEC_SEQ, axis=0)
    y = _conv_silu(x, cw_ref[...])
    beta_all, g_all = _gates(ba_ref[...], alog_ref[...], dtb_ref[...])
    beta_all = jnp.where(valid, beta_all, 0.0)
    gc_all = _cumsum_rows(jnp.where(valid, g_all, 0.0), S_TILE)
    z = z_ref[...]
    ym = jnp.where(valid, y, 0.0)

    all_units = [(b, h) for b in range(SB) for h in range(H_A)]
    per_group = LANES // S_TILE
    unit_groups = [all_units[i:i + per_group] for i in range(0, len(all_units), per_group)]

    def stack(units, src, base, width=LANES):
        return jnp.concatenate(
            [src[b * S_TILE:(b + 1) * S_TILE, base + h * width:base + (h + 1) * width] for b, h in units], axis=0)

    mk = _group_masks(S_TILE)
    prepared = _gdn_prepare(
        [(stack(us, y, 0), stack(us, ym, H_A * DK_A), stack(us, ym, 2 * H_A * DK_A),
          stack(us, beta_all, 0, 1), stack(us, gc_all, H_A, 1)) for us in unit_groups], S_TILE, mk)
    items = [(u, wq, lhs, egl, [s_in_ref[b, h] for b, h in us], stack(us, z, 0))
             for us, (u, wq, lhs, egl) in zip(unit_groups, prepared)]
    for us, (on, s_new) in zip(unit_groups, _gdn_apply(items, nw_ref[...], S_TILE)):
        for i, (b, h) in enumerate(us):
            o_ref[b * S_TILE:(b + 1) * S_TILE, h * DV_A:(h + 1) * DV_A] = on[i * S_TILE:(i + 1) * S_TILE]
            s_out_ref[b, h] = s_new[i]


def _gdn_sample(proj, hist, state_all, cw_all, alog_all, dtb_all, nw_all, layer, s_acc):
    rows = SB * S_TILE
    n_seq = proj.shape[0] // S_TILE
    depth = state_all.shape[0]
    vec = lambda: pl.BlockSpec((None, 1, LANES), lambda i: (layer, 0, 0))
    in_specs = [pl.BlockSpec((rows, CONV_DIM), lambda i: (i, 0)),
                pl.BlockSpec((None, rows, CONV_DIM), lambda i: (layer, i, 0)),
                pl.BlockSpec((rows, A_WIDTH), lambda i: (i, COL_Z // A_WIDTH)),
                pl.BlockSpec((rows, LANES), lambda i: (i, COL_BA // LANES)),
                pl.BlockSpec((None, CONV_WIDTH, CONV_DIM), lambda i: (layer, 0, 0)),
                vec(), vec(), vec(),
                pl.BlockSpec((None, SB, H_A, DK_A, DV_A), lambda i: (layer, i, 0, 0, 0))]
    in_specs.append(pl.BlockSpec(memory_space=pl.ANY))
    args = [proj, hist, proj, proj, cw_all, alog_all, dtb_all, nw_all, state_all, s_acc]
    aliases = {len(args) - 1: 1}
    return pl.pallas_call(
        _gdn_sample_kernel,
        name="gdn_sample",
        grid=(n_seq // SB,),
        in_specs=in_specs,
        out_specs=[pl.BlockSpec((rows, A_WIDTH), lambda i: (i, 0)),
                   pl.BlockSpec((None, SB, H_A, DK_A, DV_A), lambda i: (layer, i, 0, 0, 0)),
                   pl.BlockSpec((rows, CONV_DIM), lambda i: (i, 0))],
        out_shape=[jax.ShapeDtypeStruct((n_seq * S_TILE, A_WIDTH), F32),
                   jax.ShapeDtypeStruct((depth, n_seq, H_A, DK_A, DV_A), F32),
                   jax.ShapeDtypeStruct((n_seq * S_TILE, CONV_DIM), F32)],
        input_output_aliases=aliases,
        compiler_params=_params(1),
    )(*args)


def _swa_heads(items, mask, sink_ref, q_rows):
    lane_head = lax.broadcasted_iota(jnp.int32, (1, LANES), 1) >> int(math.log2(HD_B))
    chains = [(it, hk) for it in range(len(items)) for hk in range(H_KVB)]
    each = lambda f, *lists: [f(*a) for a in zip(*lists)]
    qcat = [jnp.concatenate([q[:, gi * LANES:(gi + 1) * LANES] for gi in range(GQA_GROUP)], axis=0)
            * (HD_B ** -0.5) for q, _, _ in items]
    sk_head = [jnp.concatenate(
        [jnp.broadcast_to(sink_ref[hk * GQA_GROUP + gi:hk * GQA_GROUP + gi + 1, 0:1], (q_rows, 1))
         for gi in range(GQA_GROUP)], axis=0) for hk in range(H_KVB)]
    sk = [sk_head[hk] for _, hk in chains]
    kh = [jnp.where(lane_head == hk, items[it][1], 0.0) for it, hk in chains]
    vh = [jnp.where(lane_head == hk, items[it][2], 0.0) for it, hk in chains]
    s = [jnp.where(mask, _dot_nt(qcat[it], kh[ch]), NEG) for ch, (it, _) in enumerate(chains)]
    m = each(lambda a, b: jnp.maximum(jnp.max(a, axis=-1, keepdims=True), b), s, sk)
    p = each(lambda a, b: jnp.exp(a - b), s, m)
    den = each(lambda a, b, d: jnp.sum(a, axis=-1, keepdims=True) + jnp.exp(b - d), p, sk, m)
    o = each(lambda a, b, d: _dot(a, b) * (1.0 / d), p, vh, den)
    outs = []
    for it in range(len(items)):
        tot = o[H_KVB * it]
        for hk in range(1, H_KVB):
            tot = tot + o[H_KVB * it + hk]
        outs.append(jnp.concatenate([tot[gi * q_rows:(gi + 1) * q_rows] for gi in range(GQA_GROUP)], axis=1))
    return outs


def _swa_prompt_kernel(q_ref, kc_ref, kp_ref, vc_ref, vp_ref, sink_ref, o_ref):
    n0 = pl.program_id(1) * SWA_QB
    keys = 2 * WINDOW
    k_ext = jnp.concatenate([kp_ref[...], kc_ref[...]], axis=0)
    v_ext = jnp.concatenate([vp_ref[...], vc_ref[...]], axis=0)
    c = lax.broadcasted_iota(jnp.int32, (keys, 2 * WINDOW), 0)
    r = lax.broadcasted_iota(jnp.int32, (keys, 2 * WINDOW), 1) & (WINDOW - 1)
    band = jnp.where((c > r) & (c <= r + WINDOW), 0.0, NEG)
    shift = int(math.log2(HD_B))
    lane_head = lax.broadcasted_iota(jnp.int32, (1, LANES), 1) >> shift
    row = lax.broadcasted_iota(jnp.int32, (LANES, 1), 0)
    row_head = row >> shift
    each = lambda f, *lists: [f(*a) for a in zip(*lists)]
    n_gp = GQA_GROUP // 2

    chains = [(j, hk, gp) for j in range(SWA_QB) for hk in range(H_KVB) for gp in range(n_gp)]
    bias = [jnp.where(c + n0 * WINDOW >= WINDOW, band, NEG)] + [band] * (SWA_QB - 1)
    kh, vth, qpair = {}, {}, {}
    for j in range(SWA_QB):
        k_all = k_ext[j * WINDOW:(j + 2) * WINDOW]
        vt = v_ext[j * WINDOW:(j + 2) * WINDOW].T
        qs = q_ref[j * WINDOW:(j + 1) * WINDOW, :] * (HD_B ** -0.5 * LOG2E)
        for hk in range(H_KVB):
            kh[j, hk] = jnp.where(lane_head == hk, k_all, 0.0).astype(BF16)
            vth[j, hk] = jnp.where(row_head == hk, vt, jnp.where(row == (1 - hk) * HD_B, 1.0, 0.0)).astype(BF16)
        for gp in range(n_gp):
            qpair[j, gp] = jnp.concatenate([qs[:, (2 * gp) * LANES:(2 * gp + 1) * LANES],
                                            qs[:, (2 * gp + 1) * LANES:(2 * gp + 2) * LANES]],
                                           axis=0).astype(BF16)
    ones_row = [(1 - hk) * HD_B for _, hk, _ in chains]
    sk = [jnp.concatenate([sink_ref[hk * GQA_GROUP + 2 * gp + i:hk * GQA_GROUP + 2 * gp + i + 1, :]
                           for i in range(2)], axis=1) * LOG2E for _, hk, gp in chains]
    st = [_dot_nt(kh[j, hk], qpair[j, gp]) + bias[j] for j, hk, gp in chains]
    m = each(lambda a, b: jnp.maximum(jnp.max(a, axis=0, keepdims=True), b), st, sk)
    pt = each(lambda a, b: jnp.exp2(a - b).astype(BF16), st, m)
    ot = [jnp.dot(vth[j, hk], pt[ch], preferred_element_type=F32) for ch, (j, hk, _) in enumerate(chains)]
    den = each(lambda a, i, b, d: a[i:i + 1, :] + jnp.exp2(b - d), ot, ones_row, sk, m)
    ot = [jnp.where(row_head == hk, ot[ch] * (1.0 / den[ch]), 0.0) for ch, (_, hk, _) in enumerate(chains)]
    for j in range(SWA_QB):
        tiles = []
        for gp in range(n_gp):
            tot = ot[chains.index((j, 0, gp))]
            for hk in range(1, H_KVB):
                tot = tot + ot[chains.index((j, hk, gp))]
            tiles += [tot[:, :WINDOW].T, tot[:, WINDOW:].T]
        o_ref[j * WINDOW:(j + 1) * WINDOW, :] = jnp.concatenate(tiles, axis=1)


def _swa_prompt(proj, n_batch, seq, sink_all, layer):
    nb = seq // WINDOW
    ns = nb // SWA_QB
    qrows = SWA_QB * WINDOW
    cur = lambda col: (lambda b, n: (b * ns + n, col))
    prev = lambda col: (lambda b, n: (b * nb + jnp.maximum(n * SWA_QB - 1, 0), col))
    return pl.pallas_call(
        _swa_prompt_kernel,
        name="swa_prompt",
        grid=(n_batch, ns),
        in_specs=[pl.BlockSpec((qrows, B_WIDTH), cur(COL_QB // B_WIDTH)),
                  pl.BlockSpec((qrows, KV_WIDTH), cur(COL_KB // KV_WIDTH)),
                  pl.BlockSpec((WINDOW, KV_WIDTH), prev(COL_KB // KV_WIDTH)),
                  pl.BlockSpec((qrows, KV_WIDTH), cur(COL_VB // KV_WIDTH)),
                  pl.BlockSpec((WINDOW, KV_WIDTH), prev(COL_VB // KV_WIDTH)),
                  pl.BlockSpec((None, H_QB, LANES), lambda b, n: (layer, 0, 0))],
        out_specs=pl.BlockSpec((qrows, B_WIDTH), lambda b, n: (b * ns + n, 0)),
        out_shape=jax.ShapeDtypeStruct((n_batch * seq, B_WIDTH), F32),
        compiler_params=_params(2),
    )(proj, proj, proj, proj, proj, sink_all)


def _shift_cache(cache, new_tile):
    rolled = pltpu.roll(cache, WINDOW - DEC_SEQ, axis=0)
    moved = pltpu.roll(new_tile, S_TILE - DEC_SEQ - ROW0, axis=0)
    row = lax.broadcasted_iota(jnp.int32, (S_TILE, 1), 0)
    tail = jnp.where(row >= S_TILE - DEC_SEQ, moved, rolled[WINDOW - S_TILE:])
    return jnp.concatenate([rolled[:WINDOW - S_TILE], tail], axis=0)


def _swa_sample_kernel(q_ref, k_ref, v_ref, kc_ref, vc_ref, sink_ref, *rest):
    o_ref, ko_ref, vo_ref = rest[-3:]
    shape = (GQA_GROUP * S_TILE, 2 * WINDOW)
    r = (lax.broadcasted_iota(jnp.int32, shape, 0) & (S_TILE - 1)) - ROW0
    c = lax.broadcasted_iota(jnp.int32, shape, 1)
    j = c - WINDOW - ROW0
    mask = ((c < WINDOW) & (c > r)) | ((j >= 0) & (j < DEC_SEQ) & (j <= r))
    pad = jnp.zeros((WINDOW - S_TILE, KV_WIDTH), F32)
    items = []
    for b in range(SB):
        rows = slice(b * S_TILE, (b + 1) * S_TILE)
        k_new = k_ref[rows, :]
        v_new = v_ref[rows, :]
        items.append((q_ref[rows, :], jnp.concatenate([kc_ref[b], k_new, pad], axis=0),
                      jnp.concatenate([vc_ref[b], v_new, pad], axis=0)))
        ko_ref[b] = _shift_cache(kc_ref[b], k_new)
        vo_ref[b] = _shift_cache(vc_ref[b], v_new)
    for b, o in enumerate(_swa_heads(items, mask, sink_ref, S_TILE)):
        o_ref[b * S_TILE:(b + 1) * S_TILE, :] = o


def _swa_sample(proj, kc_all, vc_all, sink_all, layer, k_acc, v_acc):
    rows = SB * S_TILE
    n_seq = proj.shape[0] // S_TILE
    depth = kc_all.shape[0]
    cache = lambda: pl.BlockSpec((None, SB, WINDOW, KV_WIDTH), lambda i: (layer, i, 0, 0))
    in_specs = [pl.BlockSpec((rows, B_WIDTH), lambda i: (i, COL_QB // B_WIDTH)),
                pl.BlockSpec((rows, KV_WIDTH), lambda i: (i, COL_KB // KV_WIDTH)),
                pl.BlockSpec((rows, KV_WIDTH), lambda i: (i, COL_VB // KV_WIDTH)),
                cache(), cache(),
                pl.BlockSpec((None, H_QB, LANES), lambda i: (layer, 0, 0))]
    in_specs += [pl.BlockSpec(memory_space=pl.ANY)] * 2
    args = [proj, proj, proj, kc_all, vc_all, sink_all, k_acc, v_acc]
    aliases = {len(args) - 2: 1, len(args) - 1: 2}
    return pl.pallas_call(
        _swa_sample_kernel,
        name="swa_sample",
        grid=(n_seq // SB,),
        in_specs=in_specs,
        out_specs=[pl.BlockSpec((rows, B_WIDTH), lambda i: (i, 0)), cache(), cache()],
        out_shape=[jax.ShapeDtypeStruct((n_seq * S_TILE, B_WIDTH), F32),
                   jax.ShapeDtypeStruct((depth, n_seq, WINDOW, KV_WIDTH), F32),
                   jax.ShapeDtypeStruct((depth, n_seq, WINDOW, KV_WIDTH), F32)],
        input_output_aliases=aliases,
        compiler_params=_params(1),
    )(*args)


def kernel(x_prompt, x_sample, state_delta, state_conv, cache_swa_k, cache_swa_v, w_in, conv_w, a_log,
           dt_bias, norm_a_w, sinks, w_out, ln1_g, ln1_b, w_ffn_in, w_ffn_out, ln2_g, ln2_b):
    depth = w_in.shape[0]
    n_batch, seq, _ = x_prompt.shape
    n_dec, dec_seq, _ = x_sample.shape
    assert dec_seq == DEC_SEQ and seq % SEG == 0 and n_dec % SB == 0 and n_batch % N_SEQ == 0
    assert cache_swa_k.shape[2] == WINDOW
    alpha = (2 * depth) ** 0.25

    c0 = CONV_DIM + A_WIDTH
    q0 = c0 + 2 * H_A
    order = jnp.array(QB_HEAD_ORDER)
    w_q = w_in[:, :, q0:q0 + B_WIDTH].reshape(depth, D_MODEL, H_QB, HD_B)[:, :, order]
    w_in_r = jnp.concatenate(
        [w_in[:, :, :c0], w_q.reshape(depth, D_MODEL, B_WIDTH), w_in[:, :, q0 + B_WIDTH:],
         w_in[:, :, c0:q0], jnp.zeros((depth, D_MODEL, LANES - 2 * H_A), w_in.dtype)], axis=-1).astype(BF16)
    w_out_bq = w_out[:, A_WIDTH:].reshape(depth, H_QB, HD_B, D_MODEL)[:, order].reshape(depth, B_WIDTH, D_MODEL)
    w_out_b = jnp.concatenate([w_out[:, :A_WIDTH], w_out_bq], axis=1).astype(BF16)
    w_ffn_in_b = w_ffn_in.astype(BF16)
    w_ffn_out_b = w_ffn_out.astype(BF16)
    lane_pad = lambda t: jnp.pad(t, ((0, 0), (H_A, LANES - 2 * H_A)))[:, None, :]
    alog_v = lane_pad(a_log)
    dtb_v = lane_pad(dt_bias)
    nw_v = norm_a_w[:, None, :]
    sink_v = jnp.broadcast_to(sinks[:, :, None], (depth, H_QB, LANES))
    ln1_g3, ln1_b3, ln2_g3, ln2_b3 = (t[:, None, :] for t in (ln1_g, ln1_b, ln2_g, ln2_b))

    hist = jnp.pad(state_conv, ((0, 0), (0, 0), (0, S_TILE - (CONV_WIDTH - 1)), (0, 0)))
    hist = hist.reshape(depth, n_dec * S_TILE, CONV_DIM)
    kc = cache_swa_k.reshape(depth, n_dec, WINDOW, KV_WIDTH)
    vc = cache_swa_v.reshape(depth, n_dec, WINDOW, KV_WIDTH)

    xp = x_prompt.reshape(n_batch * seq, D_MODEL)
    xs = jnp.pad(x_sample, ((0, 0), (ROW0, S_TILE - ROW0 - DEC_SEQ), (0, 0))).reshape(n_dec * S_TILE, D_MODEL)

    outs = [[] for _ in range(8)]
    s_acc = jnp.zeros((depth, n_dec, H_A, DK_A, DV_A), F32)
    k_acc = jnp.zeros((depth, n_dec, WINDOW, KV_WIDTH), F32)
    v_acc = jnp.zeros((depth, n_dec, WINDOW, KV_WIDTH), F32)
    for l in range(depth):
        proj = _in_proj(xp, w_in_r, l)
        oa, s_p, conv_p = _gdn_prompt(proj, n_batch, seq, conv_w, alog_v, dtb_v, nw_v, l)
        ob = _swa_prompt(proj, n_batch, seq, sink_v, l)
        xp = _mix_ffn(xp, oa, ob, w_out_b, ln1_g3, ln1_b3, w_ffn_in_b, w_ffn_out_b, ln2_g3, ln2_b3, l, alpha)
        proj3 = proj.reshape(n_batch, seq, PROJ_PAD)
        outs[0].append(s_p)
        outs[1].append(conv_p.reshape(n_batch, SUBLANES, CONV_DIM)[:, SUBLANES - (CONV_WIDTH - 1):])
        outs[2].append(proj3[:, seq - WINDOW:, COL_KB:COL_KB + KV_WIDTH].reshape(n_batch, WINDOW, H_KVB, HD_B))
        outs[3].append(proj3[:, seq - WINDOW:, COL_VB:COL_VB + KV_WIDTH].reshape(n_batch, WINDOW, H_KVB, HD_B))
        proj = _in_proj(xs, w_in_r, l)
        oa, s_acc, conv_s = _gdn_sample(proj, hist, state_delta, conv_w, alog_v, dtb_v, nw_v, l, s_acc)
        ob, k_acc, v_acc = _swa_sample(proj, kc, vc, sink_v, l, k_acc, v_acc)
        xs = _mix_ffn(xs, oa, ob, w_out_b, ln1_g3, ln1_b3, w_ffn_in_b, w_ffn_out_b, ln2_g3, ln2_b3, l, alpha)
        outs[5].append(conv_s.reshape(n_dec, S_TILE, CONV_DIM)[:, :CONV_WIDTH - 1])

    y_prompt = xp.reshape(n_batch, seq, D_MODEL)
    y_sample = xs.reshape(n_dec, S_TILE, D_MODEL)[:, ROW0:ROW0 + DEC_SEQ]
    stacked = [jnp.stack(o) if o else None for o in outs]
    stacked[4] = s_acc
    stacked[6] = k_acc.reshape(depth, n_dec, WINDOW, H_KVB, HD_B)
    stacked[7] = v_acc.reshape(depth, n_dec, WINDOW, H_KVB, HD_B)
    return (y_prompt, y_sample) + tuple(stacked)
```

```python
import functools
import math
from typing import NamedTuple

import jax
import jax.numpy as jnp
from jax import lax
from jax.experimental import pallas as pl
from jax.experimental.pallas import tpu as pltpu

F32 = jnp.float32
BF16 = jnp.bfloat16

D_MODEL = 1024
H_A = 4
DK_A = 128
DV_A = 128
CONV_WIDTH = 4
CONV_DIM = 2 * H_A * DK_A + H_A * DV_A
A_WIDTH = H_A * DV_A
CHUNK = 64
HD_B = 64
H_QB = 8
H_KVB = 2
GQA_GROUP = H_QB // H_KVB
B_WIDTH = H_QB * HD_B
KV_WIDTH = H_KVB * HD_B
WINDOW = 128
D_FF = 2816
EPS = 1e-6

LANES = 128
SUBLANES = 8

COL_Z = CONV_DIM
COL_QB = COL_Z + A_WIDTH
COL_KB = COL_QB + B_WIDTH
COL_VB = COL_KB + KV_WIDTH
COL_BA = COL_VB + KV_WIDTH
PROJ_PAD = COL_BA + LANES
QB_HEAD_ORDER = tuple(hk * GQA_GROUP + g for g in range(GQA_GROUP) for hk in range(H_KVB))

S_TILE = SUBLANES
ROW0 = CONV_WIDTH - 1
DEC_SEQ = 4
SB = 16

TM = 512
SEG = 128
N_CHUNK = SEG // CHUNK
PREP_CHUNKS = 1
N_SEQ = 8
FF_CHUNK = 256
SWA_QB = 4
NEG = -1e30
LOG2E = math.log2(math.e)
VMEM_LIMIT = 56 * 1024 * 1024


def _params(n_axes, vmem=None):
    return pltpu.CompilerParams(
        dimension_semantics=("arbitrary",) * n_axes,
        vmem_limit_bytes=vmem if vmem is not None else VMEM_LIMIT)


def _dot(a, b):
    return jnp.dot(a.astype(BF16), b.astype(BF16), preferred_element_type=F32)


def _dot_nt(a, b):
    return lax.dot_general(a.astype(BF16), b.astype(BF16), (((1,), (1,)), ((), ())),
                           preferred_element_type=F32)


def _sigmoid(x):
    return 1.0 / (1.0 + jnp.exp(-x))


def _silu(x):
    h = 0.5 * x
    return h + h * jnp.tanh(h)


def _softplus(x):
    return jnp.maximum(x, 0.0) + jnp.log(1.0 + jnp.exp(-jnp.abs(x)))


def _layernorm(h, g, b):
    mu = jnp.mean(h, axis=-1, keepdims=True)
    d = h - mu
    var = jnp.mean(d * d, axis=-1, keepdims=True)
    return d * lax.rsqrt(var + EPS) * g + b


def _two_stream_specs(shape_cols, n_p):
    return (pl.BlockSpec((TM, shape_cols), lambda i: (jnp.minimum(i, n_p - 1), 0)),
            pl.BlockSpec((TM, shape_cols), lambda i: (jnp.maximum(i - n_p, 0), 0)))


def _per_stream(n_p, body, prompt_refs, sample_refs):
    i = pl.program_id(0)

    @pl.when(i < n_p)
    def _():
        body(*prompt_refs)

    @pl.when(i >= n_p)
    def _():
        body(*sample_refs)


def _in_proj_kernel(n_p, xp_ref, xs_ref, w_ref, op_ref, os_ref):
    def body(x_ref, o_ref):
        xb = x_ref[...].astype(BF16)
        n = o_ref.shape[1]
        for n0 in range(0, n, 512):
            n1 = min(n0 + 512, n)
            o_ref[:, n0:n1] = jnp.dot(xb, w_ref[:, n0:n1], preferred_element_type=F32)

    _per_stream(n_p, body, (xp_ref, op_ref), (xs_ref, os_ref))


def _in_proj(xp, xs, w_all, layer):
    n_p, n_s = xp.shape[0] // TM, xs.shape[0] // TM
    return pl.pallas_call(
        functools.partial(_in_proj_kernel, n_p),
        name="in_proj",
        grid=(n_p + n_s,),
        in_specs=[*_two_stream_specs(D_MODEL, n_p),
                  pl.BlockSpec((None, D_MODEL, PROJ_PAD), lambda i: (layer, 0, 0))],
        out_specs=list(_two_stream_specs(PROJ_PAD, n_p)),
        out_shape=[jax.ShapeDtypeStruct((xp.shape[0], PROJ_PAD), F32),
                   jax.ShapeDtypeStruct((xs.shape[0], PROJ_PAD), F32)],
        compiler_params=_params(1),
    )(xp, xs, w_all)


def _mix_ffn_kernel(alpha, n_p, xp_ref, xs_ref, oap_ref, oas_ref, obp_ref, obs_ref, wm_ref, g1_ref, b1_ref,
                    wi_ref, wo_ref, g2_ref, b2_ref, yp_ref, ys_ref):
    def body(x_ref, oa_ref, ob_ref, o_ref):
        m = (jnp.dot(oa_ref[...].astype(BF16), wm_ref[0:A_WIDTH, :], preferred_element_type=F32)
             + jnp.dot(ob_ref[...].astype(BF16), wm_ref[A_WIDTH:, :], preferred_element_type=F32))
        x1 = _layernorm(alpha * x_ref[...] + m, g1_ref[...], b1_ref[...])
        xb = x1.astype(BF16)
        acc = alpha * x1
        for c0 in range(0, D_FF, FF_CHUNK):
            gate = jnp.dot(xb, wi_ref[:, c0:c0 + FF_CHUNK], preferred_element_type=F32)
            up = jnp.dot(xb, wi_ref[:, D_FF + c0:D_FF + c0 + FF_CHUNK], preferred_element_type=F32)
            h = _silu(gate) * up
            acc = acc + jnp.dot(h.astype(BF16), wo_ref[c0:c0 + FF_CHUNK, :], preferred_element_type=F32)
        o_ref[...] = _layernorm(acc, g2_ref[...], b2_ref[...])

    _per_stream(n_p, body, (xp_ref, oap_ref, obp_ref, yp_ref), (xs_ref, oas_ref, obs_ref, ys_ref))


def _mix_ffn(xp, xs, oap, oas, obp, obs, wm_all, g1_all, b1_all, wi_all, wo_all, g2_all, b2_all, layer, alpha):
    n_p, n_s = xp.shape[0] // TM, xs.shape[0] // TM
    vec = lambda: pl.BlockSpec((None, 1, D_MODEL), lambda i: (layer, 0, 0))
    weight = lambda rows, cols: pl.BlockSpec((None, rows, cols), lambda i: (layer, 0, 0),
                                             pipeline_mode=pl.Buffered(1))
    return pl.pallas_call(
        functools.partial(_mix_ffn_kernel, alpha, n_p),
        name="mix_ffn",
        grid=(n_p + n_s,),
        in_specs=[*_two_stream_specs(D_MODEL, n_p), *_two_stream_specs(A_WIDTH, n_p),
                  *_two_stream_specs(B_WIDTH, n_p),
                  weight(D_MODEL, D_MODEL), vec(), vec(),
                  weight(D_MODEL, 2 * D_FF), weight(D_FF, D_MODEL), vec(), vec()],
        out_specs=list(_two_stream_specs(D_MODEL, n_p)),
        out_shape=[jax.ShapeDtypeStruct(xp.shape, F32), jax.ShapeDtypeStruct(xs.shape, F32)],
        compiler_params=_params(1),
    )(xp, xs, oap, oas, obp, obs, wm_all, g1_all, b1_all, wi_all, wo_all, g2_all, b2_all)


def _conv_silu(x, cw):
    y = cw[3:4, :] * x
    for j in range(1, CONV_WIDTH):
        y = y + cw[CONV_WIDTH - 1 - j:CONV_WIDTH - j, :] * pltpu.roll(x, j, axis=0)
    return _silu(y)


def _gates(ba, alog, dtb):
    beta = _sigmoid(ba)
    g = -jnp.exp(alog) * _softplus(ba + dtb)
    return beta, g


def _cumsum_rows(g, c):
    rin = lax.broadcasted_iota(jnp.int32, g.shape, 0) & (c - 1)
    s = 1
    while s < c:
        g = g + jnp.where(rin >= s, pltpu.roll(g, s, axis=0), 0.0)
        s *= 2
    return g


class _Masks(NamedTuple):
    causal: jax.Array
    neg_strict: jax.Array
    eye: jax.Array
    unit: tuple


def _group_masks(c):
    r = LANES
    shift = int(math.log2(c))
    ri = lax.broadcasted_iota(jnp.int32, (r, r), 0)
    ci = lax.broadcasted_iota(jnp.int32, (r, r), 1)
    same = (ri >> shift) == (ci >> shift)
    return _Masks(
        causal=jnp.where(same & (ri >= ci), 1.0, 0.0),
        neg_strict=jnp.where(same & (ri > ci), -1.0, 0.0),
        eye=jnp.where(ri == ci, 1.0, 0.0),
        unit=tuple(jnp.where((ci >> shift) == i, 1.0, 0.0) for i in range(r // c)))


def _gdn_prepare(groups, c, mk):
    r = LANES
    n_units = r // c
    n_factors = int(math.log2(c))
    each = lambda f, *lists: [f(*a) for a in zip(*lists)]

    q, k, v, beta, gcol = (list(t) for t in zip(*groups))
    qn = each(lambda t: t * lax.rsqrt(jnp.sum(t * t, axis=-1, keepdims=True) + EPS) * (DK_A ** -0.5), q)
    kn = each(lambda t: t * lax.rsqrt(jnp.sum(t * t, axis=-1, keepdims=True) + EPS), k)
    gc = each(lambda t: jnp.broadcast_to(t, (r, r)), gcol)
    e = each(lambda t: jnp.exp(jnp.minimum(t - t.T, 0.0)), gc)
    kb = each(lambda a, b: a * b, kn, beta)
    kq = each(lambda a, b, d: _dot_nt(jnp.concatenate([a, b], axis=0), d), kb, qn, kn)
    bk = each(lambda a, b: a[:r] * (b * mk.neg_strict), kq, e)
    aqk = each(lambda a, b: a[r:] * (b * mk.causal), kq, e)

    p = each(lambda t: mk.eye + t, bk)
    bk = each(lambda t: _dot(t, t), bk)
    for _ in range(n_factors - 2):
        st = each(lambda a, b: _dot(jnp.concatenate([a, b], axis=0), b), p, bk)
        p = each(lambda a, b: a + b[:r], p, st)
        bk = each(lambda t: t[r:], st)
    p = each(lambda a, b: a + _dot(a, b), p, bk)

    eg = each(jnp.exp, gc)
    uw = each(lambda a, b, d, f, h: _dot(a, jnp.concatenate([b * d, f * h], axis=1)), p, v, beta, kb, eg)
    qd = each(lambda a, b: a * b, qn, eg)

    out = []
    for gi in range(len(groups)):
        g_last = [gc[gi][(i + 1) * c - 1:(i + 1) * c, :] for i in range(n_units)]
        gl = jnp.concatenate([jnp.broadcast_to(t, (c, r)) for t in g_last], axis=0)
        kdt = (kn[gi] * jnp.exp(gl - gc[gi])).T
        u = uw[gi][:, :DV_A]
        w = uw[gi][:, DV_A:]
        wq = jnp.concatenate([jnp.concatenate([w[i * c:(i + 1) * c], qd[gi][i * c:(i + 1) * c]], axis=0)
                              for i in range(n_units)], axis=0).astype(BF16)
        lhs = jnp.concatenate([aqk[gi]] + [kdt * mk.unit[i] for i in range(n_units)], axis=0).astype(BF16)
        out.append((u, wq, lhs, [jnp.exp(t) for t in g_last]))
    return out


def _gdn_apply(items, nw, c):
    r = LANES
    n_units = r // c
    res1 = [[jnp.dot(wq[2 * c * i:2 * c * (i + 1)], st[i].astype(BF16), preferred_element_type=F32)
             for i in range(n_units)] for _, wq, _, _, st, _ in items]
    vn = [jnp.concatenate([it[0][i * c:(i + 1) * c] - r1[i][:c] for i in range(n_units)], axis=0)
          for it, r1 in zip(items, res1)]
    res2 = [jnp.dot(it[2], v.astype(BF16), preferred_element_type=F32) for it, v in zip(items, vn)]
    out = []
    for (_, _, _, egl, st, z), r1, r2 in zip(items, res1, res2):
        o = jnp.concatenate([r1[i][c:] for i in range(n_units)], axis=0) + r2[:r]
        s_new = [st[i] * egl[i] + r2[r * (i + 1):r * (i + 2)] for i in range(n_units)]
        out.append((o * lax.rsqrt(jnp.mean(o * o, axis=-1, keepdims=True) + EPS) * nw * _silu(z), s_new))
    return out


def _gdn_prompt_kernel(n_seg, prev_ref, qkv_ref, z_ref, ba_ref, cw_ref, alog_ref, dtb_ref, nw_ref,
                       o_ref, s_ref, conv_ref, s_scr, u_scr, wq_scr, lhs_scr, egl_scr):
    seg = pl.program_id(1)
    n_pairs = H_A // 2

    @pl.when(seg == 0)
    def _():
        s_scr[...] = jnp.zeros_like(s_scr)

    cw = cw_ref[...]
    alog = alog_ref[...]
    dtb = dtb_ref[...]
    nw = nw_ref[...]
    mk = _group_masks(CHUNK)

    def prepare(it, carry):
        groups, where = [], []
        for j in range(PREP_CHUNKS):
            ci = it * PREP_CHUNKS + j
            r0 = pl.multiple_of(ci * CHUNK, CHUNK)
            for q in range(N_SEQ):
                before = jnp.where(ci > 0, qkv_ref[q, pl.ds(pl.multiple_of(jnp.maximum(r0 - SUBLANES, 0), SUBLANES),
                                                            SUBLANES), :],
                                   jnp.where(seg > 0, prev_ref[q], 0.0))
                y = _conv_silu(jnp.concatenate([before, qkv_ref[q, pl.ds(r0, CHUNK), :]], axis=0), cw)[SUBLANES:]
                beta_all, g_all = _gates(ba_ref[q, pl.ds(r0, CHUNK), :], alog, dtb)
                gc_all = _cumsum_rows(g_all, CHUNK)
                for pair in range(n_pairs):
                    heads = (2 * pair, 2 * pair + 1)
                    cols = lambda base: jnp.concatenate(
                        [y[:, base + h * LANES:base + (h + 1) * LANES] for h in heads], axis=0)
                    beta = jnp.concatenate([beta_all[:, h:h + 1] for h in heads], axis=0)
                    gcol = jnp.concatenate([gc_all[:, H_A + h:H_A + h + 1] for h in heads], axis=0)
                    groups.append((cols(0), cols(H_A * DK_A), cols(2 * H_A * DK_A), beta, gcol))
                    where.append((ci, q, pair))
        for (ci, q, pair), (u, wq, lhs, egl) in zip(where, _gdn_prepare(groups, CHUNK, mk)):
            u_scr[ci, q, pair] = u
            wq_scr[ci, q, pair] = wq
            lhs_scr[ci, q, pair] = lhs
            for i in range(2):
                egl_scr[ci, q, pair, i:i + 1, :] = egl[i]
        return carry

    lax.fori_loop(0, N_CHUNK // PREP_CHUNKS, prepare, 0)

    def scan(ci, carry):
        r0 = pl.multiple_of(ci * CHUNK, CHUNK)
        items, where = [], []
        for q in range(N_SEQ):
            z = z_ref[q, pl.ds(r0, CHUNK), :]
            for pair in range(n_pairs):
                heads = (2 * pair, 2 * pair + 1)
                zz = jnp.concatenate([z[:, h * DV_A:(h + 1) * DV_A] for h in heads], axis=0)
                egl = [egl_scr[ci, q, pair, i:i + 1, :] for i in range(2)]
                items.append((u_scr[ci, q, pair], wq_scr[ci, q, pair], lhs_scr[ci, q, pair], egl,
                              [s_scr[q, h] for h in heads], zz))
                where.append((q, heads))
        for (q, heads), (on, s_new) in zip(where, _gdn_apply(items, nw, CHUNK)):
            for i, h in enumerate(heads):
                o_ref[q, pl.ds(r0, CHUNK), h * DV_A:(h + 1) * DV_A] = on[i * CHUNK:(i + 1) * CHUNK]
                s_scr[q, h] = s_new[i]
        return carry

    lax.fori_loop(0, N_CHUNK, scan, 0)

    @pl.when(seg == n_seg - 1)
    def _():
        s_ref[:, 0] = s_scr[...]
        conv_ref[:, 0] = qkv_ref[:, SEG - SUBLANES:SEG, :]


def _gdn_prompt(proj, n_batch, seq, cw_all, alog_all, dtb_all, nw_all, layer):
    n_seg = seq // SEG
    seg8 = SEG // SUBLANES
    seq8 = seq // SUBLANES
    n_pairs = H_A // 2
    nb = n_batch // N_SEQ
    proj3 = proj.reshape(N_SEQ, nb * seq, PROJ_PAD)
    blk = lambda width, col: pl.BlockSpec((N_SEQ, SEG, width), lambda bb, s: (0, bb * n_seg + s, col))
    vec = lambda: pl.BlockSpec((None, 1, LANES), lambda bb, s: (layer, 0, 0))
    o, s_out, conv = pl.pallas_call(
        functools.partial(_gdn_prompt_kernel, n_seg),
        name="gdn_prompt",
        grid=(nb, n_seg),
        in_specs=[pl.BlockSpec((N_SEQ, SUBLANES, CONV_DIM),
                               lambda bb, s: (0, jnp.maximum(bb * seq8 + s * seg8 - 1, 0), 0)),
                  blk(CONV_DIM, 0), blk(A_WIDTH, COL_Z // A_WIDTH), blk(LANES, COL_BA // LANES),
                  pl.BlockSpec((None, CONV_WIDTH, CONV_DIM), lambda bb, s: (layer, 0, 0)),
                  vec(), vec(), vec()],
        out_specs=[blk(A_WIDTH, 0),
                   pl.BlockSpec((N_SEQ, 1, H_A, DK_A, DV_A), lambda bb, s: (0, bb, 0, 0, 0)),
                   pl.BlockSpec((N_SEQ, 1, SUBLANES, CONV_DIM), lambda bb, s: (0, bb, 0, 0))],
        out_shape=[jax.ShapeDtypeStruct((N_SEQ, nb * seq, A_WIDTH), F32),
                   jax.ShapeDtypeStruct((N_SEQ, nb, H_A, DK_A, DV_A), F32),
                   jax.ShapeDtypeStruct((N_SEQ, nb, SUBLANES, CONV_DIM), F32)],
        scratch_shapes=[pltpu.VMEM((N_SEQ, H_A, DK_A, DV_A), F32),
                        pltpu.VMEM((N_CHUNK, N_SEQ, n_pairs, LANES, DV_A), F32),
                        pltpu.VMEM((N_CHUNK, N_SEQ, n_pairs, 2 * LANES, DK_A), BF16),
                        pltpu.VMEM((N_CHUNK, N_SEQ, n_pairs, 3 * LANES, LANES), BF16),
                        pltpu.VMEM((N_CHUNK, N_SEQ, n_pairs, SUBLANES, LANES), F32)],
        compiler_params=_params(2),
    )(proj3, proj3, proj3, proj3, cw_all, alog_all, dtb_all, nw_all)
    return (o.reshape(n_batch * seq, A_WIDTH), s_out.reshape(n_batch, H_A, DK_A, DV_A),
            conv.reshape(n_batch * SUBLANES, CONV_DIM))


def _token_rows(n_rows):
    row = lax.broadcasted_iota(jnp.int32, (n_rows, 1), 0) & (S_TILE - 1)
    return (row >= ROW0) & (row < ROW0 + DEC_SEQ)


def _gdn_sample_kernel(qkv_ref, hist_ref, z_ref, ba_ref, cw_ref, alog_ref, dtb_ref, nw_ref, s_in_ref, *rest):
    o_ref, s_out_ref, conv_ref = rest[-3:]
    rows = SB * S_TILE
    valid = _token_rows(rows)
    x = jnp.where(valid, qkv_ref[...], 0.0) + hist_ref[...]
    conv_ref[...] = pltpu.roll(x, rows - DEC_SEQ, axis=0)
    y = _conv_silu(x, cw_ref[...])
    beta_all, g_all = _gates(ba_ref[...], alog_ref[...], dtb_ref[...])
    beta_all = jnp.where(valid, beta_all, 0.0)
    gc_all = _cumsum_rows(jnp.where(valid, g_all, 0.0), S_TILE)
    z = z_ref[...]
    ym = jnp.where(valid, y, 0.0)

    all_units = [(b, h) for b in range(SB) for h in range(H_A)]
    per_group = LANES // S_TILE
    unit_groups = [all_units[i:i + per_group] for i in range(0, len(all_units), per_group)]

    def stack(units, src, base, width=LANES):
        return jnp.concatenate(
            [src[b * S_TILE:(b + 1) * S_TILE, base + h * width:base + (h + 1) * width] for b, h in units], axis=0)

    mk = _group_masks(S_TILE)
    prepared = _gdn_prepare(
        [(stack(us, y, 0), stack(us, ym, H_A * DK_A), stack(us, ym, 2 * H_A * DK_A),
          stack(us, beta_all, 0, 1), stack(us, gc_all, H_A, 1)) for us in unit_groups], S_TILE, mk)
    items = [(u, wq, lhs, egl, [s_in_ref[b, h] for b, h in us], stack(us, z, 0))
             for us, (u, wq, lhs, egl) in zip(unit_groups, prepared)]
    for us, (on, s_new) in zip(unit_groups, _gdn_apply(items, nw_ref[...], S_TILE)):
        for i, (b, h) in enumerate(us):
            o_ref[b * S_TILE:(b + 1) * S_TILE, h * DV_A:(h + 1) * DV_A] = on[i * S_TILE:(i + 1) * S_TILE]
            s_out_ref[b, h] = s_new[i]


def _gdn_sample(proj, hist, state_all, cw_all, alog_all, dtb_all, nw_all, layer, s_acc):
    rows = SB * S_TILE
    n_seq = proj.shape[0] // S_TILE
    depth = state_all.shape[0]
    vec = lambda: pl.BlockSpec((None, 1, LANES), lambda i: (layer, 0, 0))
    in_specs = [pl.BlockSpec((rows, CONV_DIM), lambda i: (i, 0)),
                pl.BlockSpec((None, rows, CONV_DIM), lambda i: (layer, i, 0)),
                pl.BlockSpec((rows, A_WIDTH), lambda i: (i, COL_Z // A_WIDTH)),
                pl.BlockSpec((rows, LANES), lambda i: (i, COL_BA // LANES)),
                pl.BlockSpec((None, CONV_WIDTH, CONV_DIM), lambda i: (layer, 0, 0)),
                vec(), vec(), vec(),
                pl.BlockSpec((None, SB, H_A, DK_A, DV_A), lambda i: (layer, i, 0, 0, 0))]
    in_specs.append(pl.BlockSpec(memory_space=pl.ANY))
    args = [proj, hist, proj, proj, cw_all, alog_all, dtb_all, nw_all, state_all, s_acc]
    aliases = {len(args) - 1: 1}
    return pl.pallas_call(
        _gdn_sample_kernel,
        name="gdn_sample",
        grid=(n_seq // SB,),
        in_specs=in_specs,
        out_specs=[pl.BlockSpec((rows, A_WIDTH), lambda i: (i, 0)),
                   pl.BlockSpec((None, SB, H_A, DK_A, DV_A), lambda i: (layer, i, 0, 0, 0)),
                   pl.BlockSpec((rows, CONV_DIM), lambda i: (i, 0))],
        out_shape=[jax.ShapeDtypeStruct((n_seq * S_TILE, A_WIDTH), F32),
                   jax.ShapeDtypeStruct((depth, n_seq, H_A, DK_A, DV_A), F32),
                   jax.ShapeDtypeStruct((n_seq * S_TILE, CONV_DIM), F32)],
        input_output_aliases=aliases,
        compiler_params=_params(1),
    )(*args)


def _swa_heads(items, mask, sink_ref, q_rows):
    lane_head = lax.broadcasted_iota(jnp.int32, (1, LANES), 1) >> int(math.log2(HD_B))
    chains = [(it, hk) for it in range(len(items)) for hk in range(H_KVB)]
    each = lambda f, *lists: [f(*a) for a in zip(*lists)]
    qcat = [jnp.concatenate([q[:, gi * LANES:(gi + 1) * LANES] for gi in range(GQA_GROUP)], axis=0)
            * (HD_B ** -0.5) for q, _, _ in items]
    sk_head = [jnp.concatenate(
        [jnp.broadcast_to(sink_ref[hk * GQA_GROUP + gi:hk * GQA_GROUP + gi + 1, 0:1], (q_rows, 1))
         for gi in range(GQA_GROUP)], axis=0) for hk in range(H_KVB)]
    sk = [sk_head[hk] for _, hk in chains]
    kh = [jnp.where(lane_head == hk, items[it][1], 0.0) for it, hk in chains]
    vh = [jnp.where(lane_head == hk, items[it][2], 0.0) for it, hk in chains]
    s = [jnp.where(mask, _dot_nt(qcat[it], kh[ch]), NEG) for ch, (it, _) in enumerate(chains)]
    m = each(lambda a, b: jnp.maximum(jnp.max(a, axis=-1, keepdims=True), b), s, sk)
    p = each(lambda a, b: jnp.exp(a - b), s, m)
    den = each(lambda a, b, d: jnp.sum(a, axis=-1, keepdims=True) + jnp.exp(b - d), p, sk, m)
    o = each(lambda a, b, d: _dot(a, b) * (1.0 / d), p, vh, den)
    outs = []
    for it in range(len(items)):
        tot = o[H_KVB * it]
        for hk in range(1, H_KVB):
            tot = tot + o[H_KVB * it + hk]
        outs.append(jnp.concatenate([tot[gi * q_rows:(gi + 1) * q_rows] for gi in range(GQA_GROUP)], axis=1))
    return outs


def _swa_prompt_kernel(q_ref, kc_ref, kp_ref, vc_ref, vp_ref, sink_ref, o_ref):
    n0 = pl.program_id(1) * SWA_QB
    keys = 2 * WINDOW
    k_ext = jnp.concatenate([kp_ref[...], kc_ref[...]], axis=0)
    v_ext = jnp.concatenate([vp_ref[...], vc_ref[...]], axis=0)
    c = lax.broadcasted_iota(jnp.int32, (keys, 2 * WINDOW), 0)
    r = lax.broadcasted_iota(jnp.int32, (keys, 2 * WINDOW), 1) & (WINDOW - 1)
    band = jnp.where((c > r) & (c <= r + WINDOW), 0.0, NEG)
    shift = int(math.log2(HD_B))
    lane_head = lax.broadcasted_iota(jnp.int32, (1, LANES), 1) >> shift
    row = lax.broadcasted_iota(jnp.int32, (LANES, 1), 0)
    row_head = row >> shift
    each = lambda f, *lists: [f(*a) for a in zip(*lists)]
    n_gp = GQA_GROUP // 2

    chains = [(j, hk, gp) for j in range(SWA_QB) for hk in range(H_KVB) for gp in range(n_gp)]
    bias = [jnp.where(c + n0 * WINDOW >= WINDOW, band, NEG)] + [band] * (SWA_QB - 1)
    kh, vth, qpair = {}, {}, {}
    for j in range(SWA_QB):
        k_all = k_ext[j * WINDOW:(j + 2) * WINDOW]
        vt = v_ext[j * WINDOW:(j + 2) * WINDOW].T
        qs = q_ref[j * WINDOW:(j + 1) * WINDOW, :] * (HD_B ** -0.5 * LOG2E)
        for hk in range(H_KVB):
            kh[j, hk] = jnp.where(lane_head == hk, k_all, 0.0).astype(BF16)
            vth[j, hk] = jnp.where(row_head == hk, vt, jnp.where(row == (1 - hk) * HD_B, 1.0, 0.0)).astype(BF16)
        for gp in range(n_gp):
            qpair[j, gp] = jnp.concatenate([qs[:, (2 * gp) * LANES:(2 * gp + 1) * LANES],
                                            qs[:, (2 * gp + 1) * LANES:(2 * gp + 2) * LANES]],
                                           axis=0).astype(BF16)
    ones_row = [(1 - hk) * HD_B for _, hk, _ in chains]
    sk = [jnp.concatenate([sink_ref[hk * GQA_GROUP + 2 * gp + i:hk * GQA_GROUP + 2 * gp + i + 1, :]
                           for i in range(2)], axis=1) * LOG2E for _, hk, gp in chains]
    st = [_dot_nt(kh[j, hk], qpair[j, gp]) + bias[j] for j, hk, gp in chains]
    m = each(lambda a, b: jnp.maximum(jnp.max(a, axis=0, keepdims=True), b), st, sk)
    pt = each(lambda a, b: jnp.exp2(a - b).astype(BF16), st, m)
    ot = [jnp.dot(vth[j, hk], pt[ch], preferred_element_type=F32) for ch, (j, hk, _) in enumerate(chains)]
    den = each(lambda a, i, b, d: a[i:i + 1, :] + jnp.exp2(b - d), ot, ones_row, sk, m)
    ot = [jnp.where(row_head == hk, ot[ch] * (1.0 / den[ch]), 0.0) for ch, (_, hk, _) in enumerate(chains)]
    for j in range(SWA_QB):
        tiles = []
        for gp in range(n_gp):
            tot = ot[chains.index((j, 0, gp))]
            for hk in range(1, H_KVB):
                tot = tot + ot[chains.index((j, hk, gp))]
            tiles += [tot[:, :WINDOW].T, tot[:, WINDOW:].T]
        o_ref[j * WINDOW:(j + 1) * WINDOW, :] = jnp.concatenate(tiles, axis=1)


def _swa_prompt(proj, n_batch, seq, sink_all, layer):
    nb = seq // WINDOW
    ns = nb // SWA_QB
    qrows = SWA_QB * WINDOW
    cur = lambda col: (lambda b, n: (b * ns + n, col))
    prev = lambda col: (lambda b, n: (b * nb + jnp.maximum(n * SWA_QB - 1, 0), col))
    return pl.pallas_call(
        _swa_prompt_kernel,
        name="swa_prompt",
        grid=(n_batch, ns),
        in_specs=[pl.BlockSpec((qrows, B_WIDTH), cur(COL_QB // B_WIDTH)),
                  pl.BlockSpec((qrows, KV_WIDTH), cur(COL_KB // KV_WIDTH)),
                  pl.BlockSpec((WINDOW, KV_WIDTH), prev(COL_KB // KV_WIDTH)),
                  pl.BlockSpec((qrows, KV_WIDTH), cur(COL_VB // KV_WIDTH)),
                  pl.BlockSpec((WINDOW, KV_WIDTH), prev(COL_VB // KV_WIDTH)),
                  pl.BlockSpec((None, H_QB, LANES), lambda b, n: (layer, 0, 0))],
        out_specs=pl.BlockSpec((qrows, B_WIDTH), lambda b, n: (b * ns + n, 0)),
        out_shape=jax.ShapeDtypeStruct((n_batch * seq, B_WIDTH), F32),
        compiler_params=_params(2),
    )(proj, proj, proj, proj, proj, sink_all)


def _shift_cache(cache, new_tile):
    rolled = pltpu.roll(cache, WINDOW - DEC_SEQ, axis=0)
    moved = pltpu.roll(new_tile, S_TILE - DEC_SEQ - ROW0, axis=0)
    row = lax.broadcasted_iota(jnp.int32, (S_TILE, 1), 0)
    tail = jnp.where(row >= S_TILE - DEC_SEQ, moved, rolled[WINDOW - S_TILE:])
    return jnp.concatenate([rolled[:WINDOW - S_TILE], tail], axis=0)


def _swa_sample_kernel(q_ref, k_ref, v_ref, kc_ref, vc_ref, sink_ref, *rest):
    o_ref, ko_ref, vo_ref = rest[-3:]
    shape = (GQA_GROUP * S_TILE, 2 * WINDOW)
    r = (lax.broadcasted_iota(jnp.int32, shape, 0) & (S_TILE - 1)) - ROW0
    c = lax.broadcasted_iota(jnp.int32, shape, 1)
    j = c - WINDOW - ROW0
    mask = ((c < WINDOW) & (c > r)) | ((j >= 0) & (j < DEC_SEQ) & (j <= r))
    pad = jnp.zeros((WINDOW - S_TILE, KV_WIDTH), F32)
    items = []
    for b in range(SB):
        rows = slice(b * S_TILE, (b + 1) * S_TILE)
        k_new = k_ref[rows, :]
        v_new = v_ref[rows, :]
        items.append((q_ref[rows, :], jnp.concatenate([kc_ref[b], k_new, pad], axis=0),
                      jnp.concatenate([vc_ref[b], v_new, pad], axis=0)))
        ko_ref[b] = _shift_cache(kc_ref[b], k_new)
        vo_ref[b] = _shift_cache(vc_ref[b], v_new)
    for b, o in enumerate(_swa_heads(items, mask, sink_ref, S_TILE)):
        o_ref[b * S_TILE:(b + 1) * S_TILE, :] = o


def _swa_sample(proj, kc_all, vc_all, sink_all, layer, k_acc, v_acc):
    rows = SB * S_TILE
    n_seq = proj.shape[0] // S_TILE
    depth = kc_all.shape[0]
    cache = lambda: pl.BlockSpec((None, SB, WINDOW, KV_WIDTH), lambda i: (layer, i, 0, 0))
    in_specs = [pl.BlockSpec((rows, B_WIDTH), lambda i: (i, COL_QB // B_WIDTH)),
                pl.BlockSpec((rows, KV_WIDTH), lambda i: (i, COL_KB // KV_WIDTH)),
                pl.BlockSpec((rows, KV_WIDTH), lambda i: (i, COL_VB // KV_WIDTH)),
                cache(), cache(),
                pl.BlockSpec((None, H_QB, LANES), lambda i: (layer, 0, 0))]
    in_specs += [pl.BlockSpec(memory_space=pl.ANY)] * 2
    args = [proj, proj, proj, kc_all, vc_all, sink_all, k_acc, v_acc]
    aliases = {len(args) - 2: 1, len(args) - 1: 2}
    return pl.pallas_call(
        _swa_sample_kernel,
        name="swa_sample",
        grid=(n_seq // SB,),
        in_specs=in_specs,
        out_specs=[pl.BlockSpec((rows, B_WIDTH), lambda i: (i, 0)), cache(), cache()],
        out_shape=[jax.ShapeDtypeStruct((n_seq * S_TILE, B_WIDTH), F32),
                   jax.ShapeDtypeStruct((depth, n_seq, WINDOW, KV_WIDTH), F32),
                   jax.ShapeDtypeStruct((depth, n_seq, WINDOW, KV_WIDTH), F32)],
        input_output_aliases=aliases,
        compiler_params=_params(1),
    )(*args)


def kernel(x_prompt, x_sample, state_delta, state_conv, cache_swa_k, cache_swa_v, w_in, conv_w, a_log,
           dt_bias, norm_a_w, sinks, w_out, ln1_g, ln1_b, w_ffn_in, w_ffn_out, ln2_g, ln2_b):
    depth = w_in.shape[0]
    n_batch, seq, _ = x_prompt.shape
    n_dec, dec_seq, _ = x_sample.shape
    assert dec_seq == DEC_SEQ and seq % SEG == 0 and n_dec % SB == 0 and n_batch % N_SEQ == 0
    assert (n_batch * seq) % TM == 0 and (n_dec * S_TILE) % TM == 0
    assert cache_swa_k.shape[2] == WINDOW
    alpha = (2 * depth) ** 0.25

    c0 = CONV_DIM + A_WIDTH
    q0 = c0 + 2 * H_A
    order = jnp.array(QB_HEAD_ORDER)
    w_q = w_in[:, :, q0:q0 + B_WIDTH].reshape(depth, D_MODEL, H_QB, HD_B)[:, :, order]
    w_in_r = jnp.concatenate(
        [w_in[:, :, :c0], w_q.reshape(depth, D_MODEL, B_WIDTH), w_in[:, :, q0 + B_WIDTH:],
         w_in[:, :, c0:q0], jnp.zeros((depth, D_MODEL, LANES - 2 * H_A), w_in.dtype)], axis=-1).astype(BF16)
    w_out_bq = w_out[:, A_WIDTH:].reshape(depth, H_QB, HD_B, D_MODEL)[:, order].reshape(depth, B_WIDTH, D_MODEL)
    w_out_b = jnp.concatenate([w_out[:, :A_WIDTH], w_out_bq], axis=1).astype(BF16)
    w_ffn_in_b = w_ffn_in.astype(BF16)
    w_ffn_out_b = w_ffn_out.astype(BF16)
    lane_pad = lambda t: jnp.pad(t, ((0, 0), (H_A, LANES - 2 * H_A)))[:, None, :]
    alog_v = lane_pad(a_log)
    dtb_v = lane_pad(dt_bias)
    nw_v = norm_a_w[:, None, :]
    sink_v = jnp.broadcast_to(sinks[:, :, None], (depth, H_QB, LANES))
    ln1_g3, ln1_b3, ln2_g3, ln2_b3 = (t[:, None, :] for t in (ln1_g, ln1_b, ln2_g, ln2_b))

    hist = jnp.pad(state_conv, ((0, 0), (0, 0), (0, S_TILE - (CONV_WIDTH - 1)), (0, 0)))
    hist = hist.reshape(depth, n_dec * S_TILE, CONV_DIM)
    kc = cache_swa_k.reshape(depth, n_dec, WINDOW, KV_WIDTH)
    vc = cache_swa_v.reshape(depth, n_dec, WINDOW, KV_WIDTH)

    xp = x_prompt.reshape(n_batch * seq, D_MODEL)
    xs = jnp.pad(x_sample, ((0, 0), (ROW0, S_TILE - ROW0 - DEC_SEQ), (0, 0))).reshape(n_dec * S_TILE, D_MODEL)

    outs = [[] for _ in range(8)]
    s_acc = jnp.zeros((depth, n_dec, H_A, DK_A, DV_A), F32)
    k_acc = jnp.zeros((depth, n_dec, WINDOW, KV_WIDTH), F32)
    v_acc = jnp.zeros((depth, n_dec, WINDOW, KV_WIDTH), F32)
    for l in range(depth):
        proj_p, proj_s = _in_proj(xp, xs, w_in_r, l)
        oa_p, s_p, conv_p = _gdn_prompt(proj_p, n_batch, seq, conv_w, alog_v, dtb_v, nw_v, l)
        ob_p = _swa_prompt(proj_p, n_batch, seq, sink_v, l)
        oa_s, s_acc, conv_s = _gdn_sample(proj_s, hist, state_delta, conv_w, alog_v, dtb_v, nw_v, l, s_acc)
        ob_s, k_acc, v_acc = _swa_sample(proj_s, kc, vc, sink_v, l, k_acc, v_acc)
        xp, xs = _mix_ffn(xp, xs, oa_p, oa_s, ob_p, ob_s, w_out_b, ln1_g3, ln1_b3, w_ffn_in_b, w_ffn_out_b,
                          ln2_g3, ln2_b3, l, alpha)
        proj3 = proj_p.reshape(n_batch, seq, PROJ_PAD)
        outs[0].append(s_p)
        outs[1].append(conv_p.reshape(n_batch, SUBLANES, CONV_DIM)[:, SUBLANES - (CONV_WIDTH - 1):])
        outs[2].append(proj3[:, seq - WINDOW:, COL_KB:COL_KB + KV_WIDTH].reshape(n_batch, WINDOW, H_KVB, HD_B))
        outs[3].append(proj3[:, seq - WINDOW:, COL_VB:COL_VB + KV_WIDTH].reshape(n_batch, WINDOW, H_KVB, HD_B))
        outs[5].append(conv_s.reshape(n_dec, S_TILE, CONV_DIM)[:, :CONV_WIDTH - 1])

    y_prompt = xp.reshape(n_batch, seq, D_MODEL)
    y_sample = xs.reshape(n_dec, S_TILE, D_MODEL)[:, ROW0:ROW0 + DEC_SEQ]
    stacked = [jnp.stack(o) if o else None for o in outs]
    stacked[4] = s_acc
    stacked[6] = k_acc.reshape(depth, n_dec, WINDOW, H_KVB, HD_B)
    stacked[7] = v_acc.reshape(depth, n_dec, WINDOW, H_KVB, HD_B)
    return (y_prompt, y_sample) + tuple(stacked)
```

```python
import functools
import math
from typing import NamedTuple

import jax
import jax.numpy as jnp
from jax import lax
from jax.experimental import pallas as pl
from jax.experimental.pallas import tpu as pltpu

F32 = jnp.float32
BF16 = jnp.bfloat16

D_MODEL = 1024
H_A = 4
DK_A = 128
DV_A = 128
CONV_WIDTH = 4
CONV_DIM = 2 * H_A * DK_A + H_A * DV_A
A_WIDTH = H_A * DV_A
CHUNK = 64
HD_B = 64
H_QB = 8
H_KVB = 2
GQA_GROUP = H_QB // H_KVB
B_WIDTH = H_QB * HD_B
KV_WIDTH = H_KVB * HD_B
WINDOW = 128
D_FF = 2816
EPS = 1e-6

LANES = 128
SUBLANES = 8

COL_Z = CONV_DIM
COL_QB = COL_Z + A_WIDTH
COL_KB = COL_QB + B_WIDTH
COL_VB = COL_KB + KV_WIDTH
COL_BA = COL_VB + KV_WIDTH
PROJ_PAD = COL_BA + LANES
QB_HEAD_ORDER = tuple(hk * GQA_GROUP + g for g in range(GQA_GROUP) for hk in range(H_KVB))

S_TILE = SUBLANES
ROW0 = CONV_WIDTH - 1
DEC_SEQ = 4
SB = 32

TM = 512
SEG = 128
N_CHUNK = SEG // CHUNK
PAIR_UNITS = LANES // CHUNK
PREP_CHUNKS = 2
N_SEQ = 8
FF_CHUNK = 256
PROJ_COLS = 512
SWA_QB = 8
NEG = -1e30
LOG2E = math.log2(math.e)
VMEM_LIMIT = 56 * 1024 * 1024


def _params(n_axes, vmem=None):
    return pltpu.CompilerParams(
        dimension_semantics=("arbitrary",) * n_axes,
        vmem_limit_bytes=vmem if vmem is not None else VMEM_LIMIT)


def _dot(a, b):
    return jnp.dot(a.astype(BF16), b.astype(BF16), preferred_element_type=F32)


def _dot_nt(a, b):
    return lax.dot_general(a.astype(BF16), b.astype(BF16), (((1,), (1,)), ((), ())),
                           preferred_element_type=F32)


def _sigmoid(x):
    return 1.0 / (1.0 + jnp.exp(-x))


def _silu(x):
    h = 0.5 * x
    return h + h * jnp.tanh(h)


def _softplus(x):
    return jnp.maximum(x, 0.0) + jnp.log(1.0 + jnp.exp(-jnp.abs(x)))


def _layernorm(h, g, b):
    mu = jnp.mean(h, axis=-1, keepdims=True)
    d = h - mu
    var = jnp.mean(d * d, axis=-1, keepdims=True)
    return d * lax.rsqrt(var + EPS) * g + b


def _two_stream_specs(shape_cols, n_p, sample_rows=None):
    return (pl.BlockSpec((TM, shape_cols), lambda i: (jnp.minimum(i, n_p - 1), 0)),
            pl.BlockSpec((TM if sample_rows is None else sample_rows, shape_cols),
                         lambda i: (jnp.maximum(i - n_p, 0), 0)))


def _per_stream(n_p, body, prompt_refs, sample_refs):
    i = pl.program_id(0)

    @pl.when(i < n_p)
    def _():
        body(False, *prompt_refs)

    @pl.when(i >= n_p)
    def _():
        body(True, *sample_refs)


def _spread_tokens(t):
    rows = t.shape[0]
    token = _token_rows(SUBLANES)
    first = pltpu.roll(t, ROW0, axis=0)
    second = pltpu.roll(t, rows - (DEC_SEQ - ROW0), axis=0)
    tiles = []
    for v in range(rows // SUBLANES):
        sl = slice(v * SUBLANES, (v + 1) * SUBLANES)
        tiles += [jnp.where(token, first[sl], 0.0), jnp.where(token, second[sl], 0.0)]
    return jnp.concatenate(tiles, axis=0)


def _gather_tokens(t):
    rows = t.shape[0]
    low = lax.broadcasted_iota(jnp.int32, (SUBLANES, 1), 0) < DEC_SEQ
    first = pltpu.roll(t, rows - ROW0, axis=0)
    second = pltpu.roll(t, DEC_SEQ - ROW0, axis=0)
    tiles = []
    for v in range(rows // (2 * SUBLANES)):
        a = slice(2 * v * SUBLANES, (2 * v + 1) * SUBLANES)
        b = slice((2 * v + 1) * SUBLANES, (2 * v + 2) * SUBLANES)
        tiles.append(jnp.where(low, first[a], second[b]))
    return jnp.concatenate(tiles, axis=0)


def _in_proj_kernel(n_p, xp_ref, xs_ref, w_ref, op_ref, os_ref):
    def body(sample, x_ref, o_ref):
        xb = x_ref[...].astype(BF16)
        n = o_ref.shape[1]
        for n0 in range(0, n, PROJ_COLS):
            n1 = min(n0 + PROJ_COLS, n)
            res = jnp.dot(xb, w_ref[:, n0:n1], preferred_element_type=F32)
            o_ref[:, n0:n1] = _spread_tokens(res) if sample else res

    _per_stream(n_p, body, (xp_ref, op_ref), (xs_ref, os_ref))


def _in_proj(xp, xs, w_all, layer):
    ts = TM // (S_TILE // DEC_SEQ)
    n_p, n_s = xp.shape[0] // TM, xs.shape[0] // ts
    return pl.pallas_call(
        functools.partial(_in_proj_kernel, n_p),
        name="in_proj",
        grid=(n_p + n_s,),
        in_specs=[*_two_stream_specs(D_MODEL, n_p, ts),
                  pl.BlockSpec((None, D_MODEL, PROJ_PAD), lambda i: (layer, 0, 0))],
        out_specs=list(_two_stream_specs(PROJ_PAD, n_p)),
        out_shape=[jax.ShapeDtypeStruct((xp.shape[0], PROJ_PAD), F32),
                   jax.ShapeDtypeStruct((n_s * TM, PROJ_PAD), F32)],
        compiler_params=_params(1),
    )(xp, xs, w_all)


def _mix_ffn_kernel(alpha, n_p, xp_ref, xs_ref, oap_ref, oas_ref, obp_ref, obs_ref, wm_ref, g1_ref, b1_ref,
                    wi_ref, wo_ref, g2_ref, b2_ref, yp_ref, ys_ref):
    def body(sample, x_ref, oa_ref, ob_ref, o_ref):
        oa, ob = oa_ref[...], ob_ref[...]
        if sample:
            oa, ob = _gather_tokens(oa), _gather_tokens(ob)
        m = (jnp.dot(oa.astype(BF16), wm_ref[0:A_WIDTH, :], preferred_element_type=F32)
             + jnp.dot(ob.astype(BF16), wm_ref[A_WIDTH:, :], preferred_element_type=F32))
        x1 = _layernorm(alpha * x_ref[...] + m, g1_ref[...], b1_ref[...])
        xb = x1.astype(BF16)
        acc = alpha * x1
        for c0 in range(0, D_FF, FF_CHUNK):
            gate = jnp.dot(xb, wi_ref[:, c0:c0 + FF_CHUNK], preferred_element_type=F32)
            up = jnp.dot(xb, wi_ref[:, D_FF + c0:D_FF + c0 + FF_CHUNK], preferred_element_type=F32)
            h = _silu(gate) * up
            acc = acc + jnp.dot(h.astype(BF16), wo_ref[c0:c0 + FF_CHUNK, :], preferred_element_type=F32)
        o_ref[...] = _layernorm(acc, g2_ref[...], b2_ref[...])

    _per_stream(n_p, body, (xp_ref, oap_ref, obp_ref, yp_ref), (xs_ref, oas_ref, obs_ref, ys_ref))


def _mix_ffn(xp, xs, oap, oas, obp, obs, wm_all, g1_all, b1_all, wi_all, wo_all, g2_all, b2_all, layer, alpha):
    ts = TM // (S_TILE // DEC_SEQ)
    n_p, n_s = xp.shape[0] // TM, xs.shape[0] // ts
    vec = lambda: pl.BlockSpec((None, 1, D_MODEL), lambda i: (layer, 0, 0))
    weight = lambda rows, cols: pl.BlockSpec((None, rows, cols), lambda i: (layer, 0, 0),
                                             pipeline_mode=pl.Buffered(1))
    return pl.pallas_call(
        functools.partial(_mix_ffn_kernel, alpha, n_p),
        name="mix_ffn",
        grid=(n_p + n_s,),
        in_specs=[*_two_stream_specs(D_MODEL, n_p, ts), *_two_stream_specs(A_WIDTH, n_p),
                  *_two_stream_specs(B_WIDTH, n_p),
                  weight(D_MODEL, D_MODEL), vec(), vec(),
                  weight(D_MODEL, 2 * D_FF), weight(D_FF, D_MODEL), vec(), vec()],
        out_specs=list(_two_stream_specs(D_MODEL, n_p, ts)),
        out_shape=[jax.ShapeDtypeStruct(xp.shape, F32), jax.ShapeDtypeStruct(xs.shape, F32)],
        compiler_params=_params(1),
    )(xp, xs, oap, oas, obp, obs, wm_all, g1_all, b1_all, wi_all, wo_all, g2_all, b2_all)


def _conv_silu(x, cw):
    y = cw[3:4, :] * x
    for j in range(1, CONV_WIDTH):
        y = y + cw[CONV_WIDTH - 1 - j:CONV_WIDTH - j, :] * pltpu.roll(x, j, axis=0)
    return _silu(y)


def _gates(ba, alog, dtb):
    beta = _sigmoid(ba)
    g = -jnp.exp(alog) * _softplus(ba + dtb)
    return beta, g


def _cumsum_rows(g, c):
    rin = lax.broadcasted_iota(jnp.int32, g.shape, 0) & (c - 1)
    s = 1
    while s < c:
        g = g + jnp.where(rin >= s, pltpu.roll(g, s, axis=0), 0.0)
        s *= 2
    return g


class _Masks(NamedTuple):
    causal: jax.Array
    neg_strict: jax.Array
    eye: jax.Array
    unit: tuple


def _group_masks(c):
    r = LANES
    shift = int(math.log2(c))
    ri = lax.broadcasted_iota(jnp.int32, (r, r), 0)
    ci = lax.broadcasted_iota(jnp.int32, (r, r), 1)
    same = (ri >> shift) == (ci >> shift)
    return _Masks(
        causal=jnp.where(same & (ri >= ci), 1.0, 0.0),
        neg_strict=jnp.where(same & (ri > ci), -1.0, 0.0),
        eye=jnp.where(ri == ci, 1.0, 0.0),
        unit=tuple(jnp.where((ci >> shift) == i, 1.0, 0.0) for i in range(r // c)))


def _gdn_prepare(groups, c, mk):
    r = LANES
    n_units = r // c
    n_factors = int(math.log2(c))
    each = lambda f, *lists: [f(*a) for a in zip(*lists)]

    q, k, v, beta, gcol = (list(t) for t in zip(*groups))
    qn = each(lambda t: t * (lax.rsqrt(jnp.sum(t * t, axis=-1, keepdims=True) + EPS) * (DK_A ** -0.5)), q)
    kn = each(lambda t: t * lax.rsqrt(jnp.sum(t * t, axis=-1, keepdims=True) + EPS), k)
    gc = each(lambda t: jnp.broadcast_to(t, (r, r)), gcol)
    e = each(lambda t: jnp.exp(jnp.minimum(t - t.T, 0.0)), gc)
    kb = each(lambda a, b: a * b, kn, beta)
    kq = each(lambda a, b, d: _dot_nt(jnp.concatenate([a, b], axis=0), d), kb, qn, kn)
    bk = each(lambda a, b: a[:r] * (b * mk.neg_strict), kq, e)
    aqk = each(lambda a, b: a[r:] * (b * mk.causal), kq, e)

    p = each(lambda t: mk.eye + t, bk)
    bk = each(lambda t: _dot(t, t), bk)
    for _ in range(n_factors - 2):
        st = each(lambda a, b: _dot(jnp.concatenate([a, b], axis=0), b), p, bk)
        p = each(lambda a, b: a + b[:r], p, st)
        bk = each(lambda t: t[r:], st)
    p = each(lambda a, b: a + _dot(a, b), p, bk)

    eg = each(jnp.exp, gc)
    uw = each(lambda a, b, d, f, h: _dot(a, jnp.concatenate([b * d, f * h], axis=1)), p, v, beta, kb, eg)
    qd = each(lambda a, b: a * b, qn, eg)

    out = []
    for gi in range(len(groups)):
        g_last = [gc[gi][(i + 1) * c - 1:(i + 1) * c, :] for i in range(n_units)]
        gl = jnp.concatenate([jnp.broadcast_to(t, (c, r)) for t in g_last], axis=0)
        kdt = (kn[gi] * jnp.exp(gl - gc[gi])).T
        u = uw[gi][:, :DV_A]
        w = uw[gi][:, DV_A:]
        wq = jnp.concatenate([jnp.concatenate([w[i * c:(i + 1) * c], qd[gi][i * c:(i + 1) * c]], axis=0)
                              for i in range(n_units)], axis=0).astype(BF16)
        lhs = jnp.concatenate([aqk[gi]] + [kdt * mk.unit[i] for i in range(n_units)], axis=0).astype(BF16)
        out.append((u, wq, lhs, [jnp.exp(t) for t in g_last]))
    return out


def _gdn_apply(items, nw, c):
    r = LANES
    n_units = r // c
    res1 = [[jnp.dot(wq[2 * c * i:2 * c * (i + 1)], st[i].astype(BF16), preferred_element_type=F32)
             for i in range(n_units)] for _, wq, _, _, st, _ in items]
    vn = [jnp.concatenate([it[0][i * c:(i + 1) * c] - r1[i][:c] for i in range(n_units)], axis=0)
          for it, r1 in zip(items, res1)]
    res2 = [jnp.dot(it[2], v.astype(BF16), preferred_element_type=F32) for it, v in zip(items, vn)]
    out = []
    for (_, _, _, egl, st, z), r1, r2 in zip(items, res1, res2):
        o = jnp.concatenate([r1[i][c:] for i in range(n_units)], axis=0) + r2[:r]
        s_new = [st[i] * egl[i] + r2[r * (i + 1):r * (i + 2)] for i in range(n_units)]
        out.append((o * lax.rsqrt(jnp.mean(o * o, axis=-1, keepdims=True) + EPS) * nw * _silu(z), s_new))
    return out


def _gdn_prompt_kernel(n_seg, prev_ref, qkv_ref, z_ref, ba_ref, cw_ref, alog_ref, dtb_ref, nw_ref,
                       o_ref, s_ref, conv_ref, s_scr, u_scr, wq_scr, lhs_scr, egl_scr):
    seg = pl.program_id(1)
    n_pairs = H_A // 2

    @pl.when(seg == 0)
    def _():
        s_scr[...] = jnp.zeros_like(s_scr)

    cw = cw_ref[...]
    alog = alog_ref[...]
    dtb = dtb_ref[...]
    nw = nw_ref[...]
    mk = _group_masks(CHUNK)

    def prepare(it, carry):
        groups, where = [], []
        for j in range(PREP_CHUNKS):
            ci = it * PREP_CHUNKS + j
            r0 = pl.multiple_of(ci * CHUNK, CHUNK)
            for q in range(N_SEQ):
                before = jnp.where(ci > 0, qkv_ref[q, pl.ds(pl.multiple_of(jnp.maximum(r0 - SUBLANES, 0), SUBLANES),
                                                            SUBLANES), :],
                                   jnp.where(seg > 0, prev_ref[q], 0.0))
                y = _conv_silu(jnp.concatenate([before, qkv_ref[q, pl.ds(r0, CHUNK), :]], axis=0), cw)[SUBLANES:]
                beta_all, g_all = _gates(ba_ref[q, pl.ds(r0, CHUNK), :], alog, dtb)
                gc_all = _cumsum_rows(g_all, CHUNK)
                for pair in range(n_pairs):
                    heads = (2 * pair, 2 * pair + 1)
                    cols = lambda base: jnp.concatenate(
                        [y[:, base + h * LANES:base + (h + 1) * LANES] for h in heads], axis=0)
                    beta = jnp.concatenate([beta_all[:, h:h + 1] for h in heads], axis=0)
                    gcol = jnp.concatenate([gc_all[:, H_A + h:H_A + h + 1] for h in heads], axis=0)
                    groups.append((cols(0), cols(H_A * DK_A), cols(2 * H_A * DK_A), beta, gcol))
                    where.append((ci, q, pair))
        for (ci, q, pair), (u, wq, lhs, egl) in zip(where, _gdn_prepare(groups, CHUNK, mk)):
            u_scr[ci, q, pair] = u
            wq_scr[ci, q, pair] = wq
            lhs_scr[ci, q, pair] = lhs
            for i in range(PAIR_UNITS):
                egl_scr[ci, q, pair, i:i + 1, :] = egl[i]
        return carry

    lax.fori_loop(0, N_CHUNK // PREP_CHUNKS, prepare, 0)

    def scan(ci, carry):
        r0 = pl.multiple_of(ci * CHUNK, CHUNK)
        items, where = [], []
        for q in range(N_SEQ):
            z = z_ref[q, pl.ds(r0, CHUNK), :]
            for pair in range(n_pairs):
                heads = (2 * pair, 2 * pair + 1)
                zz = jnp.concatenate([z[:, h * DV_A:(h + 1) * DV_A] for h in heads], axis=0)
                egl = [egl_scr[ci, q, pair, i:i + 1, :] for i in range(PAIR_UNITS)]
                items.append((u_scr[ci, q, pair], wq_scr[ci, q, pair], lhs_scr[ci, q, pair], egl,
                              [s_scr[q, h] for h in heads], zz))
                where.append((q, heads))
        for (q, heads), (on, s_new) in zip(where, _gdn_apply(items, nw, CHUNK)):
            for i, h in enumerate(heads):
                o_ref[q, pl.ds(r0, CHUNK), h * DV_A:(h + 1) * DV_A] = on[i * CHUNK:(i + 1) * CHUNK]
                s_scr[q, h] = s_new[i]
        return carry

    lax.fori_loop(0, N_CHUNK, scan, 0)

    @pl.when(seg == n_seg - 1)
    def _():
        s_ref[:, 0] = s_scr[...]
        conv_ref[:, 0] = qkv_ref[:, SEG - SUBLANES:SEG, :]


def _gdn_prompt(proj, n_batch, seq, cw_all, alog_all, dtb_all, nw_all, layer):
    n_seg = seq // SEG
    seg8 = SEG // SUBLANES
    seq8 = seq // SUBLANES
    n_pairs = H_A // 2
    nb = n_batch // N_SEQ
    proj3 = proj.reshape(N_SEQ, nb * seq, PROJ_PAD)
    blk = lambda width, col: pl.BlockSpec((N_SEQ, SEG, width), lambda bb, s: (0, bb * n_seg + s, col))
    vec = lambda: pl.BlockSpec((None, 1, LANES), lambda bb, s: (layer, 0, 0))
    o, s_out, conv = pl.pallas_call(
        functools.partial(_gdn_prompt_kernel, n_seg),
        name="gdn_prompt",
        grid=(nb, n_seg),
        in_specs=[pl.BlockSpec((N_SEQ, SUBLANES, CONV_DIM),
                               lambda bb, s: (0, jnp.maximum(bb * seq8 + s * seg8 - 1, 0), 0)),
                  blk(CONV_DIM, 0), blk(A_WIDTH, COL_Z // A_WIDTH), blk(LANES, COL_BA // LANES),
                  pl.BlockSpec((None, CONV_WIDTH, CONV_DIM), lambda bb, s: (layer, 0, 0)),
                  vec(), vec(), vec()],
        out_specs=[blk(A_WIDTH, 0),
                   pl.BlockSpec((N_SEQ, 1, H_A, DK_A, DV_A), lambda bb, s: (0, bb, 0, 0, 0)),
                   pl.BlockSpec((N_SEQ, 1, SUBLANES, CONV_DIM), lambda bb, s: (0, bb, 0, 0))],
        out_shape=[jax.ShapeDtypeStruct((N_SEQ, nb * seq, A_WIDTH), F32),
                   jax.ShapeDtypeStruct((N_SEQ, nb, H_A, DK_A, DV_A), F32),
                   jax.ShapeDtypeStruct((N_SEQ, nb, SUBLANES, CONV_DIM), F32)],
        scratch_shapes=[pltpu.VMEM((N_SEQ, H_A, DK_A, DV_A), F32),
                        pltpu.VMEM((N_CHUNK, N_SEQ, n_pairs, LANES, DV_A), F32),
                        pltpu.VMEM((N_CHUNK, N_SEQ, n_pairs, 2 * LANES, DK_A), BF16),
                        pltpu.VMEM((N_CHUNK, N_SEQ, n_pairs, 3 * LANES, LANES), BF16),
                        pltpu.VMEM((N_CHUNK, N_SEQ, n_pairs, SUBLANES, LANES), F32)],
        compiler_params=_params(2),
    )(proj3, proj3, proj3, proj3, cw_all, alog_all, dtb_all, nw_all)
    return (o.reshape(n_batch * seq, A_WIDTH), s_out.reshape(n_batch, H_A, DK_A, DV_A),
            conv.reshape(n_batch * SUBLANES, CONV_DIM))


def _token_rows(n_rows):
    row = lax.broadcasted_iota(jnp.int32, (n_rows, 1), 0) & (S_TILE - 1)
    return (row >= ROW0) & (row < ROW0 + DEC_SEQ)


def _gdn_sample_kernel(qkv_ref, hist_ref, z_ref, ba_ref, cw_ref, alog_ref, dtb_ref, nw_ref, s_in_ref, s_acc_ref,
                       o_ref, s_out_ref, conv_ref):
    del s_acc_ref
    rows = SB * S_TILE
    valid = _token_rows(rows)
    x = jnp.where(valid, qkv_ref[...], 0.0) + hist_ref[...]
    conv_ref[...] = pltpu.roll(x, rows - DEC_SEQ, axis=0)
    y = _conv_silu(x, cw_ref[...])
    beta_all, g_all = _gates(ba_ref[...], alog_ref[...], dtb_ref[...])
    beta_all = jnp.where(valid, beta_all, 0.0)
    gc_all = _cumsum_rows(jnp.where(valid, g_all, 0.0), S_TILE)
    z = z_ref[...]
    ym = jnp.where(valid, y, 0.0)

    all_units = [(b, h) for b in range(SB) for h in range(H_A)]
    per_group = LANES // S_TILE
    unit_groups = [all_units[i:i + per_group] for i in range(0, len(all_units), per_group)]

    def stack(units, src, base, width=LANES):
        return jnp.concatenate(
            [src[b * S_TILE:(b + 1) * S_TILE, base + h * width:base + (h + 1) * width] for b, h in units], axis=0)

    mk = _group_masks(S_TILE)
    prepared = _gdn_prepare(
        [(stack(us, y, 0), stack(us, ym, H_A * DK_A), stack(us, ym, 2 * H_A * DK_A),
          stack(us, beta_all, 0, 1), stack(us, gc_all, H_A, 1)) for us in unit_groups], S_TILE, mk)
    items = [(u, wq, lhs, egl, [s_in_ref[b, h] for b, h in us], stack(us, z, 0))
             for us, (u, wq, lhs, egl) in zip(unit_groups, prepared)]
    for us, (on, s_new) in zip(unit_groups, _gdn_apply(items, nw_ref[...], S_TILE)):
        for i, (b, h) in enumerate(us):
            o_ref[b * S_TILE:(b + 1) * S_TILE, h * DV_A:(h + 1) * DV_A] = on[i * S_TILE:(i + 1) * S_TILE]
            s_out_ref[b, h] = s_new[i]


def _gdn_sample(proj, hist, state_all, cw_all, alog_all, dtb_all, nw_all, layer, s_acc):
    rows = SB * S_TILE
    n_seq = proj.shape[0] // S_TILE
    depth = state_all.shape[0]
    vec = lambda: pl.BlockSpec((None, 1, LANES), lambda i: (layer, 0, 0))
    in_specs = [pl.BlockSpec((rows, CONV_DIM), lambda i: (i, 0)),
                pl.BlockSpec((None, rows, CONV_DIM), lambda i: (layer, i, 0)),
                pl.BlockSpec((rows, A_WIDTH), lambda i: (i, COL_Z // A_WIDTH)),
                pl.BlockSpec((rows, LANES), lambda i: (i, COL_BA // LANES)),
                pl.BlockSpec((None, CONV_WIDTH, CONV_DIM), lambda i: (layer, 0, 0)),
                vec(), vec(), vec(),
                pl.BlockSpec((None, SB, H_A, DK_A, DV_A), lambda i: (layer, i, 0, 0, 0))]
    in_specs.append(pl.BlockSpec(memory_space=pl.ANY))
    args = [proj, hist, proj, proj, cw_all, alog_all, dtb_all, nw_all, state_all, s_acc]
    aliases = {len(args) - 1: 1}
    return pl.pallas_call(
        _gdn_sample_kernel,
        name="gdn_sample",
        grid=(n_seq // SB,),
        in_specs=in_specs,
        out_specs=[pl.BlockSpec((rows, A_WIDTH), lambda i: (i, 0)),
                   pl.BlockSpec((None, SB, H_A, DK_A, DV_A), lambda i: (layer, i, 0, 0, 0)),
                   pl.BlockSpec((rows, CONV_DIM), lambda i: (i, 0))],
        out_shape=[jax.ShapeDtypeStruct((n_seq * S_TILE, A_WIDTH), F32),
                   jax.ShapeDtypeStruct((depth, n_seq, H_A, DK_A, DV_A), F32),
                   jax.ShapeDtypeStruct((n_seq * S_TILE, CONV_DIM), F32)],
        input_output_aliases=aliases,
        compiler_params=_params(1),
    )(*args)


def _swa_heads(items, mask, sink_ref, q_rows):
    lane_head = lax.broadcasted_iota(jnp.int32, (1, LANES), 1) >> int(math.log2(HD_B))
    chains = [(it, hk) for it in range(len(items)) for hk in range(H_KVB)]
    each = lambda f, *lists: [f(*a) for a in zip(*lists)]
    qcat = [jnp.concatenate([q[:, gi * LANES:(gi + 1) * LANES] for gi in range(GQA_GROUP)], axis=0)
            * (HD_B ** -0.5) for q, _, _ in items]
    sk_head = [jnp.concatenate(
        [jnp.broadcast_to(sink_ref[hk * GQA_GROUP + gi:hk * GQA_GROUP + gi + 1, 0:1], (q_rows, 1))
         for gi in range(GQA_GROUP)], axis=0) for hk in range(H_KVB)]
    sk = [sk_head[hk] for _, hk in chains]
    kh = [jnp.where(lane_head == hk, items[it][1], 0.0) for it, hk in chains]
    vh = [jnp.where(lane_head == hk, items[it][2], 0.0) for it, hk in chains]
    s = [jnp.where(mask, _dot_nt(qcat[it], kh[ch]), NEG) for ch, (it, _) in enumerate(chains)]
    m = each(lambda a, b: jnp.maximum(jnp.max(a, axis=-1, keepdims=True), b), s, sk)
    p = each(lambda a, b: jnp.exp(a - b), s, m)
    den = each(lambda a, b, d: jnp.sum(a, axis=-1, keepdims=True) + jnp.exp(b - d), p, sk, m)
    o = each(lambda a, b, d: _dot(a, b) * (1.0 / d), p, vh, den)
    outs = []
    for it in range(len(items)):
        tot = o[H_KVB * it]
        for hk in range(1, H_KVB):
            tot = tot + o[H_KVB * it + hk]
        outs.append(jnp.concatenate([tot[gi * q_rows:(gi + 1) * q_rows] for gi in range(GQA_GROUP)], axis=1))
    return outs


def _swa_prompt_kernel(q_ref, kc_ref, kp_ref, vc_ref, vp_ref, sink_ref, o_ref):
    n0 = pl.program_id(1) * SWA_QB
    keys = 2 * WINDOW
    k_ext = jnp.concatenate([kp_ref[...], kc_ref[...]], axis=0)
    v_ext = jnp.concatenate([vp_ref[...], vc_ref[...]], axis=0)
    c = lax.broadcasted_iota(jnp.int32, (keys, 2 * WINDOW), 0)
    r = lax.broadcasted_iota(jnp.int32, (keys, 2 * WINDOW), 1) & (WINDOW - 1)
    band = jnp.where((c > r) & (c <= r + WINDOW), 0.0, NEG)
    shift = int(math.log2(HD_B))
    lane_head = lax.broadcasted_iota(jnp.int32, (1, LANES), 1) >> shift
    row = lax.broadcasted_iota(jnp.int32, (LANES, 1), 0)
    row_head = row >> shift
    each = lambda f, *lists: [f(*a) for a in zip(*lists)]
    n_gp = GQA_GROUP // 2

    chains = [(j, hk, gp) for j in range(SWA_QB) for hk in range(H_KVB) for gp in range(n_gp)]
    bias = [jnp.where(c + n0 * WINDOW >= WINDOW, band, NEG)] + [band] * (SWA_QB - 1)
    kh, vth, qpair = {}, {}, {}
    for j in range(SWA_QB):
        k_all = k_ext[j * WINDOW:(j + 2) * WINDOW]
        vt = v_ext[j * WINDOW:(j + 2) * WINDOW].T
        qs = q_ref[j * WINDOW:(j + 1) * WINDOW, :] * (HD_B ** -0.5 * LOG2E)
        for hk in range(H_KVB):
            kh[j, hk] = jnp.where(lane_head == hk, k_all, 0.0).astype(BF16)
            vth[j, hk] = jnp.where(row_head == hk, vt, jnp.where(row == (1 - hk) * HD_B, 1.0, 0.0)).astype(BF16)
        for gp in range(n_gp):
            qpair[j, gp] = jnp.concatenate([qs[:, (2 * gp) * LANES:(2 * gp + 1) * LANES],
                                            qs[:, (2 * gp + 1) * LANES:(2 * gp + 2) * LANES]],
                                           axis=0).astype(BF16)
    ones_row = [(1 - hk) * HD_B for _, hk, _ in chains]
    sk = [jnp.concatenate([sink_ref[hk * GQA_GROUP + 2 * gp + i:hk * GQA_GROUP + 2 * gp + i + 1, :]
                           for i in range(2)], axis=1) * LOG2E for _, hk, gp in chains]
    st = [_dot_nt(kh[j, hk], qpair[j, gp]) + bias[j] for j, hk, gp in chains]
    m = each(lambda a, b: jnp.maximum(jnp.max(a, axis=0, keepdims=True), b), st, sk)
    pt = each(lambda a, b: jnp.exp2(a - b).astype(BF16), st, m)
    ot = [jnp.dot(vth[j, hk], pt[ch], preferred_element_type=F32) for ch, (j, hk, _) in enumerate(chains)]
    den = each(lambda a, i, b, d: a[i:i + 1, :] + jnp.exp2(b - d), ot, ones_row, sk, m)
    ot = [jnp.where(row_head == hk, ot[ch] * (1.0 / den[ch]), 0.0) for ch, (_, hk, _) in enumerate(chains)]
    for j in range(SWA_QB):
        tiles = []
        for gp in range(n_gp):
            tot = ot[chains.index((j, 0, gp))]
            for hk in range(1, H_KVB):
                tot = tot + ot[chains.index((j, hk, gp))]
            tiles += [tot[:, :WINDOW].T, tot[:, WINDOW:].T]
        o_ref[j * WINDOW:(j + 1) * WINDOW, :] = jnp.concatenate(tiles, axis=1)


def _swa_prompt(proj, n_batch, seq, sink_all, layer):
    nb = seq // WINDOW
    ns = nb // SWA_QB
    qrows = SWA_QB * WINDOW
    cur = lambda col: (lambda b, n: (b * ns + n, col))
    prev = lambda col: (lambda b, n: (b * nb + jnp.maximum(n * SWA_QB - 1, 0), col))
    return pl.pallas_call(
        _swa_prompt_kernel,
        name="swa_prompt",
        grid=(n_batch, ns),
        in_specs=[pl.BlockSpec((qrows, B_WIDTH), cur(COL_QB // B_WIDTH)),
                  pl.BlockSpec((qrows, KV_WIDTH), cur(COL_KB // KV_WIDTH)),
                  pl.BlockSpec((WINDOW, KV_WIDTH), prev(COL_KB // KV_WIDTH)),
                  pl.BlockSpec((qrows, KV_WIDTH), cur(COL_VB // KV_WIDTH)),
                  pl.BlockSpec((WINDOW, KV_WIDTH), prev(COL_VB // KV_WIDTH)),
                  pl.BlockSpec((None, H_QB, LANES), lambda b, n: (layer, 0, 0))],
        out_specs=pl.BlockSpec((qrows, B_WIDTH), lambda b, n: (b * ns + n, 0)),
        out_shape=jax.ShapeDtypeStruct((n_batch * seq, B_WIDTH), F32),
        compiler_params=_params(2),
    )(proj, proj, proj, proj, proj, sink_all)


def _shift_cache(cache, new_tile):
    rolled = pltpu.roll(cache, WINDOW - DEC_SEQ, axis=0)
    moved = pltpu.roll(new_tile, S_TILE - DEC_SEQ - ROW0, axis=0)
    row = lax.broadcasted_iota(jnp.int32, (S_TILE, 1), 0)
    tail = jnp.where(row >= S_TILE - DEC_SEQ, moved, rolled[WINDOW - S_TILE:])
    return jnp.concatenate([rolled[:WINDOW - S_TILE], tail], axis=0)


def _swa_sample_kernel(q_ref, k_ref, v_ref, kc_ref, vc_ref, sink_ref, k_acc_ref, v_acc_ref, o_ref, ko_ref, vo_ref):
    del k_acc_ref, v_acc_ref
    shape = (GQA_GROUP * S_TILE, 2 * WINDOW)
    r = (lax.broadcasted_iota(jnp.int32, shape, 0) & (S_TILE - 1)) - ROW0
    c = lax.broadcasted_iota(jnp.int32, shape, 1)
    j = c - WINDOW - ROW0
    mask = ((c < WINDOW) & (c > r)) | ((j >= 0) & (j < DEC_SEQ) & (j <= r))
    pad = jnp.zeros((WINDOW - S_TILE, KV_WIDTH), F32)
    items = []
    for b in range(SB):
        rows = slice(b * S_TILE, (b + 1) * S_TILE)
        k_new = k_ref[rows, :]
        v_new = v_ref[rows, :]
        items.append((q_ref[rows, :], jnp.concatenate([kc_ref[b], k_new, pad], axis=0),
                      jnp.concatenate([vc_ref[b], v_new, pad], axis=0)))
        ko_ref[b] = _shift_cache(kc_ref[b], k_new)
        vo_ref[b] = _shift_cache(vc_ref[b], v_new)
    for b, o in enumerate(_swa_heads(items, mask, sink_ref, S_TILE)):
        o_ref[b * S_TILE:(b + 1) * S_TILE, :] = o


def _swa_sample(proj, kc_all, vc_all, sink_all, layer, k_acc, v_acc):
    rows = SB * S_TILE
    n_seq = proj.shape[0] // S_TILE
    depth = kc_all.shape[0]
    cache = lambda: pl.BlockSpec((None, SB, WINDOW, KV_WIDTH), lambda i: (layer, i, 0, 0))
    in_specs = [pl.BlockSpec((rows, B_WIDTH), lambda i: (i, COL_QB // B_WIDTH)),
                pl.BlockSpec((rows, KV_WIDTH), lambda i: (i, COL_KB // KV_WIDTH)),
                pl.BlockSpec((rows, KV_WIDTH), lambda i: (i, COL_VB // KV_WIDTH)),
                cache(), cache(),
                pl.BlockSpec((None, H_QB, LANES), lambda i: (layer, 0, 0))]
    in_specs += [pl.BlockSpec(memory_space=pl.ANY)] * 2
    args = [proj, proj, proj, kc_all, vc_all, sink_all, k_acc, v_acc]
    aliases = {len(args) - 2: 1, len(args) - 1: 2}
    return pl.pallas_call(
        _swa_sample_kernel,
        name="swa_sample",
        grid=(n_seq // SB,),
        in_specs=in_specs,
        out_specs=[pl.BlockSpec((rows, B_WIDTH), lambda i: (i, 0)), cache(), cache()],
        out_shape=[jax.ShapeDtypeStruct((n_seq * S_TILE, B_WIDTH), F32),
                   jax.ShapeDtypeStruct((depth, n_seq, WINDOW, KV_WIDTH), F32),
                   jax.ShapeDtypeStruct((depth, n_seq, WINDOW, KV_WIDTH), F32)],
        input_output_aliases=aliases,
        compiler_params=_params(1),
    )(*args)


def kernel(x_prompt, x_sample, state_delta, state_conv, cache_swa_k, cache_swa_v, w_in, conv_w, a_log,
           dt_bias, norm_a_w, sinks, w_out, ln1_g, ln1_b, w_ffn_in, w_ffn_out, ln2_g, ln2_b):
    depth = w_in.shape[0]
    n_batch, seq, _ = x_prompt.shape
    n_dec, dec_seq, _ = x_sample.shape
    assert dec_seq == DEC_SEQ and seq % SEG == 0 and n_dec % SB == 0 and n_batch % N_SEQ == 0
    assert (n_batch * seq) % TM == 0 and (n_dec * S_TILE) % TM == 0
    assert cache_swa_k.shape[2] == WINDOW
    alpha = (2 * depth) ** 0.25

    c0 = CONV_DIM + A_WIDTH
    q0 = c0 + 2 * H_A
    order = jnp.array(QB_HEAD_ORDER)
    w_q = w_in[:, :, q0:q0 + B_WIDTH].reshape(depth, D_MODEL, H_QB, HD_B)[:, :, order]
    w_in_r = jnp.concatenate(
        [w_in[:, :, :c0], w_q.reshape(depth, D_MODEL, B_WIDTH), w_in[:, :, q0 + B_WIDTH:],
         w_in[:, :, c0:q0], jnp.zeros((depth, D_MODEL, LANES - 2 * H_A), w_in.dtype)], axis=-1).astype(BF16)
    w_out_bq = w_out[:, A_WIDTH:].reshape(depth, H_QB, HD_B, D_MODEL)[:, order].reshape(depth, B_WIDTH, D_MODEL)
    w_out_b = jnp.concatenate([w_out[:, :A_WIDTH], w_out_bq], axis=1).astype(BF16)
    w_ffn_in_b = w_ffn_in.astype(BF16)
    w_ffn_out_b = w_ffn_out.astype(BF16)
    lane_pad = lambda t: jnp.pad(t, ((0, 0), (H_A, LANES - 2 * H_A)))[:, None, :]
    alog_v = lane_pad(a_log)
    dtb_v = lane_pad(dt_bias)
    nw_v = norm_a_w[:, None, :]
    sink_v = jnp.broadcast_to(sinks[:, :, None], (depth, H_QB, LANES))
    ln1_g3, ln1_b3, ln2_g3, ln2_b3 = (t[:, None, :] for t in (ln1_g, ln1_b, ln2_g, ln2_b))

    hist = jnp.pad(state_conv, ((0, 0), (0, 0), (0, S_TILE - (CONV_WIDTH - 1)), (0, 0)))
    hist = hist.reshape(depth, n_dec * S_TILE, CONV_DIM)
    kc = cache_swa_k.reshape(depth, n_dec, WINDOW, KV_WIDTH)
    vc = cache_swa_v.reshape(depth, n_dec, WINDOW, KV_WIDTH)

    xp = x_prompt.reshape(n_batch * seq, D_MODEL)
    xs = x_sample.reshape(n_dec * DEC_SEQ, D_MODEL)

    outs = [[] for _ in range(8)]
    s_acc = jnp.zeros((depth, n_dec, H_A, DK_A, DV_A), F32)
    k_acc = jnp.zeros((depth, n_dec, WINDOW, KV_WIDTH), F32)
    v_acc = jnp.zeros((depth, n_dec, WINDOW, KV_WIDTH), F32)
    for l in range(depth):
        proj_p, proj_s = _in_proj(xp, xs, w_in_r, l)
        oa_p, s_p, conv_p = _gdn_prompt(proj_p, n_batch, seq, conv_w, alog_v, dtb_v, nw_v, l)
        ob_p = _swa_prompt(proj_p, n_batch, seq, sink_v, l)
        oa_s, s_acc, conv_s = _gdn_sample(proj_s, hist, state_delta, conv_w, alog_v, dtb_v, nw_v, l, s_acc)
        ob_s, k_acc, v_acc = _swa_sample(proj_s, kc, vc, sink_v, l, k_acc, v_acc)
        xp, xs = _mix_ffn(xp, xs, oa_p, oa_s, ob_p, ob_s, w_out_b, ln1_g3, ln1_b3, w_ffn_in_b, w_ffn_out_b,
                          ln2_g3, ln2_b3, l, alpha)
        proj3 = proj_p.reshape(n_batch, seq, PROJ_PAD)
        outs[0].append(s_p)
        outs[1].append(conv_p.reshape(n_batch, SUBLANES, CONV_DIM)[:, SUBLANES - (CONV_WIDTH - 1):])
        outs[2].append(proj3[:, seq - WINDOW:, COL_KB:COL_KB + KV_WIDTH].reshape(n_batch, WINDOW, H_KVB, HD_B))
        outs[3].append(proj3[:, seq - WINDOW:, COL_VB:COL_VB + KV_WIDTH].reshape(n_batch, WINDOW, H_KVB, HD_B))
        outs[5].append(conv_s.reshape(n_dec, S_TILE, CONV_DIM)[:, :CONV_WIDTH - 1])

    y_prompt = xp.reshape(n_batch, seq, D_MODEL)
    y_sample = xs.reshape(n_dec, DEC_SEQ, D_MODEL)
    stacked = [jnp.stack(o) if o else None for o in outs]
    stacked[4] = s_acc
    stacked[6] = k_acc.reshape(depth, n_dec, WINDOW, H_KVB, HD_B)
    stacked[7] = v_acc.reshape(depth, n_dec, WINDOW, H_KVB, HD_B)
    return (y_prompt, y_sample) + tuple(stacked)
```

```python
import functools
import math
from typing import NamedTuple

import jax
import jax.numpy as jnp
from jax import lax
from jax.experimental import pallas as pl
from jax.experimental.pallas import tpu as pltpu

F32 = jnp.float32
BF16 = jnp.bfloat16

D_MODEL = 1024
H_A = 4
DK_A = 128
DV_A = 128
CONV_WIDTH = 4
CONV_DIM = 2 * H_A * DK_A + H_A * DV_A
A_WIDTH = H_A * DV_A
CHUNK = 64
HD_B = 64
H_QB = 8
H_KVB = 2
GQA_GROUP = H_QB // H_KVB
B_WIDTH = H_QB * HD_B
KV_WIDTH = H_KVB * HD_B
WINDOW = 128
D_FF = 2816
EPS = 1e-6

LANES = 128
SUBLANES = 8

COL_Z = CONV_DIM
COL_QB = COL_Z + A_WIDTH
COL_KB = COL_QB + B_WIDTH
COL_VB = COL_KB + KV_WIDTH
COL_BA = COL_VB + KV_WIDTH
PROJ_PAD = COL_BA + LANES
QB_HEAD_ORDER = tuple(hk * GQA_GROUP + g for g in range(GQA_GROUP) for hk in range(H_KVB))

S_TILE = SUBLANES
ROW0 = CONV_WIDTH - 1
DEC_SEQ = 4
SB = 32

TM = 512
SEG = 128
N_CHUNK = SEG // CHUNK
PAIR_UNITS = LANES // CHUNK
PREP_CHUNKS = 2
N_SEQ = 8
FF_CHUNK = 256
W_STAGE_COLS = 512
PROJ_COLS = 512
SWA_QB = 8
NEG = -1e30
LOG2E = math.log2(math.e)
VMEM_LIMIT = 56 * 1024 * 1024


def _params(n_axes, vmem=None):
    return pltpu.CompilerParams(
        dimension_semantics=("arbitrary",) * n_axes,
        vmem_limit_bytes=vmem if vmem is not None else VMEM_LIMIT)


def _dot(a, b):
    return jnp.dot(a.astype(BF16), b.astype(BF16), preferred_element_type=F32)


def _dot_nt(a, b):
    return lax.dot_general(a.astype(BF16), b.astype(BF16), (((1,), (1,)), ((), ())),
                           preferred_element_type=F32)


def _sigmoid(x):
    return 1.0 / (1.0 + jnp.exp(-x))


def _silu(x):
    h = 0.5 * x
    return h + h * jnp.tanh(h)


def _softplus(x):
    return jnp.maximum(x, 0.0) + jnp.log(1.0 + jnp.exp(-jnp.abs(x)))


def _layernorm(h, g, b):
    mu = jnp.mean(h, axis=-1, keepdims=True)
    d = h - mu
    var = jnp.mean(d * d, axis=-1, keepdims=True)
    return d * lax.rsqrt(var + EPS) * g + b


def _two_stream_specs(shape_cols, n_p, sample_rows=None):
    return (pl.BlockSpec((TM, shape_cols), lambda i: (jnp.minimum(i, n_p - 1), 0)),
            pl.BlockSpec((TM if sample_rows is None else sample_rows, shape_cols),
                         lambda i: (jnp.maximum(i - n_p, 0), 0)))


def _per_stream(n_p, body, prompt_refs, sample_refs):
    i = pl.program_id(0)

    @pl.when(i < n_p)
    def _():
        body(False, *prompt_refs)

    @pl.when(i >= n_p)
    def _():
        body(True, *sample_refs)


def _spread_tokens(t):
    rows = t.shape[0]
    token = _token_rows(SUBLANES)
    first = pltpu.roll(t, ROW0, axis=0)
    second = pltpu.roll(t, rows - (DEC_SEQ - ROW0), axis=0)
    tiles = []
    for v in range(rows // SUBLANES):
        sl = slice(v * SUBLANES, (v + 1) * SUBLANES)
        tiles += [jnp.where(token, first[sl], 0.0), jnp.where(token, second[sl], 0.0)]
    return jnp.concatenate(tiles, axis=0)


def _gather_tokens(t):
    rows = t.shape[0]
    low = lax.broadcasted_iota(jnp.int32, (SUBLANES, 1), 0) < DEC_SEQ
    first = pltpu.roll(t, rows - ROW0, axis=0)
    second = pltpu.roll(t, DEC_SEQ - ROW0, axis=0)
    tiles = []
    for v in range(rows // (2 * SUBLANES)):
        a = slice(2 * v * SUBLANES, (2 * v + 1) * SUBLANES)
        b = slice((2 * v + 1) * SUBLANES, (2 * v + 2) * SUBLANES)
        tiles.append(jnp.where(low, first[a], second[b]))
    return jnp.concatenate(tiles, axis=0)


def _in_proj_kernel(n_p, xp_ref, xs_ref, w_ref, op_ref, os_ref):
    def body(sample, x_ref, o_ref):
        xb = x_ref[...].astype(BF16)
        n = o_ref.shape[1]
        for n0 in range(0, n, PROJ_COLS):
            n1 = min(n0 + PROJ_COLS, n)
            res = jnp.dot(xb, w_ref[:, n0:n1], preferred_element_type=F32)
            o_ref[:, n0:n1] = _spread_tokens(res) if sample else res

    _per_stream(n_p, body, (xp_ref, op_ref), (xs_ref, os_ref))


def _in_proj(xp, xs, w_all, layer):
    ts = TM // (S_TILE // DEC_SEQ)
    n_p, n_s = xp.shape[0] // TM, xs.shape[0] // ts
    return pl.pallas_call(
        functools.partial(_in_proj_kernel, n_p),
        name="in_proj",
        grid=(n_p + n_s,),
        in_specs=[*_two_stream_specs(D_MODEL, n_p, ts),
                  pl.BlockSpec((None, D_MODEL, PROJ_PAD), lambda i: (layer, 0, 0))],
        out_specs=list(_two_stream_specs(PROJ_PAD, n_p)),
        out_shape=[jax.ShapeDtypeStruct((xp.shape[0], PROJ_PAD), F32),
                   jax.ShapeDtypeStruct((n_s * TM, PROJ_PAD), F32)],
        compiler_params=_params(1),
    )(xp, xs, w_all)


def _stage_weight(w_hbm, w_scr, stage, sem, axis, chunk):
    n = w_hbm.shape[axis] // chunk

    def copy(c, slot):
        idx = (slice(None), pl.ds(c * chunk, chunk)) if axis == 1 else (pl.ds(c * chunk, chunk), slice(None))
        return pltpu.make_async_copy(w_hbm.at[idx], stage.at[slot], sem.at[slot])

    copy(0, 0).start()
    for c in range(n):
        slot = c % 2
        if c + 1 < n:
            copy(c + 1, 1 - slot).start()
        copy(c, slot).wait()
        if axis == 1:
            w_scr[:, c * chunk:(c + 1) * chunk] = stage[slot].astype(BF16)
        else:
            w_scr[c * chunk:(c + 1) * chunk, :] = stage[slot].astype(BF16)


def _mix_ffn_kernel(alpha, n_p, layer, xp_ref, xs_ref, oap_ref, oas_ref, obp_ref, obs_ref, wm_ref, g1_ref, b1_ref,
                    wi_hbm, wo_hbm, g2_ref, b2_ref, yp_ref, ys_ref, wi_ref, wo_ref, stage_i, stage_o, sem_i, sem_o):
    @pl.when(pl.program_id(0) == 0)
    def _():
        _stage_weight(wi_hbm.at[layer], wi_ref, stage_i, sem_i, 1, W_STAGE_COLS)
        _stage_weight(wo_hbm.at[layer], wo_ref, stage_o, sem_o, 0, FF_CHUNK)

    def body(sample, x_ref, oa_ref, ob_ref, o_ref):
        oa, ob = oa_ref[...], ob_ref[...]
        if sample:
            oa, ob = _gather_tokens(oa), _gather_tokens(ob)
        m = (jnp.dot(oa.astype(BF16), wm_ref[0:A_WIDTH, :], preferred_element_type=F32)
             + jnp.dot(ob.astype(BF16), wm_ref[A_WIDTH:, :], preferred_element_type=F32))
        x1 = _layernorm(alpha * x_ref[...] + m, g1_ref[...], b1_ref[...])
        xb = x1.astype(BF16)
        acc = alpha * x1
        for c0 in range(0, D_FF, FF_CHUNK):
            gate = jnp.dot(xb, wi_ref[:, c0:c0 + FF_CHUNK], preferred_element_type=F32)
            up = jnp.dot(xb, wi_ref[:, D_FF + c0:D_FF + c0 + FF_CHUNK], preferred_element_type=F32)
            h = _silu(gate) * up
            acc = acc + jnp.dot(h.astype(BF16), wo_ref[c0:c0 + FF_CHUNK, :], preferred_element_type=F32)
        o_ref[...] = _layernorm(acc, g2_ref[...], b2_ref[...])

    _per_stream(n_p, body, (xp_ref, oap_ref, obp_ref, yp_ref), (xs_ref, oas_ref, obs_ref, ys_ref))


def _mix_ffn(xp, xs, oap, oas, obp, obs, wm_all, g1_all, b1_all, wi_all, wo_all, g2_all, b2_all, layer, alpha):
    ts = TM // (S_TILE // DEC_SEQ)
    n_p, n_s = xp.shape[0] // TM, xs.shape[0] // ts
    vec = lambda: pl.BlockSpec((None, 1, D_MODEL), lambda i: (layer, 0, 0))
    weight = lambda rows, cols: pl.BlockSpec((None, rows, cols), lambda i: (layer, 0, 0),
                                             pipeline_mode=pl.Buffered(1))
    hbm = lambda: pl.BlockSpec(memory_space=pl.ANY)
    return pl.pallas_call(
        functools.partial(_mix_ffn_kernel, alpha, n_p, layer),
        name="mix_ffn",
        grid=(n_p + n_s,),
        in_specs=[*_two_stream_specs(D_MODEL, n_p, ts), *_two_stream_specs(A_WIDTH, n_p),
                  *_two_stream_specs(B_WIDTH, n_p),
                  weight(D_MODEL, D_MODEL), vec(), vec(),
                  hbm(), hbm(), vec(), vec()],
        out_specs=list(_two_stream_specs(D_MODEL, n_p, ts)),
        out_shape=[jax.ShapeDtypeStruct(xp.shape, F32), jax.ShapeDtypeStruct(xs.shape, F32)],
        scratch_shapes=[pltpu.VMEM((D_MODEL, 2 * D_FF), BF16),
                        pltpu.VMEM((D_FF, D_MODEL), BF16),
                        pltpu.VMEM((2, D_MODEL, W_STAGE_COLS), F32),
                        pltpu.VMEM((2, FF_CHUNK, D_MODEL), F32),
                        pltpu.SemaphoreType.DMA((2,)),
                        pltpu.SemaphoreType.DMA((2,))],
        compiler_params=_params(1),
    )(xp, xs, oap, oas, obp, obs, wm_all, g1_all, b1_all, wi_all, wo_all, g2_all, b2_all)


def _conv_silu(x, cw):
    y = cw[3:4, :] * x
    for j in range(1, CONV_WIDTH):
        y = y + cw[CONV_WIDTH - 1 - j:CONV_WIDTH - j, :] * pltpu.roll(x, j, axis=0)
    return _silu(y)


def _gates(ba, alog, dtb):
    beta = _sigmoid(ba)
    g = -jnp.exp(alog) * _softplus(ba + dtb)
    return beta, g


def _cumsum_rows(g, c):
    rin = lax.broadcasted_iota(jnp.int32, g.shape, 0) & (c - 1)
    s = 1
    while s < c:
        g = g + jnp.where(rin >= s, pltpu.roll(g, s, axis=0), 0.0)
        s *= 2
    return g


class _Masks(NamedTuple):
    causal: jax.Array
    neg_strict: jax.Array
    eye: jax.Array
    unit: tuple


def _group_masks(c):
    r = LANES
    shift = int(math.log2(c))
    ri = lax.broadcasted_iota(jnp.int32, (r, r), 0)
    ci = lax.broadcasted_iota(jnp.int32, (r, r), 1)
    same = (ri >> shift) == (ci >> shift)
    return _Masks(
        causal=jnp.where(same & (ri >= ci), 1.0, 0.0),
        neg_strict=jnp.where(same & (ri > ci), -1.0, 0.0),
        eye=jnp.where(ri == ci, 1.0, 0.0),
        unit=tuple(jnp.where((ci >> shift) == i, 1.0, 0.0) for i in range(r // c)))


def _gdn_prepare(groups, c, mk):
    r = LANES
    n_units = r // c
    n_factors = int(math.log2(c))
    each = lambda f, *lists: [f(*a) for a in zip(*lists)]

    q, k, v, beta, gcol = (list(t) for t in zip(*groups))
    qn = each(lambda t: t * (lax.rsqrt(jnp.sum(t * t, axis=-1, keepdims=True) + EPS) * (DK_A ** -0.5)), q)
    kn = each(lambda t: t * lax.rsqrt(jnp.sum(t * t, axis=-1, keepdims=True) + EPS), k)
    gc = each(lambda t: jnp.broadcast_to(t, (r, r)), gcol)
    e = each(lambda t: jnp.exp(jnp.minimum(t - t.T, 0.0)), gc)
    kb = each(lambda a, b: a * b, kn, beta)
    kq = each(lambda a, b, d: _dot_nt(jnp.concatenate([a, b], axis=0), d), kb, qn, kn)
    bk = each(lambda a, b: a[:r] * (b * mk.neg_strict), kq, e)
    aqk = each(lambda a, b: a[r:] * (b * mk.causal), kq, e)

    p = each(lambda t: mk.eye + t, bk)
    bk = each(lambda t: _dot(t, t), bk)
    for _ in range(n_factors - 2):
        st = each(lambda a, b: _dot(jnp.concatenate([a, b], axis=0), b), p, bk)
        p = each(lambda a, b: a + b[:r], p, st)
        bk = each(lambda t: t[r:], st)
    p = each(lambda a, b: a + _dot(a, b), p, bk)

    eg = each(jnp.exp, gc)
    uw = each(lambda a, b, d, f, h: _dot(a, jnp.concatenate([b * d, f * h], axis=1)), p, v, beta, kb, eg)
    qd = each(lambda a, b: a * b, qn, eg)

    out = []
    for gi in range(len(groups)):
        g_last = [gc[gi][(i + 1) * c - 1:(i + 1) * c, :] for i in range(n_units)]
        gl = jnp.concatenate([jnp.broadcast_to(t, (c, r)) for t in g_last], axis=0)
        kdt = (kn[gi] * jnp.exp(gl - gc[gi])).T
        u = uw[gi][:, :DV_A]
        w = uw[gi][:, DV_A:]
        wq = jnp.concatenate([jnp.concatenate([w[i * c:(i + 1) * c], qd[gi][i * c:(i + 1) * c]], axis=0)
                              for i in range(n_units)], axis=0).astype(BF16)
        lhs = jnp.concatenate([aqk[gi]] + [kdt * mk.unit[i] for i in range(n_units)], axis=0).astype(BF16)
        out.append((u, wq, lhs, [jnp.exp(t) for t in g_last]))
    return out


def _gdn_apply(items, nw, c):
    r = LANES
    n_units = r // c
    res1 = [[jnp.dot(wq[2 * c * i:2 * c * (i + 1)], st[i].astype(BF16), preferred_element_type=F32)
             for i in range(n_units)] for _, wq, _, _, st, _ in items]
    vn = [jnp.concatenate([it[0][i * c:(i + 1) * c] - r1[i][:c] for i in range(n_units)], axis=0)
          for it, r1 in zip(items, res1)]
    res2 = [jnp.dot(it[2], v.astype(BF16), preferred_element_type=F32) for it, v in zip(items, vn)]
    out = []
    for (_, _, _, egl, st, z), r1, r2 in zip(items, res1, res2):
        o = jnp.concatenate([r1[i][c:] for i in range(n_units)], axis=0) + r2[:r]
        s_new = [st[i] * egl[i] + r2[r * (i + 1):r * (i + 2)] for i in range(n_units)]
        out.append((o * lax.rsqrt(jnp.mean(o * o, axis=-1, keepdims=True) + EPS) * nw * _silu(z), s_new))
    return out


def _gdn_prompt_kernel(n_seg, prev_ref, qkv_ref, z_ref, ba_ref, cw_ref, alog_ref, dtb_ref, nw_ref,
                       o_ref, s_ref, conv_ref, s_scr, u_scr, wq_scr, lhs_scr, egl_scr):
    seg = pl.program_id(1)
    n_pairs = H_A // 2

    @pl.when(seg == 0)
    def _():
        s_scr[...] = jnp.zeros_like(s_scr)

    cw = cw_ref[...]
    alog = alog_ref[...]
    dtb = dtb_ref[...]
    nw = nw_ref[...]
    mk = _group_masks(CHUNK)

    def prepare(it, carry):
        groups, where = [], []
        for j in range(PREP_CHUNKS):
            ci = it * PREP_CHUNKS + j
            r0 = pl.multiple_of(ci * CHUNK, CHUNK)
            for q in range(N_SEQ):
                before = jnp.where(ci > 0, qkv_ref[q, pl.ds(pl.multiple_of(jnp.maximum(r0 - SUBLANES, 0), SUBLANES),
                                                            SUBLANES), :],
                                   jnp.where(seg > 0, prev_ref[q], 0.0))
                y = _conv_silu(jnp.concatenate([before, qkv_ref[q, pl.ds(r0, CHUNK), :]], axis=0), cw)[SUBLANES:]
                beta_all, g_all = _gates(ba_ref[q, pl.ds(r0, CHUNK), :], alog, dtb)
                gc_all = _cumsum_rows(g_all, CHUNK)
                for pair in range(n_pairs):
                    heads = (2 * pair, 2 * pair + 1)
                    cols = lambda base: jnp.concatenate(
                        [y[:, base + h * LANES:base + (h + 1) * LANES] for h in heads], axis=0)
                    beta = jnp.concatenate([beta_all[:, h:h + 1] for h in heads], axis=0)
                    gcol = jnp.concatenate([gc_all[:, H_A + h:H_A + h + 1] for h in heads], axis=0)
                    groups.append((cols(0), cols(H_A * DK_A), cols(2 * H_A * DK_A), beta, gcol))
                    where.append((ci, q, pair))
        for (ci, q, pair), (u, wq, lhs, egl) in zip(where, _gdn_prepare(groups, CHUNK, mk)):
            u_scr[ci, q, pair] = u
            wq_scr[ci, q, pair] = wq
            lhs_scr[ci, q, pair] = lhs
            for i in range(PAIR_UNITS):
                egl_scr[ci, q, pair, i:i + 1, :] = egl[i]
        return carry

    lax.fori_loop(0, N_CHUNK // PREP_CHUNKS, prepare, 0)

    def scan(ci, carry):
        r0 = pl.multiple_of(ci * CHUNK, CHUNK)
        items, where = [], []
        for q in range(N_SEQ):
            z = z_ref[q, pl.ds(r0, CHUNK), :]
            for pair in range(n_pairs):
                heads = (2 * pair, 2 * pair + 1)
                zz = jnp.concatenate([z[:, h * DV_A:(h + 1) * DV_A] for h in heads], axis=0)
                egl = [egl_scr[ci, q, pair, i:i + 1, :] for i in range(PAIR_UNITS)]
                items.append((u_scr[ci, q, pair], wq_scr[ci, q, pair], lhs_scr[ci, q, pair], egl,
                              [s_scr[q, h] for h in heads], zz))
                where.append((q, heads))
        for (q, heads), (on, s_new) in zip(where, _gdn_apply(items, nw, CHUNK)):
            for i, h in enumerate(heads):
                o_ref[q, pl.ds(r0, CHUNK), h * DV_A:(h + 1) * DV_A] = on[i * CHUNK:(i + 1) * CHUNK]
                s_scr[q, h] = s_new[i]
        return carry

    lax.fori_loop(0, N_CHUNK, scan, 0)

    @pl.when(seg == n_seg - 1)
    def _():
        s_ref[:, 0] = s_scr[...]
        conv_ref[:, 0] = qkv_ref[:, SEG - SUBLANES:SEG, :]


def _gdn_prompt(proj, n_batch, seq, cw_all, alog_all, dtb_all, nw_all, layer):
    n_seg = seq // SEG
    seg8 = SEG // SUBLANES
    seq8 = seq // SUBLANES
    n_pairs = H_A // 2
    nb = n_batch // N_SEQ
    proj3 = proj.reshape(N_SEQ, nb * seq, PROJ_PAD)
    blk = lambda width, col: pl.BlockSpec((N_SEQ, SEG, width), lambda bb, s: (0, bb * n_seg + s, col))
    vec = lambda: pl.BlockSpec((None, 1, LANES), lambda bb, s: (layer, 0, 0))
    o, s_out, conv = pl.pallas_call(
        functools.partial(_gdn_prompt_kernel, n_seg),
        name="gdn_prompt",
        grid=(nb, n_seg),
        in_specs=[pl.BlockSpec((N_SEQ, SUBLANES, CONV_DIM),
                               lambda bb, s: (0, jnp.maximum(bb * seq8 + s * seg8 - 1, 0), 0)),
                  blk(CONV_DIM, 0), blk(A_WIDTH, COL_Z // A_WIDTH), blk(LANES, COL_BA // LANES),
                  pl.BlockSpec((None, CONV_WIDTH, CONV_DIM), lambda bb, s: (layer, 0, 0)),
                  vec(), vec(), vec()],
        out_specs=[blk(A_WIDTH, 0),
                   pl.BlockSpec((N_SEQ, 1, H_A, DK_A, DV_A), lambda bb, s: (0, bb, 0, 0, 0)),
                   pl.BlockSpec((N_SEQ, 1, SUBLANES, CONV_DIM), lambda bb, s: (0, bb, 0, 0))],
        out_shape=[jax.ShapeDtypeStruct((N_SEQ, nb * seq, A_WIDTH), F32),
                   jax.ShapeDtypeStruct((N_SEQ, nb, H_A, DK_A, DV_A), F32),
                   jax.ShapeDtypeStruct((N_SEQ, nb, SUBLANES, CONV_DIM), F32)],
        scratch_shapes=[pltpu.VMEM((N_SEQ, H_A, DK_A, DV_A), F32),
                        pltpu.VMEM((N_CHUNK, N_SEQ, n_pairs, LANES, DV_A), F32),
                        pltpu.VMEM((N_CHUNK, N_SEQ, n_pairs, 2 * LANES, DK_A), BF16),
                        pltpu.VMEM((N_CHUNK, N_SEQ, n_pairs, 3 * LANES, LANES), BF16),
                        pltpu.VMEM((N_CHUNK, N_SEQ, n_pairs, SUBLANES, LANES), F32)],
        compiler_params=_params(2),
    )(proj3, proj3, proj3, proj3, cw_all, alog_all, dtb_all, nw_all)
    return (o.reshape(n_batch * seq, A_WIDTH), s_out.reshape(n_batch, H_A, DK_A, DV_A),
            conv.reshape(n_batch * SUBLANES, CONV_DIM))


def _token_rows(n_rows):
    row = lax.broadcasted_iota(jnp.int32, (n_rows, 1), 0) & (S_TILE - 1)
    return (row >= ROW0) & (row < ROW0 + DEC_SEQ)


def _gdn_sample_kernel(qkv_ref, hist_ref, z_ref, ba_ref, cw_ref, alog_ref, dtb_ref, nw_ref, s_in_ref, s_acc_ref,
                       o_ref, s_out_ref, conv_ref):
    del s_acc_ref
    rows = SB * S_TILE
    valid = _token_rows(rows)
    x = jnp.where(valid, qkv_ref[...], 0.0) + hist_ref[...]
    conv_ref[...] = pltpu.roll(x, rows - DEC_SEQ, axis=0)
    y = _conv_silu(x, cw_ref[...])
    beta_all, g_all = _gates(ba_ref[...], alog_ref[...], dtb_ref[...])
    beta_all = jnp.where(valid, beta_all, 0.0)
    gc_all = _cumsum_rows(jnp.where(valid, g_all, 0.0), S_TILE)
    z = z_ref[...]
    ym = jnp.where(valid, y, 0.0)

    all_units = [(b, h) for b in range(SB) for h in range(H_A)]
    per_group = LANES // S_TILE
    unit_groups = [all_units[i:i + per_group] for i in range(0, len(all_units), per_group)]

    def stack(units, src, base, width=LANES):
        return jnp.concatenate(
            [src[b * S_TILE:(b + 1) * S_TILE, base + h * width:base + (h + 1) * width] for b, h in units], axis=0)

    mk = _group_masks(S_TILE)
    prepared = _gdn_prepare(
        [(stack(us, y, 0), stack(us, ym, H_A * DK_A), stack(us, ym, 2 * H_A * DK_A),
          stack(us, beta_all, 0, 1), stack(us, gc_all, H_A, 1)) for us in unit_groups], S_TILE, mk)
    items = [(u, wq, lhs, egl, [s_in_ref[b, h] for b, h in us], stack(us, z, 0))
             for us, (u, wq, lhs, egl) in zip(unit_groups, prepared)]
    for us, (on, s_new) in zip(unit_groups, _gdn_apply(items, nw_ref[...], S_TILE)):
        for i, (b, h) in enumerate(us):
            o_ref[b * S_TILE:(b + 1) * S_TILE, h * DV_A:(h + 1) * DV_A] = on[i * S_TILE:(i + 1) * S_TILE]
            s_out_ref[b, h] = s_new[i]


def _gdn_sample(proj, hist, state_all, cw_all, alog_all, dtb_all, nw_all, layer, s_acc):
    rows = SB * S_TILE
    n_seq = proj.shape[0] // S_TILE
    depth = state_all.shape[0]
    vec = lambda: pl.BlockSpec((None, 1, LANES), lambda i: (layer, 0, 0))
    in_specs = [pl.BlockSpec((rows, CONV_DIM), lambda i: (i, 0)),
                pl.BlockSpec((None, rows, CONV_DIM), lambda i: (layer, i, 0)),
                pl.BlockSpec((rows, A_WIDTH), lambda i: (i, COL_Z // A_WIDTH)),
                pl.BlockSpec((rows, LANES), lambda i: (i, COL_BA // LANES)),
                pl.BlockSpec((None, CONV_WIDTH, CONV_DIM), lambda i: (layer, 0, 0)),
                vec(), vec(), vec(),
                pl.BlockSpec((None, SB, H_A, DK_A, DV_A), lambda i: (layer, i, 0, 0, 0))]
    in_specs.append(pl.BlockSpec(memory_space=pl.ANY))
    args = [proj, hist, proj, proj, cw_all, alog_all, dtb_all, nw_all, state_all, s_acc]
    aliases = {len(args) - 1: 1}
    return pl.pallas_call(
        _gdn_sample_kernel,
        name="gdn_sample",
        grid=(n_seq // SB,),
        in_specs=in_specs,
        out_specs=[pl.BlockSpec((rows, A_WIDTH), lambda i: (i, 0)),
                   pl.BlockSpec((None, SB, H_A, DK_A, DV_A), lambda i: (layer, i, 0, 0, 0)),
                   pl.BlockSpec((rows, CONV_DIM), lambda i: (i, 0))],
        out_shape=[jax.ShapeDtypeStruct((n_seq * S_TILE, A_WIDTH), F32),
                   jax.ShapeDtypeStruct((depth, n_seq, H_A, DK_A, DV_A), F32),
                   jax.ShapeDtypeStruct((n_seq * S_TILE, CONV_DIM), F32)],
        input_output_aliases=aliases,
        compiler_params=_params(1),
    )(*args)


def _swa_heads(items, mask, sink_ref, q_rows):
    lane_head = lax.broadcasted_iota(jnp.int32, (1, LANES), 1) >> int(math.log2(HD_B))
    chains = [(it, hk) for it in range(len(items)) for hk in range(H_KVB)]
    each = lambda f, *lists: [f(*a) for a in zip(*lists)]
    qcat = [jnp.concatenate([q[:, gi * LANES:(gi + 1) * LANES] for gi in range(GQA_GROUP)], axis=0)
            * (HD_B ** -0.5) for q, _, _ in items]
    sk_head = [jnp.concatenate(
        [jnp.broadcast_to(sink_ref[hk * GQA_GROUP + gi:hk * GQA_GROUP + gi + 1, 0:1], (q_rows, 1))
         for gi in range(GQA_GROUP)], axis=0) for hk in range(H_KVB)]
    sk = [sk_head[hk] for _, hk in chains]
    kh = [jnp.where(lane_head == hk, items[it][1], 0.0) for it, hk in chains]
    vh = [jnp.where(lane_head == hk, items[it][2], 0.0) for it, hk in chains]
    s = [jnp.where(mask, _dot_nt(qcat[it], kh[ch]), NEG) for ch, (it, _) in enumerate(chains)]
    m = each(lambda a, b: jnp.maximum(jnp.max(a, axis=-1, keepdims=True), b), s, sk)
    p = each(lambda a, b: jnp.exp(a - b), s, m)
    den = each(lambda a, b, d: jnp.sum(a, axis=-1, keepdims=True) + jnp.exp(b - d), p, sk, m)
    o = each(lambda a, b, d: _dot(a, b) * (1.0 / d), p, vh, den)
    outs = []
    for it in range(len(items)):
        tot = o[H_KVB * it]
        for hk in range(1, H_KVB):
            tot = tot + o[H_KVB * it + hk]
        outs.append(jnp.concatenate([tot[gi * q_rows:(gi + 1) * q_rows] for gi in range(GQA_GROUP)], axis=1))
    return outs


def _swa_prompt_kernel(q_ref, kc_ref, kp_ref, vc_ref, vp_ref, sink_ref, o_ref):
    n0 = pl.program_id(1) * SWA_QB
    keys = 2 * WINDOW
    k_ext = jnp.concatenate([kp_ref[...], kc_ref[...]], axis=0)
    v_ext = jnp.concatenate([vp_ref[...], vc_ref[...]], axis=0)
    c = lax.broadcasted_iota(jnp.int32, (keys, 2 * WINDOW), 0)
    r = lax.broadcasted_iota(jnp.int32, (keys, 2 * WINDOW), 1) & (WINDOW - 1)
    band = jnp.where((c > r) & (c <= r + WINDOW), 0.0, NEG)
    shift = int(math.log2(HD_B))
    lane_head = lax.broadcasted_iota(jnp.int32, (1, LANES), 1) >> shift
    row = lax.broadcasted_iota(jnp.int32, (LANES, 1), 0)
    row_head = row >> shift
    each = lambda f, *lists: [f(*a) for a in zip(*lists)]
    n_gp = GQA_GROUP // 2

    chains = [(j, hk, gp) for j in range(SWA_QB) for hk in range(H_KVB) for gp in range(n_gp)]
    bias = [jnp.where(c + n0 * WINDOW >= WINDOW, band, NEG)] + [band] * (SWA_QB - 1)
    kh, vth, qpair = {}, {}, {}
    for j in range(SWA_QB):
        k_all = k_ext[j * WINDOW:(j + 2) * WINDOW]
        vt = v_ext[j * WINDOW:(j + 2) * WINDOW].T
        qs = q_ref[j * WINDOW:(j + 1) * WINDOW, :] * (HD_B ** -0.5 * LOG2E)
        for hk in range(H_KVB):
            kh[j, hk] = jnp.where(lane_head == hk, k_all, 0.0).astype(BF16)
            vth[j, hk] = jnp.where(row_head == hk, vt, jnp.where(row == (1 - hk) * HD_B, 1.0, 0.0)).astype(BF16)
        for gp in range(n_gp):
            qpair[j, gp] = jnp.concatenate([qs[:, (2 * gp) * LANES:(2 * gp + 1) * LANES],
                                            qs[:, (2 * gp + 1) * LANES:(2 * gp + 2) * LANES]],
                                           axis=0).astype(BF16)
    ones_row = [(1 - hk) * HD_B for _, hk, _ in chains]
    sk = [jnp.concatenate([sink_ref[hk * GQA_GROUP + 2 * gp + i:hk * GQA_GROUP + 2 * gp + i + 1, :]
                           for i in range(2)], axis=1) * LOG2E for _, hk, gp in chains]
    st = [_dot_nt(kh[j, hk], qpair[j, gp]) + bias[j] for j, hk, gp in chains]
    m = each(lambda a, b: jnp.maximum(jnp.max(a, axis=0, keepdims=True), b), st, sk)
    pt = each(lambda a, b: jnp.exp2(a - b).astype(BF16), st, m)
    ot = [jnp.dot(vth[j, hk], pt[ch], preferred_element_type=F32) for ch, (j, hk, _) in enumerate(chains)]
    den = each(lambda a, i, b, d: a[i:i + 1, :] + jnp.exp2(b - d), ot, ones_row, sk, m)
    ot = [jnp.where(row_head == hk, ot[ch] * (1.0 / den[ch]), 0.0) for ch, (_, hk, _) in enumerate(chains)]
    for j in range(SWA_QB):
        tiles = []
        for gp in range(n_gp):
            tot = ot[chains.index((j, 0, gp))]
            for hk in range(1, H_KVB):
                tot = tot + ot[chains.index((j, hk, gp))]
            tiles += [tot[:, :WINDOW].T, tot[:, WINDOW:].T]
        o_ref[j * WINDOW:(j + 1) * WINDOW, :] = jnp.concatenate(tiles, axis=1)


def _swa_prompt(proj, n_batch, seq, sink_all, layer):
    nb = seq // WINDOW
    ns = nb // SWA_QB
    qrows = SWA_QB * WINDOW
    cur = lambda col: (lambda b, n: (b * ns + n, col))
    prev = lambda col: (lambda b, n: (b * nb + jnp.maximum(n * SWA_QB - 1, 0), col))
    return pl.pallas_call(
        _swa_prompt_kernel,
        name="swa_prompt",
        grid=(n_batch, ns),
        in_specs=[pl.BlockSpec((qrows, B_WIDTH), cur(COL_QB // B_WIDTH)),
                  pl.BlockSpec((qrows, KV_WIDTH), cur(COL_KB // KV_WIDTH)),
                  pl.BlockSpec((WINDOW, KV_WIDTH), prev(COL_KB // KV_WIDTH)),
                  pl.BlockSpec((qrows, KV_WIDTH), cur(COL_VB // KV_WIDTH)),
                  pl.BlockSpec((WINDOW, KV_WIDTH), prev(COL_VB // KV_WIDTH)),
                  pl.BlockSpec((None, H_QB, LANES), lambda b, n: (layer, 0, 0))],
        out_specs=pl.BlockSpec((qrows, B_WIDTH), lambda b, n: (b * ns + n, 0)),
        out_shape=jax.ShapeDtypeStruct((n_batch * seq, B_WIDTH), F32),
        compiler_params=_params(2),
    )(proj, proj, proj, proj, proj, sink_all)


def _shift_cache(cache, new_tile):
    rolled = pltpu.roll(cache, WINDOW - DEC_SEQ, axis=0)
    moved = pltpu.roll(new_tile, S_TILE - DEC_SEQ - ROW0, axis=0)
    row = lax.broadcasted_iota(jnp.int32, (S_TILE, 1), 0)
    tail = jnp.where(row >= S_TILE - DEC_SEQ, moved, rolled[WINDOW - S_TILE:])
    return jnp.concatenate([rolled[:WINDOW - S_TILE], tail], axis=0)


def _swa_sample_kernel(q_ref, k_ref, v_ref, kc_ref, vc_ref, sink_ref, k_acc_ref, v_acc_ref, o_ref, ko_ref, vo_ref):
    del k_acc_ref, v_acc_ref
    shape = (GQA_GROUP * S_TILE, 2 * WINDOW)
    r = (lax.broadcasted_iota(jnp.int32, shape, 0) & (S_TILE - 1)) - ROW0
    c = lax.broadcasted_iota(jnp.int32, shape, 1)
    j = c - WINDOW - ROW0
    mask = ((c < WINDOW) & (c > r)) | ((j >= 0) & (j < DEC_SEQ) & (j <= r))
    pad = jnp.zeros((WINDOW - S_TILE, KV_WIDTH), F32)
    items = []
    for b in range(SB):
        rows = slice(b * S_TILE, (b + 1) * S_TILE)
        k_new = k_ref[rows, :]
        v_new = v_ref[rows, :]
        items.append((q_ref[rows, :], jnp.concatenate([kc_ref[b], k_new, pad], axis=0),
                      jnp.concatenate([vc_ref[b], v_new, pad], axis=0)))
        ko_ref[b] = _shift_cache(kc_ref[b], k_new)
        vo_ref[b] = _shift_cache(vc_ref[b], v_new)
    for b, o in enumerate(_swa_heads(items, mask, sink_ref, S_TILE)):
        o_ref[b * S_TILE:(b + 1) * S_TILE, :] = o


def _swa_sample(proj, kc_all, vc_all, sink_all, layer, k_acc, v_acc):
    rows = SB * S_TILE
    n_seq = proj.shape[0] // S_TILE
    depth = kc_all.shape[0]
    cache = lambda: pl.BlockSpec((None, SB, WINDOW, KV_WIDTH), lambda i: (layer, i, 0, 0))
    in_specs = [pl.BlockSpec((rows, B_WIDTH), lambda i: (i, COL_QB // B_WIDTH)),
                pl.BlockSpec((rows, KV_WIDTH), lambda i: (i, COL_KB // KV_WIDTH)),
                pl.BlockSpec((rows, KV_WIDTH), lambda i: (i, COL_VB // KV_WIDTH)),
                cache(), cache(),
                pl.BlockSpec((None, H_QB, LANES), lambda i: (layer, 0, 0))]
    in_specs += [pl.BlockSpec(memory_space=pl.ANY)] * 2
    args = [proj, proj, proj, kc_all, vc_all, sink_all, k_acc, v_acc]
    aliases = {len(args) - 2: 1, len(args) - 1: 2}
    return pl.pallas_call(
        _swa_sample_kernel,
        name="swa_sample",
        grid=(n_seq // SB,),
        in_specs=in_specs,
        out_specs=[pl.BlockSpec((rows, B_WIDTH), lambda i: (i, 0)), cache(), cache()],
        out_shape=[jax.ShapeDtypeStruct((n_seq * S_TILE, B_WIDTH), F32),
                   jax.ShapeDtypeStruct((depth, n_seq, WINDOW, KV_WIDTH), F32),
                   jax.ShapeDtypeStruct((depth, n_seq, WINDOW, KV_WIDTH), F32)],
        input_output_aliases=aliases,
        compiler_params=_params(1),
    )(*args)


def kernel(x_prompt, x_sample, state_delta, state_conv, cache_swa_k, cache_swa_v, w_in, conv_w, a_log,
           dt_bias, norm_a_w, sinks, w_out, ln1_g, ln1_b, w_ffn_in, w_ffn_out, ln2_g, ln2_b):
    depth = w_in.shape[0]
    n_batch, seq, _ = x_prompt.shape
    n_dec, dec_seq, _ = x_sample.shape
    assert dec_seq == DEC_SEQ and seq % SEG == 0 and n_dec % SB == 0 and n_batch % N_SEQ == 0
    assert (n_batch * seq) % TM == 0 and (n_dec * S_TILE) % TM == 0
    assert cache_swa_k.shape[2] == WINDOW
    alpha = (2 * depth) ** 0.25

    c0 = CONV_DIM + A_WIDTH
    q0 = c0 + 2 * H_A
    order = jnp.array(QB_HEAD_ORDER)
    w_q = w_in[:, :, q0:q0 + B_WIDTH].reshape(depth, D_MODEL, H_QB, HD_B)[:, :, order]
    w_in_r = jnp.concatenate(
        [w_in[:, :, :c0], w_q.reshape(depth, D_MODEL, B_WIDTH), w_in[:, :, q0 + B_WIDTH:],
         w_in[:, :, c0:q0], jnp.zeros((depth, D_MODEL, LANES - 2 * H_A), w_in.dtype)], axis=-1).astype(BF16)
    w_out_bq = w_out[:, A_WIDTH:].reshape(depth, H_QB, HD_B, D_MODEL)[:, order].reshape(depth, B_WIDTH, D_MODEL)
    w_out_b = jnp.concatenate([w_out[:, :A_WIDTH], w_out_bq], axis=1).astype(BF16)
    w_ffn_in_b, w_ffn_out_b = w_ffn_in, w_ffn_out
    lane_pad = lambda t: jnp.pad(t, ((0, 0), (H_A, LANES - 2 * H_A)))[:, None, :]
    alog_v = lane_pad(a_log)
    dtb_v = lane_pad(dt_bias)
    nw_v = norm_a_w[:, None, :]
    sink_v = jnp.broadcast_to(sinks[:, :, None], (depth, H_QB, LANES))
    ln1_g3, ln1_b3, ln2_g3, ln2_b3 = (t[:, None, :] for t in (ln1_g, ln1_b, ln2_g, ln2_b))

    hist = jnp.pad(state_conv, ((0, 0), (0, 0), (0, S_TILE - (CONV_WIDTH - 1)), (0, 0)))
    hist = hist.reshape(depth, n_dec * S_TILE, CONV_DIM)
    kc = cache_swa_k.reshape(depth, n_dec, WINDOW, KV_WIDTH)
    vc = cache_swa_v.reshape(depth, n_dec, WINDOW, KV_WIDTH)

    xp = x_prompt.reshape(n_batch * seq, D_MODEL)
    xs = x_sample.reshape(n_dec * DEC_SEQ, D_MODEL)

    outs = [[] for _ in range(8)]
    s_acc = jnp.zeros((depth, n_dec, H_A, DK_A, DV_A), F32)
    k_acc = jnp.zeros((depth, n_dec, WINDOW, KV_WIDTH), F32)
    v_acc = jnp.zeros((depth, n_dec, WINDOW, KV_WIDTH), F32)
    for l in range(depth):
        proj_p, proj_s = _in_proj(xp, xs, w_in_r, l)
        oa_p, s_p, conv_p = _gdn_prompt(proj_p, n_batch, seq, conv_w, alog_v, dtb_v, nw_v, l)
        ob_p = _swa_prompt(proj_p, n_batch, seq, sink_v, l)
        oa_s, s_acc, conv_s = _gdn_sample(proj_s, hist, state_delta, conv_w, alog_v, dtb_v, nw_v, l, s_acc)
        ob_s, k_acc, v_acc = _swa_sample(proj_s, kc, vc, sink_v, l, k_acc, v_acc)
        xp, xs = _mix_ffn(xp, xs, oa_p, oa_s, ob_p, ob_s, w_out_b, ln1_g3, ln1_b3, w_ffn_in_b, w_ffn_out_b,
                          ln2_g3, ln2_b3, l, alpha)
        proj3 = proj_p.reshape(n_batch, seq, PROJ_PAD)
        outs[0].append(s_p)
        outs[1].append(conv_p.reshape(n_batch, SUBLANES, CONV_DIM)[:, SUBLANES - (CONV_WIDTH - 1):])
        outs[2].append(proj3[:, seq - WINDOW:, COL_KB:COL_KB + KV_WIDTH].reshape(n_batch, WINDOW, H_KVB, HD_B))
        outs[3].append(proj3[:, seq - WINDOW:, COL_VB:COL_VB + KV_WIDTH].reshape(n_batch, WINDOW, H_KVB, HD_B))
        outs[5].append(conv_s.reshape(n_dec, S_TILE, CONV_DIM)[:, :CONV_WIDTH - 1])

    y_prompt = xp.reshape(n_batch, seq, D_MODEL)
    y_sample = xs.reshape(n_dec, DEC_SEQ, D_MODEL)
    stacked = [jnp.stack(o) if o else None for o in outs]
    stacked[4] = s_acc
    stacked[6] = k_acc.reshape(depth, n_dec, WINDOW, H_KVB, HD_B)
    stacked[7] = v_acc.reshape(depth, n_dec, WINDOW, H_KVB, HD_B)
    return (y_prompt, y_sample) + tuple(stacked)
```

```python
import functools
import math
from typing import NamedTuple

import jax
import jax.numpy as jnp
from jax import lax
from jax.experimental import pallas as pl
from jax.experimental.pallas import tpu as pltpu

F32 = jnp.float32
BF16 = jnp.bfloat16

D_MODEL = 1024
H_A = 4
DK_A = 128
DV_A = 128
CONV_WIDTH = 4
CONV_DIM = 2 * H_A * DK_A + H_A * DV_A
A_WIDTH = H_A * DV_A
CHUNK = 64
HD_B = 64
H_QB = 8
H_KVB = 2
GQA_GROUP = H_QB // H_KVB
B_WIDTH = H_QB * HD_B
KV_WIDTH = H_KVB * HD_B
WINDOW = 128
D_FF = 2816
EPS = 1e-6

LANES = 128
SUBLANES = 8

COL_Z = CONV_DIM
COL_QB = COL_Z + A_WIDTH
COL_KB = COL_QB + B_WIDTH
COL_VB = COL_KB + KV_WIDTH
COL_BA = COL_VB + KV_WIDTH
PROJ_PAD = COL_BA + LANES
QB_HEAD_ORDER = tuple(hk * GQA_GROUP + g for g in range(GQA_GROUP) for hk in range(H_KVB))

S_TILE = SUBLANES
ROW0 = CONV_WIDTH - 1
DEC_SEQ = 4
SB = 32

TM = 512
SEG = 128
N_CHUNK = SEG // CHUNK
PAIR_UNITS = LANES // CHUNK
PREP_CHUNKS = 2
N_SEQ = 8
FF_CHUNK = 256
PROJ_COLS = 512
SWA_QB = 8
NEG = -1e30
LOG2E = math.log2(math.e)
VMEM_LIMIT = 56 * 1024 * 1024


def _params(n_axes, vmem=None):
    return pltpu.CompilerParams(
        dimension_semantics=("arbitrary",) * n_axes,
        vmem_limit_bytes=vmem if vmem is not None else VMEM_LIMIT)


def _dot(a, b):
    return jnp.dot(a.astype(BF16), b.astype(BF16), preferred_element_type=F32)


def _dot_nt(a, b):
    return lax.dot_general(a.astype(BF16), b.astype(BF16), (((1,), (1,)), ((), ())),
                           preferred_element_type=F32)


def _sigmoid(x):
    return 1.0 / (1.0 + jnp.exp(-x))


def _silu(x):
    h = 0.5 * x
    return h + h * jnp.tanh(h)


def _softplus(x):
    return jnp.maximum(x, 0.0) + jnp.log(1.0 + jnp.exp(-jnp.abs(x)))


def _layernorm(h, g, b):
    mu = jnp.mean(h, axis=-1, keepdims=True)
    d = h - mu
    var = jnp.mean(d * d, axis=-1, keepdims=True)
    return d * lax.rsqrt(var + EPS) * g + b


def _two_stream_specs(shape_cols, n_p, sample_rows=None):
    return (pl.BlockSpec((TM, shape_cols), lambda i: (jnp.minimum(i, n_p - 1), 0)),
            pl.BlockSpec((TM if sample_rows is None else sample_rows, shape_cols),
                         lambda i: (jnp.maximum(i - n_p, 0), 0)))


def _per_stream(n_p, body, prompt_refs, sample_refs):
    i = pl.program_id(0)

    @pl.when(i < n_p)
    def _():
        body(False, *prompt_refs)

    @pl.when(i >= n_p)
    def _():
        body(True, *sample_refs)


def _spread_tokens(t):
    rows = t.shape[0]
    token = _token_rows(SUBLANES)
    first = pltpu.roll(t, ROW0, axis=0)
    second = pltpu.roll(t, rows - (DEC_SEQ - ROW0), axis=0)
    tiles = []
    for v in range(rows // SUBLANES):
        sl = slice(v * SUBLANES, (v + 1) * SUBLANES)
        tiles += [jnp.where(token, first[sl], 0.0), jnp.where(token, second[sl], 0.0)]
    return jnp.concatenate(tiles, axis=0)


def _gather_tokens(t):
    rows = t.shape[0]
    low = lax.broadcasted_iota(jnp.int32, (SUBLANES, 1), 0) < DEC_SEQ
    first = pltpu.roll(t, rows - ROW0, axis=0)
    second = pltpu.roll(t, DEC_SEQ - ROW0, axis=0)
    tiles = []
    for v in range(rows // (2 * SUBLANES)):
        a = slice(2 * v * SUBLANES, (2 * v + 1) * SUBLANES)
        b = slice((2 * v + 1) * SUBLANES, (2 * v + 2) * SUBLANES)
        tiles.append(jnp.where(low, first[a], second[b]))
    return jnp.concatenate(tiles, axis=0)


def _in_proj_kernel(n_p, xp_ref, xs_ref, w_ref, op_ref, os_ref):
    def body(sample, x_ref, o_ref):
        xb = x_ref[...].astype(BF16)
        n = o_ref.shape[1]
        for n0 in range(0, n, PROJ_COLS):
            n1 = min(n0 + PROJ_COLS, n)
            res = jnp.dot(xb, w_ref[:, n0:n1], preferred_element_type=F32)
            o_ref[:, n0:n1] = _spread_tokens(res) if sample else res

    _per_stream(n_p, body, (xp_ref, op_ref), (xs_ref, os_ref))


def _in_proj(xp, xs, w_all, layer):
    ts = TM // (S_TILE // DEC_SEQ)
    n_p, n_s = xp.shape[0] // TM, xs.shape[0] // ts
    return pl.pallas_call(
        functools.partial(_in_proj_kernel, n_p),
        name="in_proj",
        grid=(n_p + n_s,),
        in_specs=[*_two_stream_specs(D_MODEL, n_p, ts),
                  pl.BlockSpec((None, D_MODEL, PROJ_PAD), lambda i: (layer, 0, 0))],
        out_specs=list(_two_stream_specs(PROJ_PAD, n_p)),
        out_shape=[jax.ShapeDtypeStruct((xp.shape[0], PROJ_PAD), F32),
                   jax.ShapeDtypeStruct((n_s * TM, PROJ_PAD), F32)],
        compiler_params=_params(1),
    )(xp, xs, w_all)


def _mix_ffn_kernel(alpha, n_p, xp_ref, xs_ref, oap_ref, oas_ref, obp_ref, obs_ref, wm_ref, g1_ref, b1_ref,
                    wi_ref, wo_ref, g2_ref, b2_ref, yp_ref, ys_ref):
    def body(sample, x_ref, oa_ref, ob_ref, o_ref):
        oa, ob = oa_ref[...], ob_ref[...]
        if sample:
            oa, ob = _gather_tokens(oa), _gather_tokens(ob)
        m = (jnp.dot(oa.astype(BF16), wm_ref[0:A_WIDTH, :], preferred_element_type=F32)
             + jnp.dot(ob.astype(BF16), wm_ref[A_WIDTH:, :], preferred_element_type=F32))
        x1 = _layernorm(alpha * x_ref[...] + m, g1_ref[...], b1_ref[...])
        xb = x1.astype(BF16)
        acc = alpha * x1
        for c0 in range(0, D_FF, FF_CHUNK):
            gate = jnp.dot(xb, wi_ref[:, c0:c0 + FF_CHUNK], preferred_element_type=F32)
            up = jnp.dot(xb, wi_ref[:, D_FF + c0:D_FF + c0 + FF_CHUNK], preferred_element_type=F32)
            h = _silu(gate) * up
            acc = acc + jnp.dot(h.astype(BF16), wo_ref[c0:c0 + FF_CHUNK, :], preferred_element_type=F32)
        o_ref[...] = _layernorm(acc, g2_ref[...], b2_ref[...])

    _per_stream(n_p, body, (xp_ref, oap_ref, obp_ref, yp_ref), (xs_ref, oas_ref, obs_ref, ys_ref))


def _mix_ffn(xp, xs, oap, oas, obp, obs, wm_all, g1_all, b1_all, wi_all, wo_all, g2_all, b2_all, layer, alpha):
    ts = TM // (S_TILE // DEC_SEQ)
    n_p, n_s = xp.shape[0] // TM, xs.shape[0] // ts
    vec = lambda: pl.BlockSpec((None, 1, D_MODEL), lambda i: (layer, 0, 0))
    weight = lambda rows, cols: pl.BlockSpec((None, rows, cols), lambda i: (layer, 0, 0),
                                             pipeline_mode=pl.Buffered(1))
    return pl.pallas_call(
        functools.partial(_mix_ffn_kernel, alpha, n_p),
        name="mix_ffn",
        grid=(n_p + n_s,),
        in_specs=[*_two_stream_specs(D_MODEL, n_p, ts), *_two_stream_specs(A_WIDTH, n_p),
                  *_two_stream_specs(B_WIDTH, n_p),
                  weight(D_MODEL, D_MODEL), vec(), vec(),
                  weight(D_MODEL, 2 * D_FF), weight(D_FF, D_MODEL), vec(), vec()],
        out_specs=list(_two_stream_specs(D_MODEL, n_p, ts)),
        out_shape=[jax.ShapeDtypeStruct(xp.shape, F32), jax.ShapeDtypeStruct(xs.shape, F32)],
        compiler_params=_params(1),
    )(xp, xs, oap, oas, obp, obs, wm_all, g1_all, b1_all, wi_all, wo_all, g2_all, b2_all)


def _conv_silu(x, cw):
    y = cw[3:4, :] * x
    for j in range(1, CONV_WIDTH):
        y = y + cw[CONV_WIDTH - 1 - j:CONV_WIDTH - j, :] * pltpu.roll(x, j, axis=0)
    return _silu(y)


def _gates(ba, alog, dtb):
    beta = _sigmoid(ba)
    g = -jnp.exp(alog) * _softplus(ba + dtb)
    return beta, g


def _cumsum_rows(g, c):
    rin = lax.broadcasted_iota(jnp.int32, g.shape, 0) & (c - 1)
    s = 1
    while s < c:
        g = g + jnp.where(rin >= s, pltpu.roll(g, s, axis=0), 0.0)
        s *= 2
    return g


class _Masks(NamedTuple):
    causal: jax.Array
    neg_strict: jax.Array
    eye: jax.Array
    unit: tuple


def _group_masks(c):
    r = LANES
    shift = int(math.log2(c))
    ri = lax.broadcasted_iota(jnp.int32, (r, r), 0)
    ci = lax.broadcasted_iota(jnp.int32, (r, r), 1)
    same = (ri >> shift) == (ci >> shift)
    return _Masks(
        causal=jnp.where(same & (ri >= ci), 1.0, 0.0),
        neg_strict=jnp.where(same & (ri > ci), -1.0, 0.0),
        eye=jnp.where(ri == ci, 1.0, 0.0),
        unit=tuple(jnp.where((ci >> shift) == i, 1.0, 0.0) for i in range(r // c)))


def _gdn_prepare(groups, c, mk):
    r = LANES
    n_units = r // c
    n_factors = int(math.log2(c))
    each = lambda f, *lists: [f(*a) for a in zip(*lists)]

    q, k, v, beta, gcol = (list(t) for t in zip(*groups))
    qn = each(lambda t: t * (lax.rsqrt(jnp.sum(t * t, axis=-1, keepdims=True) + EPS) * (DK_A ** -0.5)), q)
    kn = each(lambda t: t * lax.rsqrt(jnp.sum(t * t, axis=-1, keepdims=True) + EPS), k)
    gc = each(lambda t: jnp.broadcast_to(t, (r, r)), gcol)
    e = each(lambda t: jnp.exp(jnp.minimum(t - t.T, 0.0)), gc)
    kb = each(lambda a, b: a * b, kn, beta)
    kq = each(lambda a, b, d: _dot_nt(jnp.concatenate([a, b], axis=0), d), kb, qn, kn)
    bk = each(lambda a, b: a[:r] * (b * mk.neg_strict), kq, e)
    aqk = each(lambda a, b: a[r:] * (b * mk.causal), kq, e)

    p = each(lambda t: mk.eye + t, bk)
    bk = each(lambda t: _dot(t, t), bk)
    for _ in range(n_factors - 2):
        st = each(lambda a, b: _dot(jnp.concatenate([a, b], axis=0), b), p, bk)
        p = each(lambda a, b: a + b[:r], p, st)
        bk = each(lambda t: t[r:], st)
    p = each(lambda a, b: a + _dot(a, b), p, bk)

    eg = each(jnp.exp, gc)
    uw = each(lambda a, b, d, f, h: _dot(a, jnp.concatenate([b * d, f * h], axis=1)), p, v, beta, kb, eg)
    qd = each(lambda a, b: a * b, qn, eg)

    out = []
    for gi in range(len(groups)):
        g_last = [gc[gi][(i + 1) * c - 1:(i + 1) * c, :] for i in range(n_units)]
        gl = jnp.concatenate([jnp.broadcast_to(t, (c, r)) for t in g_last], axis=0)
        kdt = (kn[gi] * jnp.exp(gl - gc[gi])).T
        u = uw[gi][:, :DV_A]
        w = uw[gi][:, DV_A:]
        wq = jnp.concatenate([jnp.concatenate([w[i * c:(i + 1) * c], qd[gi][i * c:(i + 1) * c]], axis=0)
                              for i in range(n_units)], axis=0).astype(BF16)
        lhs = jnp.concatenate([aqk[gi]] + [kdt * mk.unit[i] for i in range(n_units)], axis=0).astype(BF16)
        out.append((u, wq, lhs, [jnp.exp(t) for t in g_last]))
    return out


def _gdn_apply(items, nw, c):
    r = LANES
    n_units = r // c
    res1 = [[jnp.dot(wq[2 * c * i:2 * c * (i + 1)], st[i].astype(BF16), preferred_element_type=F32)
             for i in range(n_units)] for _, wq, _, _, st, _ in items]
    vn = [jnp.concatenate([it[0][i * c:(i + 1) * c] - r1[i][:c] for i in range(n_units)], axis=0)
          for it, r1 in zip(items, res1)]
    res2 = [jnp.dot(it[2], v.astype(BF16), preferred_element_type=F32) for it, v in zip(items, vn)]
    out = []
    for (_, _, _, egl, st, z), r1, r2 in zip(items, res1, res2):
        o = jnp.concatenate([r1[i][c:] for i in range(n_units)], axis=0) + r2[:r]
        s_new = [st[i] * egl[i] + r2[r * (i + 1):r * (i + 2)] for i in range(n_units)]
        out.append((o * lax.rsqrt(jnp.mean(o * o, axis=-1, keepdims=True) + EPS) * nw * _silu(z), s_new))
    return out


def _gdn_prompt_kernel(n_seg, prev_ref, qkv_ref, z_ref, ba_ref, cw_ref, alog_ref, dtb_ref, nw_ref, mask_ref,
                       o_ref, s_ref, conv_ref, s_scr, u_scr, wq_scr, lhs_scr, egl_scr):
    seg = pl.program_id(1)
    n_pairs = H_A // 2

    @pl.when(seg == 0)
    def _():
        s_scr[...] = jnp.zeros_like(s_scr)

    cw = cw_ref[...]
    alog = alog_ref[...]
    dtb = dtb_ref[...]
    nw = nw_ref[...]
    mk = _Masks(mask_ref[0], mask_ref[1], mask_ref[2], tuple(mask_ref[3 + i] for i in range(PAIR_UNITS)))

    def prepare(it, carry):
        groups, where = [], []
        for j in range(PREP_CHUNKS):
            ci = it * PREP_CHUNKS + j
            r0 = pl.multiple_of(ci * CHUNK, CHUNK)
            for q in range(N_SEQ):
                before = jnp.where(ci > 0, qkv_ref[q, pl.ds(pl.multiple_of(jnp.maximum(r0 - SUBLANES, 0), SUBLANES),
                                                            SUBLANES), :],
                                   jnp.where(seg > 0, prev_ref[q], 0.0))
                y = _conv_silu(jnp.concatenate([before, qkv_ref[q, pl.ds(r0, CHUNK), :]], axis=0), cw)[SUBLANES:]
                beta_all, g_all = _gates(ba_ref[q, pl.ds(r0, CHUNK), :], alog, dtb)
                gc_all = _cumsum_rows(g_all, CHUNK)
                for pair in range(n_pairs):
                    heads = (2 * pair, 2 * pair + 1)
                    cols = lambda base: jnp.concatenate(
                        [y[:, base + h * LANES:base + (h + 1) * LANES] for h in heads], axis=0)
                    beta = jnp.concatenate([beta_all[:, h:h + 1] for h in heads], axis=0)
                    gcol = jnp.concatenate([gc_all[:, H_A + h:H_A + h + 1] for h in heads], axis=0)
                    groups.append((cols(0), cols(H_A * DK_A), cols(2 * H_A * DK_A), beta, gcol))
                    where.append((ci, q, pair))
        for (ci, q, pair), (u, wq, lhs, egl) in zip(where, _gdn_prepare(groups, CHUNK, mk)):
            u_scr[ci, q, pair] = u
            wq_scr[ci, q, pair] = wq
            lhs_scr[ci, q, pair] = lhs
            for i in range(PAIR_UNITS):
                egl_scr[ci, q, pair, i:i + 1, :] = egl[i]
        return carry

    lax.fori_loop(0, N_CHUNK // PREP_CHUNKS, prepare, 0)

    def scan(ci, carry):
        r0 = pl.multiple_of(ci * CHUNK, CHUNK)
        items, where = [], []
        for q in range(N_SEQ):
            z = z_ref[q, pl.ds(r0, CHUNK), :]
            for pair in range(n_pairs):
                heads = (2 * pair, 2 * pair + 1)
                zz = jnp.concatenate([z[:, h * DV_A:(h + 1) * DV_A] for h in heads], axis=0)
                egl = [egl_scr[ci, q, pair, i:i + 1, :] for i in range(PAIR_UNITS)]
                items.append((u_scr[ci, q, pair], wq_scr[ci, q, pair], lhs_scr[ci, q, pair], egl,
                              [s_scr[q, h] for h in heads], zz))
                where.append((q, heads))
        for (q, heads), (on, s_new) in zip(where, _gdn_apply(items, nw, CHUNK)):
            for i, h in enumerate(heads):
                o_ref[q, pl.ds(r0, CHUNK), h * DV_A:(h + 1) * DV_A] = on[i * CHUNK:(i + 1) * CHUNK]
                s_scr[q, h] = s_new[i]
        return carry

    lax.fori_loop(0, N_CHUNK, scan, 0)

    @pl.when(seg == n_seg - 1)
    def _():
        s_ref[:, 0] = s_scr[...]
        conv_ref[:, 0] = qkv_ref[:, SEG - SUBLANES:SEG, :]


def _gdn_prompt(proj, n_batch, seq, cw_all, alog_all, dtb_all, nw_all, layer):
    n_seg = seq // SEG
    seg8 = SEG // SUBLANES
    seq8 = seq // SUBLANES
    n_pairs = H_A // 2
    nb = n_batch // N_SEQ
    proj3 = proj.reshape(N_SEQ, nb * seq, PROJ_PAD)
    mk = _group_masks(CHUNK)
    masks = jnp.stack([mk.causal, mk.neg_strict, mk.eye, *mk.unit]).astype(F32)
    blk = lambda width, col: pl.BlockSpec((N_SEQ, SEG, width), lambda bb, s: (0, bb * n_seg + s, col))
    vec = lambda: pl.BlockSpec((None, 1, LANES), lambda bb, s: (layer, 0, 0))
    o, s_out, conv = pl.pallas_call(
        functools.partial(_gdn_prompt_kernel, n_seg),
        name="gdn_prompt",
        grid=(nb, n_seg),
        in_specs=[pl.BlockSpec((N_SEQ, SUBLANES, CONV_DIM),
                               lambda bb, s: (0, jnp.maximum(bb * seq8 + s * seg8 - 1, 0), 0)),
                  blk(CONV_DIM, 0), blk(A_WIDTH, COL_Z // A_WIDTH), blk(LANES, COL_BA // LANES),
                  pl.BlockSpec((None, CONV_WIDTH, CONV_DIM), lambda bb, s: (layer, 0, 0)),
                  vec(), vec(), vec(),
                  pl.BlockSpec((3 + PAIR_UNITS, LANES, LANES), lambda bb, s: (0, 0, 0))],
        out_specs=[blk(A_WIDTH, 0),
                   pl.BlockSpec((N_SEQ, 1, H_A, DK_A, DV_A), lambda bb, s: (0, bb, 0, 0, 0)),
                   pl.BlockSpec((N_SEQ, 1, SUBLANES, CONV_DIM), lambda bb, s: (0, bb, 0, 0))],
        out_shape=[jax.ShapeDtypeStruct((N_SEQ, nb * seq, A_WIDTH), F32),
                   jax.ShapeDtypeStruct((N_SEQ, nb, H_A, DK_A, DV_A), F32),
                   jax.ShapeDtypeStruct((N_SEQ, nb, SUBLANES, CONV_DIM), F32)],
        scratch_shapes=[pltpu.VMEM((N_SEQ, H_A, DK_A, DV_A), F32),
                        pltpu.VMEM((N_CHUNK, N_SEQ, n_pairs, LANES, DV_A), F32),
                        pltpu.VMEM((N_CHUNK, N_SEQ, n_pairs, 2 * LANES, DK_A), BF16),
                        pltpu.VMEM((N_CHUNK, N_SEQ, n_pairs, 3 * LANES, LANES), BF16),
                        pltpu.VMEM((N_CHUNK, N_SEQ, n_pairs, SUBLANES, LANES), F32)],
        compiler_params=_params(2),
    )(proj3, proj3, proj3, proj3, cw_all, alog_all, dtb_all, nw_all, masks)
    return (o.reshape(n_batch * seq, A_WIDTH), s_out.reshape(n_batch, H_A, DK_A, DV_A),
            conv.reshape(n_batch * SUBLANES, CONV_DIM))


def _token_rows(n_rows):
    row = lax.broadcasted_iota(jnp.int32, (n_rows, 1), 0) & (S_TILE - 1)
    return (row >= ROW0) & (row < ROW0 + DEC_SEQ)


def _gdn_sample_kernel(qkv_ref, hist_ref, z_ref, ba_ref, cw_ref, alog_ref, dtb_ref, nw_ref, s_in_ref, s_acc_ref,
                       o_ref, s_out_ref, conv_ref):
    del s_acc_ref
    rows = SB * S_TILE
    valid = _token_rows(rows)
    x = jnp.where(valid, qkv_ref[...], 0.0) + hist_ref[...]
    conv_ref[...] = pltpu.roll(x, rows - DEC_SEQ, axis=0)
    y = _conv_silu(x, cw_ref[...])
    beta_all, g_all = _gates(ba_ref[...], alog_ref[...], dtb_ref[...])
    beta_all = jnp.where(valid, beta_all, 0.0)
    gc_all = _cumsum_rows(jnp.where(valid, g_all, 0.0), S_TILE)
    z = z_ref[...]
    ym = jnp.where(valid, y, 0.0)

    all_units = [(b, h) for b in range(SB) for h in range(H_A)]
    per_group = LANES // S_TILE
    unit_groups = [all_units[i:i + per_group] for i in range(0, len(all_units), per_group)]

    def stack(units, src, base, width=LANES):
        return jnp.concatenate(
            [src[b * S_TILE:(b + 1) * S_TILE, base + h * width:base + (h + 1) * width] for b, h in units], axis=0)

    mk = _group_masks(S_TILE)
    prepared = _gdn_prepare(
        [(stack(us, y, 0), stack(us, ym, H_A * DK_A), stack(us, ym, 2 * H_A * DK_A),
          stack(us, beta_all, 0, 1), stack(us, gc_all, H_A, 1)) for us in unit_groups], S_TILE, mk)
    items = [(u, wq, lhs, egl, [s_in_ref[b, h] for b, h in us], stack(us, z, 0))
             for us, (u, wq, lhs, egl) in zip(unit_groups, prepared)]
    for us, (on, s_new) in zip(unit_groups, _gdn_apply(items, nw_ref[...], S_TILE)):
        for i, (b, h) in enumerate(us):
            o_ref[b * S_TILE:(b + 1) * S_TILE, h * DV_A:(h + 1) * DV_A] = on[i * S_TILE:(i + 1) * S_TILE]
            s_out_ref[b, h] = s_new[i]


def _gdn_sample(proj, hist, state_all, cw_all, alog_all, dtb_all, nw_all, layer, s_acc):
    rows = SB * S_TILE
    n_seq = proj.shape[0] // S_TILE
    depth = state_all.shape[0]
    vec = lambda: pl.BlockSpec((None, 1, LANES), lambda i: (layer, 0, 0))
    in_specs = [pl.BlockSpec((rows, CONV_DIM), lambda i: (i, 0)),
                pl.BlockSpec((None, rows, CONV_DIM), lambda i: (layer, i, 0)),
                pl.BlockSpec((rows, A_WIDTH), lambda i: (i, COL_Z // A_WIDTH)),
                pl.BlockSpec((rows, LANES), lambda i: (i, COL_BA // LANES)),
                pl.BlockSpec((None, CONV_WIDTH, CONV_DIM), lambda i: (layer, 0, 0)),
                vec(), vec(), vec(),
                pl.BlockSpec((None, SB, H_A, DK_A, DV_A), lambda i: (layer, i, 0, 0, 0))]
    in_specs.append(pl.BlockSpec(memory_space=pl.ANY))
    args = [proj, hist, proj, proj, cw_all, alog_all, dtb_all, nw_all, state_all, s_acc]
    aliases = {len(args) - 1: 1}
    return pl.pallas_call(
        _gdn_sample_kernel,
        name="gdn_sample",
        grid=(n_seq // SB,),
        in_specs=in_specs,
        out_specs=[pl.BlockSpec((rows, A_WIDTH), lambda i: (i, 0)),
                   pl.BlockSpec((None, SB, H_A, DK_A, DV_A), lambda i: (layer, i, 0, 0, 0)),
                   pl.BlockSpec((rows, CONV_DIM), lambda i: (i, 0))],
        out_shape=[jax.ShapeDtypeStruct((n_seq * S_TILE, A_WIDTH), F32),
                   jax.ShapeDtypeStruct((depth, n_seq, H_A, DK_A, DV_A), F32),
                   jax.ShapeDtypeStruct((n_seq * S_TILE, CONV_DIM), F32)],
        input_output_aliases=aliases,
        compiler_params=_params(1),
    )(*args)


def _swa_heads(items, mask, sink_ref, q_rows):
    lane_head = lax.broadcasted_iota(jnp.int32, (1, LANES), 1) >> int(math.log2(HD_B))
    chains = [(it, hk) for it in range(len(items)) for hk in range(H_KVB)]
    each = lambda f, *lists: [f(*a) for a in zip(*lists)]
    qcat = [jnp.concatenate([q[:, gi * LANES:(gi + 1) * LANES] for gi in range(GQA_GROUP)], axis=0)
            * (HD_B ** -0.5) for q, _, _ in items]
    sk_head = [jnp.concatenate(
        [jnp.broadcast_to(sink_ref[hk * GQA_GROUP + gi:hk * GQA_GROUP + gi + 1, 0:1], (q_rows, 1))
         for gi in range(GQA_GROUP)], axis=0) for hk in range(H_KVB)]
    sk = [sk_head[hk] for _, hk in chains]
    kh = [jnp.where(lane_head == hk, items[it][1], 0.0) for it, hk in chains]
    vh = [jnp.where(lane_head == hk, items[it][2], 0.0) for it, hk in chains]
    s = [jnp.where(mask, _dot_nt(qcat[it], kh[ch]), NEG) for ch, (it, _) in enumerate(chains)]
    m = each(lambda a, b: jnp.maximum(jnp.max(a, axis=-1, keepdims=True), b), s, sk)
    p = each(lambda a, b: jnp.exp(a - b), s, m)
    den = each(lambda a, b, d: jnp.sum(a, axis=-1, keepdims=True) + jnp.exp(b - d), p, sk, m)
    o = each(lambda a, b, d: _dot(a, b) * (1.0 / d), p, vh, den)
    outs = []
    for it in range(len(items)):
        tot = o[H_KVB * it]
        for hk in range(1, H_KVB):
            tot = tot + o[H_KVB * it + hk]
        outs.append(jnp.concatenate([tot[gi * q_rows:(gi + 1) * q_rows] for gi in range(GQA_GROUP)], axis=1))
    return outs


def _swa_prompt_kernel(q_ref, kc_ref, kp_ref, vc_ref, vp_ref, sink_ref, o_ref):
    n0 = pl.program_id(1) * SWA_QB
    keys = 2 * WINDOW
    k_ext = jnp.concatenate([kp_ref[...], kc_ref[...]], axis=0)
    v_ext = jnp.concatenate([vp_ref[...], vc_ref[...]], axis=0)
    c = lax.broadcasted_iota(jnp.int32, (keys, 2 * WINDOW), 0)
    r = lax.broadcasted_iota(jnp.int32, (keys, 2 * WINDOW), 1) & (WINDOW - 1)
    band = jnp.where((c > r) & (c <= r + WINDOW), 0.0, NEG)
    shift = int(math.log2(HD_B))
    lane_head = lax.broadcasted_iota(jnp.int32, (1, LANES), 1) >> shift
    row = lax.broadcasted_iota(jnp.int32, (LANES, 1), 0)
    row_head = row >> shift
    each = lambda f, *lists: [f(*a) for a in zip(*lists)]
    n_gp = GQA_GROUP // 2

    chains = [(j, hk, gp) for j in range(SWA_QB) for hk in range(H_KVB) for gp in range(n_gp)]
    bias = [jnp.where(c + n0 * WINDOW >= WINDOW, band, NEG)] + [band] * (SWA_QB - 1)
    kh, vth, qpair = {}, {}, {}
    for j in range(SWA_QB):
        k_all = k_ext[j * WINDOW:(j + 2) * WINDOW]
        vt = v_ext[j * WINDOW:(j + 2) * WINDOW].T
        qs = q_ref[j * WINDOW:(j + 1) * WINDOW, :] * (HD_B ** -0.5 * LOG2E)
        for hk in range(H_KVB):
            kh[j, hk] = jnp.where(lane_head == hk, k_all, 0.0).astype(BF16)
            vth[j, hk] = jnp.where(row_head == hk, vt, jnp.where(row == (1 - hk) * HD_B, 1.0, 0.0)).astype(BF16)
        for gp in range(n_gp):
            qpair[j, gp] = jnp.concatenate([qs[:, (2 * gp) * LANES:(2 * gp + 1) * LANES],
                                            qs[:, (2 * gp + 1) * LANES:(2 * gp + 2) * LANES]],
                                           axis=0).astype(BF16)
    ones_row = [(1 - hk) * HD_B for _, hk, _ in chains]
    sk = [jnp.concatenate([sink_ref[hk * GQA_GROUP + 2 * gp + i:hk * GQA_GROUP + 2 * gp + i + 1, :]
                           for i in range(2)], axis=1) * LOG2E for _, hk, gp in chains]
    st = [_dot_nt(kh[j, hk], qpair[j, gp]) + bias[j] for j, hk, gp in chains]
    m = each(lambda a, b: jnp.maximum(jnp.max(a, axis=0, keepdims=True), b), st, sk)
    pt = each(lambda a, b: jnp.exp2(a - b).astype(BF16), st, m)
    ot = [jnp.dot(vth[j, hk], pt[ch], preferred_element_type=F32) for ch, (j, hk, _) in enumerate(chains)]
    den = each(lambda a, i, b, d: a[i:i + 1, :] + jnp.exp2(b - d), ot, ones_row, sk, m)
    ot = [jnp.where(row_head == hk, ot[ch] * (1.0 / den[ch]), 0.0) for ch, (_, hk, _) in enumerate(chains)]
    for j in range(SWA_QB):
        tiles = []
        for gp in range(n_gp):
            tot = ot[chains.index((j, 0, gp))]
            for hk in range(1, H_KVB):
                tot = tot + ot[chains.index((j, hk, gp))]
            tiles += [tot[:, :WINDOW].T, tot[:, WINDOW:].T]
        o_ref[j * WINDOW:(j + 1) * WINDOW, :] = jnp.concatenate(tiles, axis=1)


def _swa_prompt(proj, n_batch, seq, sink_all, layer):
    nb = seq // WINDOW
    ns = nb // SWA_QB
    qrows = SWA_QB * WINDOW
    cur = lambda col: (lambda b, n: (b * ns + n, col))
    prev = lambda col: (lambda b, n: (b * nb + jnp.maximum(n * SWA_QB - 1, 0), col))
    return pl.pallas_call(
        _swa_prompt_kernel,
        name="swa_prompt",
        grid=(n_batch, ns),
        in_specs=[pl.BlockSpec((qrows, B_WIDTH), cur(COL_QB // B_WIDTH)),
                  pl.BlockSpec((qrows, KV_WIDTH), cur(COL_KB // KV_WIDTH)),
                  pl.BlockSpec((WINDOW, KV_WIDTH), prev(COL_KB // KV_WIDTH)),
                  pl.BlockSpec((qrows, KV_WIDTH), cur(COL_VB // KV_WIDTH)),
                  pl.BlockSpec((WINDOW, KV_WIDTH), prev(COL_VB // KV_WIDTH)),
                  pl.BlockSpec((None, H_QB, LANES), lambda b, n: (layer, 0, 0))],
        out_specs=pl.BlockSpec((qrows, B_WIDTH), lambda b, n: (b * ns + n, 0)),
        out_shape=jax.ShapeDtypeStruct((n_batch * seq, B_WIDTH), F32),
        compiler_params=_params(2),
    )(proj, proj, proj, proj, proj, sink_all)


def _shift_cache(cache, new_tile):
    rolled = pltpu.roll(cache, WINDOW - DEC_SEQ, axis=0)
    moved = pltpu.roll(new_tile, S_TILE - DEC_SEQ - ROW0, axis=0)
    row = lax.broadcasted_iota(jnp.int32, (S_TILE, 1), 0)
    tail = jnp.where(row >= S_TILE - DEC_SEQ, moved, rolled[WINDOW - S_TILE:])
    return jnp.concatenate([rolled[:WINDOW - S_TILE], tail], axis=0)


def _swa_sample_kernel(q_ref, k_ref, v_ref, kc_ref, vc_ref, sink_ref, k_acc_ref, v_acc_ref, o_ref, ko_ref, vo_ref):
    del k_acc_ref, v_acc_ref
    shape = (GQA_GROUP * S_TILE, 2 * WINDOW)
    r = (lax.broadcasted_iota(jnp.int32, shape, 0) & (S_TILE - 1)) - ROW0
    c = lax.broadcasted_iota(jnp.int32, shape, 1)
    j = c - WINDOW - ROW0
    mask = ((c < WINDOW) & (c > r)) | ((j >= 0) & (j < DEC_SEQ) & (j <= r))
    pad = jnp.zeros((WINDOW - S_TILE, KV_WIDTH), F32)
    items = []
    for b in range(SB):
        rows = slice(b * S_TILE, (b + 1) * S_TILE)
        k_new = k_ref[rows, :]
        v_new = v_ref[rows, :]
        items.append((q_ref[rows, :], jnp.concatenate([kc_ref[b], k_new, pad], axis=0),
                      jnp.concatenate([vc_ref[b], v_new, pad], axis=0)))
        ko_ref[b] = _shift_cache(kc_ref[b], k_new)
        vo_ref[b] = _shift_cache(vc_ref[b], v_new)
    for b, o in enumerate(_swa_heads(items, mask, sink_ref, S_TILE)):
        o_ref[b * S_TILE:(b + 1) * S_TILE, :] = o


def _swa_sample(proj, kc_all, vc_all, sink_all, layer, k_acc, v_acc):
    rows = SB * S_TILE
    n_seq = proj.shape[0] // S_TILE
    depth = kc_all.shape[0]
    cache = lambda: pl.BlockSpec((None, SB, WINDOW, KV_WIDTH), lambda i: (layer, i, 0, 0))
    in_specs = [pl.BlockSpec((rows, B_WIDTH), lambda i: (i, COL_QB // B_WIDTH)),
                pl.BlockSpec((rows, KV_WIDTH), lambda i: (i, COL_KB // KV_WIDTH)),
                pl.BlockSpec((rows, KV_WIDTH), lambda i: (i, COL_VB // KV_WIDTH)),
                cache(), cache(),
                pl.BlockSpec((None, H_QB, LANES), lambda i: (layer, 0, 0))]
    in_specs += [pl.BlockSpec(memory_space=pl.ANY)] * 2
    args = [proj, proj, proj, kc_all, vc_all, sink_all, k_acc, v_acc]
    aliases = {len(args) - 2: 1, len(args) - 1: 2}
    return pl.pallas_call(
        _swa_sample_kernel,
        name="swa_sample",
        grid=(n_seq // SB,),
        in_specs=in_specs,
        out_specs=[pl.BlockSpec((rows, B_WIDTH), lambda i: (i, 0)), cache(), cache()],
        out_shape=[jax.ShapeDtypeStruct((n_seq * S_TILE, B_WIDTH), F32),
                   jax.ShapeDtypeStruct((depth, n_seq, WINDOW, KV_WIDTH), F32),
                   jax.ShapeDtypeStruct((depth, n_seq, WINDOW, KV_WIDTH), F32)],
        input_output_aliases=aliases,
        compiler_params=_params(1),
    )(*args)


def kernel(x_prompt, x_sample, state_delta, state_conv, cache_swa_k, cache_swa_v, w_in, conv_w, a_log,
           dt_bias, norm_a_w, sinks, w_out, ln1_g, ln1_b, w_ffn_in, w_ffn_out, ln2_g, ln2_b):
    depth = w_in.shape[0]
    n_batch, seq, _ = x_prompt.shape
    n_dec, dec_seq, _ = x_sample.shape
    assert dec_seq == DEC_SEQ and seq % SEG == 0 and n_dec % SB == 0 and n_batch % N_SEQ == 0
    assert (n_batch * seq) % TM == 0 and (n_dec * S_TILE) % TM == 0
    assert cache_swa_k.shape[2] == WINDOW
    alpha = (2 * depth) ** 0.25

    c0 = CONV_DIM + A_WIDTH
    q0 = c0 + 2 * H_A
    order = jnp.array(QB_HEAD_ORDER)
    w_q = w_in[:, :, q0:q0 + B_WIDTH].reshape(depth, D_MODEL, H_QB, HD_B)[:, :, order]
    w_in_r = jnp.concatenate(
        [w_in[:, :, :c0], w_q.reshape(depth, D_MODEL, B_WIDTH), w_in[:, :, q0 + B_WIDTH:],
         w_in[:, :, c0:q0], jnp.zeros((depth, D_MODEL, LANES - 2 * H_A), w_in.dtype)], axis=-1).astype(BF16)
    w_out_bq = w_out[:, A_WIDTH:].reshape(depth, H_QB, HD_B, D_MODEL)[:, order].reshape(depth, B_WIDTH, D_MODEL)
    w_out_b = jnp.concatenate([w_out[:, :A_WIDTH], w_out_bq], axis=1).astype(BF16)
    w_ffn_in_b = w_ffn_in.astype(BF16)
    w_ffn_out_b = w_ffn_out.astype(BF16)
    lane_pad = lambda t: jnp.pad(t, ((0, 0), (H_A, LANES - 2 * H_A)))[:, None, :]
    alog_v = lane_pad(a_log)
    dtb_v = lane_pad(dt_bias)
    nw_v = norm_a_w[:, None, :]
    sink_v = jnp.broadcast_to(sinks[:, :, None], (depth, H_QB, LANES))
    ln1_g3, ln1_b3, ln2_g3, ln2_b3 = (t[:, None, :] for t in (ln1_g, ln1_b, ln2_g, ln2_b))

    hist = jnp.pad(state_conv, ((0, 0), (0, 0), (0, S_TILE - (CONV_WIDTH - 1)), (0, 0)))
    hist = hist.reshape(depth, n_dec * S_TILE, CONV_DIM)
    kc = cache_swa_k.reshape(depth, n_dec, WINDOW, KV_WIDTH)
    vc = cache_swa_v.reshape(depth, n_dec, WINDOW, KV_WIDTH)

    xp = x_prompt.reshape(n_batch * seq, D_MODEL)
    xs = x_sample.reshape(n_dec * DEC_SEQ, D_MODEL)

    outs = [[] for _ in range(8)]
    s_acc = jnp.zeros((depth, n_dec, H_A, DK_A, DV_A), F32)
    k_acc = jnp.zeros((depth, n_dec, WINDOW, KV_WIDTH), F32)
    v_acc = jnp.zeros((depth, n_dec, WINDOW, KV_WIDTH), F32)
    for l in range(depth):
        proj_p, proj_s = _in_proj(xp, xs, w_in_r, l)
        oa_p, s_p, conv_p = _gdn_prompt(proj_p, n_batch, seq, conv_w, alog_v, dtb_v, nw_v, l)
        ob_p = _swa_prompt(proj_p, n_batch, seq, sink_v, l)
        oa_s, s_acc, conv_s = _gdn_sample(proj_s, hist, state_delta, conv_w, alog_v, dtb_v, nw_v, l, s_acc)
        ob_s, k_acc, v_acc = _swa_sample(proj_s, kc, vc, sink_v, l, k_acc, v_acc)
        xp, xs = _mix_ffn(xp, xs, oa_p, oa_s, ob_p, ob_s, w_out_b, ln1_g3, ln1_b3, w_ffn_in_b, w_ffn_out_b,
                          ln2_g3, ln2_b3, l, alpha)
        proj3 = proj_p.reshape(n_batch, seq, PROJ_PAD)
        outs[0].append(s_p)
        outs[1].append(conv_p.reshape(n_batch, SUBLANES, CONV_DIM)[:, SUBLANES - (CONV_WIDTH - 1):])
        outs[2].append(proj3[:, seq - WINDOW:, COL_KB:COL_KB + KV_WIDTH].reshape(n_batch, WINDOW, H_KVB, HD_B))
        outs[3].append(proj3[:, seq - WINDOW:, COL_VB:COL_VB + KV_WIDTH].reshape(n_batch, WINDOW, H_KVB, HD_B))
        outs[5].append(conv_s.reshape(n_dec, S_TILE, CONV_DIM)[:, :CONV_WIDTH - 1])

    y_prompt = xp.reshape(n_batch, seq, D_MODEL)
    y_sample = xs.reshape(n_dec, DEC_SEQ, D_MODEL)
    stacked = [jnp.stack(o) if o else None for o in outs]
    stacked[4] = s_acc
    stacked[6] = k_acc.reshape(depth, n_dec, WINDOW, H_KVB, HD_B)
    stacked[7] = v_acc.reshape(depth, n_dec, WINDOW, H_KVB, HD_B)
    return (y_prompt, y_sample) + tuple(stacked)
```
